```python
import math
import jax, jax.numpy as jnp
from jax import lax
import numpy as np

D_MODEL = 2048
BATCH = 1
SEQ = 8192
DEPTH = 2

N_A = DEPTH // 2
N_B = DEPTH - N_A
CONV_K = 31
FFN_CONV_K = 3
D_FF = 5632
GROUPS = ((128, 1), (512, 4), (2048, 16))
N_GROUPS = len(GROUPS)
HEADS_PER_GROUP = 8
HEAD_DIM = 128
Q_WIDTH = N_GROUPS * HEADS_PER_GROUP * HEAD_DIM
O_WIDTH = HEADS_PER_GROUP * HEAD_DIM
ROT_DIM = HEAD_DIM // 4
ROPE_THETA = 500000.0
BLK = 128
EPS = 1e-6
NEG = -1e30

kernel_name = "yoco_conformer_dilated_hybrid"


def rms_norm(x, g):
    xf = x.astype(jnp.float32)
    y = xf * lax.rsqrt(jnp.mean(xf * xf, axis=-1, keepdims=True) + EPS)
    return (y * g.astype(jnp.float32)).astype(x.dtype)


def layer_norm(x, g, b):
    xf = x.astype(jnp.float32)
    mu = jnp.mean(xf, axis=-1, keepdims=True)
    var = jnp.mean(jnp.square(xf - mu), axis=-1, keepdims=True)
    y = (xf - mu) * lax.rsqrt(var + EPS)
    return (y * g.astype(jnp.float32) + b.astype(jnp.float32)).astype(x.dtype)


def modulate(x, g, shift, scale):
    return rms_norm(x, g) * (1 + scale[:, None, :]) + shift[:, None, :]


def causal_dwconv(x, w, b):
    k, ch = w.shape
    y = lax.conv_general_dilated(
        x, w[:, None, :].astype(x.dtype), window_strides=(1,), padding=[(k - 1, 0)],
        dimension_numbers=("NWC", "WIO", "NWC"), feature_group_count=ch)
    return y + b


def rope_partial(x, positions):
    inv_freq = ROPE_THETA ** (-jnp.arange(0, ROT_DIM, 2, dtype=jnp.float32) / ROT_DIM)
    ang = positions.astype(jnp.float32)[..., None] * inv_freq
    ang = ang.reshape(ang.shape[:2] + (1,) * (x.ndim - 3) + ang.shape[-1:])
    cos, sin = jnp.cos(ang), jnp.sin(ang)
    xf = x.astype(jnp.float32)
    x1 = xf[..., : ROT_DIM // 2]
    x2 = xf[..., ROT_DIM // 2: ROT_DIM]
    out = jnp.concatenate([x1 * cos - x2 * sin, x2 * cos + x1 * sin, xf[..., ROT_DIM:]], axis=-1)
    return out.astype(x.dtype)


def conformer_conv(h, pw1_w, pw1_b, dw_w, dw_b, ln_g, ln_b, pw2_w, pw2_b):
    u = h @ pw1_w + pw1_b
    a, gt = jnp.split(u, 2, axis=-1)
    u = a * jax.nn.sigmoid(gt)
    u = causal_dwconv(u, dw_w, dw_b)
    u = jax.nn.silu(layer_norm(u, ln_g, ln_b))
    return u @ pw2_w + pw2_b


def conv_ffn(h, up_w, dw_w, dw_b, down_w):
    u = h @ up_w
    gt, val = jnp.split(u, 2, axis=-1)
    gt = causal_dwconv(gt, dw_w, dw_b)
    return (jax.nn.silu(gt) * val) @ down_w


def shared_kv(x, c, kv_mod_w, kv_mod_b, kv_norm_g, w_kv, k_norm_g, positions):
    b, s, _ = x.shape
    m = jax.nn.silu(c) @ kv_mod_w + kv_mod_b
    shift, scale = jnp.split(m, 2, axis=-1)
    h = modulate(x, kv_norm_g, shift, scale)
    kv = (h @ w_kv).reshape(b, s, 2, N_GROUPS, HEADS_PER_GROUP, HEAD_DIM)
    k = rope_partial(rms_norm(kv[:, :, 0], k_norm_g), positions)
    v = kv[:, :, 1]
    return k, v


def dilated_band_attn(q, k, v, span, r):
    b, s, h, dh = q.shape
    chunk = r * BLK
    s_pad = -(-s // chunk) * chunk
    nb = s_pad // chunk
    padw = ((0, 0), (0, s_pad - s), (0, 0), (0, 0))
    split = lambda t: jnp.pad(t, padw).reshape(b, nb, BLK, r, h, dh)
    qb, kb, vb = split(q), split(k), split(v)
    shift_prev = lambda t: jnp.concatenate([jnp.zeros_like(t[:, :1]), t[:, :-1]], axis=1)
    kcat = jnp.concatenate([shift_prev(kb), kb], axis=2)
    vcat = jnp.concatenate([shift_prev(vb), vb], axis=2)
    sc = jnp.einsum("bnqrhd,bnkrhd->bnrhqk", qb, kcat,
                    preferred_element_type=jnp.float32) * (1.0 / math.sqrt(dh))
    qi = jnp.arange(BLK)[:, None]
    kj = jnp.arange(2 * BLK)[None, :]
    dist = qi + BLK - kj
    band = (dist >= 0) & (dist <= span)
    kpos = jnp.arange(nb)[:, None, None] * BLK + kj[None] - BLK
    mask = band[None] & (kpos >= 0)
    sc = jnp.where(mask[None, :, None, None], sc, NEG)
    lse = jax.nn.logsumexp(sc, axis=-1)
    p = jnp.exp(sc - lse[..., None])
    o = jnp.einsum("bnrhqk,bnkrhd->bnqrhd", p.astype(v.dtype), vcat)
    o = o.reshape(b, s_pad, h, dh)[:, :s]
    lse = lse.transpose(0, 1, 4, 2, 3).reshape(b, s_pad, h)[:, :s]
    return o, lse


def dilated_mixture_attn(h, k, v, w_q, q_norm_g, w_o, positions):
    b, s, _ = h.shape
    q = (h @ w_q).reshape(b, s, N_GROUPS, HEADS_PER_GROUP, HEAD_DIM)
    q = rope_partial(rms_norm(q, q_norm_g), positions)
    outs, lses = [], []
    for g, (window, dil) in enumerate(GROUPS):
        o_g, lse_g = dilated_band_attn(q[:, :, g], k[:, :, g], v[:, :, g], window // dil, dil)
        outs.append(o_g)
        lses.append(lse_g)
    alpha = jax.nn.softmax(jnp.stack(lses, axis=0), axis=0)
    o = jnp.einsum("gbsh,gbshd->bshd", alpha, jnp.stack(outs, 0).astype(jnp.float32))
    return o.astype(h.dtype).reshape(b, s, O_WIDTH) @ w_o


def setup_inputs(seed: int = 0) -> dict:
    key = jax.random.key(seed)
    ks = iter(jax.random.split(key, 40))
    D = D_MODEL
    nrm = lambda shape, scale: jax.random.normal(next(ks), shape, jnp.float32) * scale
    gain = lambda shape: 1.0 + nrm(shape, 0.02)
    return {
        "x": nrm((BATCH, SEQ, D), 1.0),
        "c": nrm((BATCH, D), 1.0),
        "positions": jnp.broadcast_to(jnp.arange(SEQ, dtype=jnp.int32)[None], (BATCH, SEQ)),
        "mod_w": nrm((DEPTH, D, 6 * D), 0.5 * D ** -0.5),
        "mod_b": nrm((DEPTH, 6 * D), 0.01),
        "norm_mix_g": gain((DEPTH, D)),
        "norm_ffn_g": gain((DEPTH, D)),
        "conv_pw1_w": nrm((N_A, D, 2 * D), D ** -0.5),
        "conv_pw1_b": nrm((N_A, 2 * D), 0.01),
        "conv_dw_w": nrm((N_A, CONV_K, D), CONV_K ** -0.5),
        "conv_dw_b": nrm((N_A, D), 0.01),
        "conv_ln_g": gain((N_A, D)),
        "conv_ln_b": nrm((N_A, D), 0.01),
        "conv_pw2_w": nrm((N_A, D, D), D ** -0.5),
        "conv_pw2_b": nrm((N_A, D), 0.01),
        "kv_mod_w": nrm((D, 2 * D), 0.5 * D ** -0.5),
        "kv_mod_b": nrm((2 * D,), 0.01),
        "kv_norm_g": gain((D,)),
        "w_kv": nrm((D, 2 * Q_WIDTH), D ** -0.5),
        "k_norm_g": gain((HEAD_DIM,)),
        "w_q": nrm((N_B, D, Q_WIDTH), D ** -0.5),
        "q_norm_g": gain((N_B, HEAD_DIM)),
        "w_o": nrm((N_B, O_WIDTH, D), O_WIDTH ** -0.5),
        "ffn_up_w": nrm((DEPTH, D, 2 * D_FF), D ** -0.5),
        "ffn_dw_w": nrm((DEPTH, FFN_CONV_K, D_FF), FFN_CONV_K ** -0.5),
        "ffn_dw_b": nrm((DEPTH, D_FF), 0.01),
        "ffn_down_w": nrm((DEPTH, D_FF, D), D_FF ** -0.5),
    }


def reference(x, c, positions, mod_w, mod_b, norm_mix_g, norm_ffn_g,
              conv_pw1_w, conv_pw1_b, conv_dw_w, conv_dw_b, conv_ln_g, conv_ln_b,
              conv_pw2_w, conv_pw2_b, kv_mod_w, kv_mod_b, kv_norm_g, w_kv, k_norm_g,
              w_q, q_norm_g, w_o, ffn_up_w, ffn_dw_w, ffn_dw_b, ffn_down_w):
    k_sh = v_sh = None
    for l in range(DEPTH):
        m = jax.nn.silu(c) @ mod_w[l] + mod_b[l]
        sh_m, sc_m, g_m, sh_f, sc_f, g_f = jnp.split(m, 6, axis=-1)
        if l < N_A:
            h = modulate(x, norm_mix_g[l], sh_m, sc_m)
            y = conformer_conv(h, conv_pw1_w[l], conv_pw1_b[l], conv_dw_w[l], conv_dw_b[l],
                               conv_ln_g[l], conv_ln_b[l], conv_pw2_w[l], conv_pw2_b[l])
        else:
            if l == N_A:
                k_sh, v_sh = shared_kv(x, c, kv_mod_w, kv_mod_b, kv_norm_g, w_kv,
                                       k_norm_g, positions)
            j = l - N_A
            h = modulate(x, norm_mix_g[l], sh_m, sc_m)
            y = dilated_mixture_attn(h, k_sh, v_sh, w_q[j], q_norm_g[j], w_o[j], positions)
        x = x + g_m[:, None, :] * y
        h = modulate(x, norm_ffn_g[l], sh_f, sc_f)
        x = x + g_f[:, None, :] * conv_ffn(h, ffn_up_w[l], ffn_dw_w[l], ffn_dw_b[l], ffn_down_w[l])
    return x
```

```python
import functools
import math

import jax
import jax.numpy as jnp
from jax import lax
from jax.experimental import pallas as pl
from jax.experimental.pallas import tpu as pltpu

D_MODEL = 2048
SEQ = 8192
CONV_K = 31
FFN_CONV_K = 3
D_FF = 5632
GROUP_DILATIONS = (1, 4, 16)
GROUP_SPANS = (128, 128, 128)
N_GROUPS = 3
HEADS_PER_GROUP = 8
HEAD_DIM = 128
Q_WIDTH = N_GROUPS * HEADS_PER_GROUP * HEAD_DIM
O_WIDTH = HEADS_PER_GROUP * HEAD_DIM
ROT_DIM = HEAD_DIM // 4
ROPE_THETA = 500000.0
BLK = 128
EPS = 1e-6
NEG = -1e30

V7X_LANES = 128
V7X_SUBLANES = 8
V7X_BF16_ROWS_PER_VREG = 16
V7X_VMEM_BYTES = 64 * 1024 * 1024
VMEM_LIMIT = 56 * 1024 * 1024

TM = 1024
TN = 256
ROW_CHUNK = 64

F32 = jnp.float32
BF16 = jnp.bfloat16


def _params(n_axes):
    return pltpu.CompilerParams(
        dimension_semantics=("arbitrary",) * n_axes,
        vmem_limit_bytes=VMEM_LIMIT)


def _sigmoid(x):
    return 1.0 / (1.0 + jnp.exp(-x))


def _silu(x):
    return x * _sigmoid(x)


def _matvec_kernel(c_ref, w_ref, b_ref, o_ref, sb_ref, *, k_dim, tn):
    first = (pl.program_id(0) == 0) & (pl.program_id(1) == 0)

    @pl.when(first)
    def _():
        c = c_ref[...]
        sb_ref[...] = jnp.broadcast_to(_silu(c), (k_dim, V7X_LANES))

    n_groups = tn // V7X_LANES

    def body(t, accs):
        r0 = pl.multiple_of(t * ROW_CHUNK, ROW_CHUNK)
        s = sb_ref[pl.ds(r0, ROW_CHUNK), :]
        w = w_ref[pl.ds(r0, ROW_CHUNK), :]
        new = []
        for g in range(n_groups):
            p = w[:, g * V7X_LANES:(g + 1) * V7X_LANES] * s
            a = accs[g]
            for u in range(ROW_CHUNK // V7X_SUBLANES):
                a = a + p[u * V7X_SUBLANES:(u + 1) * V7X_SUBLANES, :]
            new.append(a)
        return tuple(new)

    init = tuple(jnp.zeros((V7X_SUBLANES, V7X_LANES), F32) for _ in range(n_groups))
    accs = lax.fori_loop(0, k_dim // ROW_CHUNK, body, init)
    row = jnp.concatenate([jnp.sum(a, axis=0, keepdims=True) for a in accs], axis=1)
    o_ref[...] = row + b_ref[...]


def _mod_matvec(c_col, w, b):
    n_l, k_dim, n = w.shape
    tn = 1024
    return pl.pallas_call(
        functools.partial(_matvec_kernel, k_dim=k_dim, tn=tn),
        grid=(n_l, n // tn),
        in_specs=[
            pl.BlockSpec((k_dim, 1), lambda l, j: (0, 0)),
            pl.BlockSpec((None, k_dim, tn), lambda l, j: (l, 0, j)),
            pl.BlockSpec((None, 1, tn), lambda l, j: (l, 0, j)),
        ],
        out_specs=pl.BlockSpec((None, 1, tn), lambda l, j: (l, 0, j)),
        out_shape=jax.ShapeDtypeStruct((n_l, 1, n), F32),
        scratch_shapes=[pltpu.VMEM((k_dim, V7X_LANES), F32)],
        compiler_params=_params(2),
        name="mod_matvec",
    )(c_col, w, b)


MOD_ROWS = V7X_BF16_ROWS_PER_VREG


def _modulate_rows(x_ref, g_ref, shift_ref, scale_ref, h_ref, *, rows, h_row0=0):
    def body(t, carry):
        r0 = pl.multiple_of(t * MOD_ROWS, MOD_ROWS)
        x = x_ref[pl.ds(r0, MOD_ROWS), :]
        ms = jnp.mean(x * x, axis=-1, keepdims=True)
        y = x * lax.rsqrt(ms + EPS)
        h = (y * g_ref[...]) * (1.0 + scale_ref[...]) + shift_ref[...]
        h0 = pl.multiple_of(h_row0 + t * MOD_ROWS, MOD_ROWS)
        h_ref[pl.ds(h0, MOD_ROWS), :] = h.astype(BF16)
        return carry

    lax.fori_loop(0, rows // MOD_ROWS, body, 0)


def _pw1_glu_kernel(x_ref, g_ref, sh_ref, sc_ref, wa_ref, wg_ref, ba_ref, bg_ref,
                    o_ref, h_ref):
    @pl.when(pl.program_id(1) == 0)
    def _():
        _modulate_rows(x_ref, g_ref, sh_ref, sc_ref, h_ref, rows=TM)

    h = h_ref[...]
    a = jnp.dot(h, wa_ref[...].astype(BF16), preferred_element_type=F32) + ba_ref[...]
    gt = jnp.dot(h, wg_ref[...].astype(BF16), preferred_element_type=F32) + bg_ref[...]
    o_ref[...] = a * _sigmoid(gt)


def _pw1_glu(x, norm_g, shift, scale, w, b):
    s, d = x.shape
    nj = d // TN
    vec = lambda col: pl.BlockSpec((1, d), lambda i, j: (0, col))
    return pl.pallas_call(
        _pw1_glu_kernel,
        grid=(s // TM, nj),
        in_specs=[
            pl.BlockSpec((TM, d), lambda i, j: (i, 0)),
            pl.BlockSpec((1, d), lambda i, j: (0, 0)),
            vec(shift[1]), vec(scale[1]),
            pl.BlockSpec((d, TN), lambda i, j: (0, j)),
            pl.BlockSpec((d, TN), lambda i, j: (0, nj + j)),
            pl.BlockSpec((1, TN), lambda i, j: (0, j)),
            pl.BlockSpec((1, TN), lambda i, j: (0, nj + j)),
        ],
        out_specs=pl.BlockSpec((TM, TN), lambda i, j: (i, j)),
        out_shape=jax.ShapeDtypeStruct((s, d), F32),
        scratch_shapes=[pltpu.VMEM((TM, d), BF16)],
        compiler_params=_params(2),
        name="pw1_glu",
    )(x, norm_g, shift[0], scale[0], w, w, b, b)


CONV_HALO = 32
CONV_ROWS = 32


def _dwconv_ln_rows(gbuf_ref, dw_ref, db_ref, lg_ref, lb_ref, tmp_ref, hb_ref, *, rows, d):
    off = CONV_HALO - (CONV_K - 1)

    def body(t, carry):
        r0 = pl.multiple_of(t * CONV_ROWS, CONV_ROWS)
        for slab in range(d // V7X_LANES):
            lanes = slice(slab * V7X_LANES, (slab + 1) * V7X_LANES)
            acc = jnp.broadcast_to(db_ref[:, lanes], (CONV_ROWS, V7X_LANES))
            for k in range(CONV_K):
                tap = gbuf_ref[slab, pl.ds(r0 + off + k, CONV_ROWS, stride=1), :]
                acc = acc + tap * dw_ref[k:k + 1, lanes]
            tmp_ref[:, lanes] = acc
        u = tmp_ref[...]
        mu = jnp.mean(u, axis=-1, keepdims=True)
        uc = u - mu
        var = jnp.mean(uc * uc, axis=-1, keepdims=True)
        y = uc * lax.rsqrt(var + EPS) * lg_ref[...] + lb_ref[...]
        hb_ref[pl.ds(r0, CONV_ROWS), :] = _silu(y).astype(BF16)
        return carry

    lax.fori_loop(0, rows // CONV_ROWS, body, 0)


def _conv_pw2_kernel(g_ref, halo_ref, dw_ref, db_ref, lg_ref, lb_ref, w_ref, b_ref,
                     x_ref, gate_ref, o_ref, gbuf_ref, tmp_ref, hb_ref, *, d):
    i = pl.program_id(0)

    @pl.when(pl.program_id(1) == 0)
    def _():
        for slab in range(d // V7X_LANES):
            lanes = slice(slab * V7X_LANES, (slab + 1) * V7X_LANES)
            halo = halo_ref[:, lanes]
            gbuf_ref[slab, 0:CONV_HALO, :] = jnp.where(i == 0, jnp.zeros_like(halo), halo)
            gbuf_ref[slab, CONV_HALO:, :] = g_ref[:, lanes]
        _dwconv_ln_rows(gbuf_ref, dw_ref, db_ref, lg_ref, lb_ref, tmp_ref, hb_ref,
                        rows=TM, d=d)

    y = jnp.dot(hb_ref[...], w_ref[...].astype(BF16), preferred_element_type=F32)
    o_ref[...] = x_ref[...] + gate_ref[...] * (y + b_ref[...])


def _conv_pw2(glu, dw_w, dw_b, ln_g, ln_b, w, b, x, gate):
    s, d = glu.shape
    halo_blocks_per_tile = TM // CONV_HALO
    gate_col = gate[1] * (d // TN)
    return pl.pallas_call(
        functools.partial(_conv_pw2_kernel, d=d),
        grid=(s // TM, d // TN),
        in_specs=[
            pl.BlockSpec((TM, d), lambda i, j: (i, 0)),
            pl.BlockSpec((CONV_HALO, d),
                         lambda i, j: (jnp.maximum(i * halo_blocks_per_tile - 1, 0), 0)),
            pl.BlockSpec((CONV_K, d), lambda i, j: (0, 0)),
            pl.BlockSpec((1, d), lambda i, j: (0, 0)),
            pl.BlockSpec((1, d), lambda i, j: (0, 0)),
            pl.BlockSpec((1, d), lambda i, j: (0, 0)),
            pl.BlockSpec((d, TN), lambda i, j: (0, j)),
            pl.BlockSpec((1, TN), lambda i, j: (0, j)),
            pl.BlockSpec((TM, TN), lambda i, j: (i, j)),
            pl.BlockSpec((1, TN), lambda i, j: (0, gate_col + j)),
        ],
        out_specs=pl.BlockSpec((TM, TN), lambda i, j: (i, j)),
        out_shape=jax.ShapeDtypeStruct((s, d), F32),
        scratch_shapes=[
            pltpu.VMEM((d // V7X_LANES, TM + CONV_HALO, V7X_LANES), F32),
            pltpu.VMEM((CONV_ROWS, d), F32),
            pltpu.VMEM((TM, d), BF16),
        ],
        compiler_params=_params(2),
        name="dwconv_pw2",
    )(glu, glu, dw_w, dw_b, ln_g, ln_b, w, b, x, gate[0])


FFN_HALO = V7X_BF16_ROWS_PER_VREG
TF = 256
FFN_OUT_LANES = 512


def _ffn_kernel(x_ref, xh_ref, g_ref, sh_ref, sc_ref, gate_ref, wg_ref, wv_ref,
                dw_ref, db_ref, wd_ref, o_ref, h_ref, u_ref):
    i = pl.program_id(0)

    @pl.when(pl.program_id(1) == 0)
    def _():
        _modulate_rows(xh_ref, g_ref, sh_ref, sc_ref, h_ref, rows=FFN_HALO)
        _modulate_rows(x_ref, g_ref, sh_ref, sc_ref, h_ref, rows=TM, h_row0=FFN_HALO)
        o_ref[...] = x_ref[...]

    u = jnp.dot(h_ref[...], wg_ref[...].astype(BF16), preferred_element_type=F32)
    u_ref[...] = u
    @pl.when(i == 0)
    def _():
        u_ref[0:FFN_HALO, :] = jnp.zeros((FFN_HALO, TF), F32)

    gt = db_ref[...] + dw_ref[2:3, :] * u_ref[pl.ds(FFN_HALO, TM), :]
    gt = gt + dw_ref[1:2, :] * u_ref[pl.ds(FFN_HALO - 1, TM), :]
    gt = gt + dw_ref[0:1, :] * u_ref[pl.ds(FFN_HALO - 2, TM), :]
    val = jnp.dot(h_ref[pl.ds(FFN_HALO, TM), :], wv_ref[...].astype(BF16),
                  preferred_element_type=F32)
    act = (_silu(gt) * val).astype(BF16)
    for c0 in range(0, o_ref.shape[1], FFN_OUT_LANES):
        cols = slice(c0, c0 + FFN_OUT_LANES)
        y = jnp.dot(act, wd_ref[:, cols].astype(BF16), preferred_element_type=F32)
        o_ref[:, cols] += gate_ref[:, cols] * y


def _conv_ffn(x, norm_g, shift, scale, gate, up_w, dw_w, dw_b, down_w):
    s, d = x.shape
    f = down_w.shape[0]
    nf = f // TF
    halo_blocks_per_tile = TM // FFN_HALO
    vec = lambda col: pl.BlockSpec((1, d), lambda i, j: (0, col))
    return pl.pallas_call(
        _ffn_kernel,
        grid=(s // TM, nf),
        in_specs=[
            pl.BlockSpec((TM, d), lambda i, j: (i, 0), pipeline_mode=pl.Buffered(1)),
            pl.BlockSpec((FFN_HALO, d),
                         lambda i, j: (jnp.maximum(i * halo_blocks_per_tile - 1, 0), 0)),
            pl.BlockSpec((1, d), lambda i, j: (0, 0)),
            vec(shift[1]), vec(scale[1]), vec(gate[1]),
            pl.BlockSpec((d, TF), lambda i, j: (0, j)),
            pl.BlockSpec((d, TF), lambda i, j: (0, nf + j)),
            pl.BlockSpec((FFN_CONV_K, TF), lambda i, j: (0, j)),
            pl.BlockSpec((1, TF), lambda i, j: (0, j)),
            pl.BlockSpec((TF, d), lambda i, j: (j, 0)),
        ],
        out_specs=pl.BlockSpec((TM, d), lambda i, j: (i, 0)),
        out_shape=jax.ShapeDtypeStruct((s, d), F32),
        scratch_shapes=[
            pltpu.VMEM((TM + FFN_HALO, d), BF16),
            pltpu.VMEM((TM + FFN_HALO, TF), F32),
        ],
        compiler_params=_params(2),
        name="conv_ffn",
    )(x, x, norm_g, shift[0], scale[0], gate[0], up_w, up_w, dw_w, dw_b, down_w)


def _rope_table_kernel(pos_ref, freq_ref, cos_ref, sin_lo_ref, sin_hi_ref):
    pos = pos_ref[...].astype(F32)
    ang = pos * freq_ref[...]
    lane = lax.broadcasted_iota(jnp.int32, ang.shape, 1)
    c = jnp.cos(ang)
    sn = jnp.sin(ang)
    half = ROT_DIM // 2
    cos_ref[...] = jnp.where(lane < ROT_DIM, c, 1.0)
    sin_lo_ref[...] = jnp.where(lane < half, -sn, 0.0)
    sin_hi_ref[...] = jnp.where((lane >= half) & (lane < ROT_DIM), sn, 0.0)


def _rope_tables(positions):
    s = positions.shape[0]
    rows = 1024
    inv_freq = ROPE_THETA ** (-jnp.arange(0, ROT_DIM, 2, dtype=F32) / ROT_DIM)
    lane_freq = jnp.concatenate(
        [inv_freq, inv_freq, jnp.zeros((HEAD_DIM - ROT_DIM,), F32)])[None, :]
    out = jax.ShapeDtypeStruct((s, HEAD_DIM), F32)
    spec = pl.BlockSpec((rows, HEAD_DIM), lambda i: (i, 0))
    return pl.pallas_call(
        _rope_table_kernel,
        grid=(s // rows,),
        in_specs=[pl.BlockSpec((rows, 1), lambda i: (i, 0)),
                  pl.BlockSpec((1, HEAD_DIM), lambda i: (0, 0))],
        out_specs=[spec, spec, spec],
        out_shape=[out, out, out],
        compiler_params=_params(1),
        name="rope_tables",
    )(positions.reshape(s, 1), lane_freq)


def _head_norm_rope(y, hg, cos, sin_lo, sin_hi):
    outs = []
    for h0 in range(0, y.shape[1], HEAD_DIM):
        q = y[:, h0:h0 + HEAD_DIM]
        ms = jnp.mean(q * q, axis=-1, keepdims=True)
        qn = q * lax.rsqrt(ms + EPS) * hg
        hi_to_lo = pltpu.roll(qn, HEAD_DIM - ROT_DIM // 2, axis=1)
        lo_to_hi = pltpu.roll(qn, ROT_DIM // 2, axis=1)
        outs.append(qn * cos + hi_to_lo * sin_lo + lo_to_hi * sin_hi)
    return jnp.concatenate(outs, axis=1)


def _proj_rope_kernel(x_ref, g_ref, sh_ref, sc_ref, w_ref, hg_ref, cos_ref, slo_ref,
                      shi_ref, o_ref, h_ref):
    @pl.when(pl.program_id(1) == 0)
    def _():
        _modulate_rows(x_ref, g_ref, sh_ref, sc_ref, h_ref, rows=TM)

    y = jnp.dot(h_ref[...], w_ref[...].astype(BF16), preferred_element_type=F32)
    o_ref[...] = _head_norm_rope(y, hg_ref[...], cos_ref[...], slo_ref[...],
                                 shi_ref[...]).astype(BF16)


def _proj_plain_kernel(x_ref, g_ref, sh_ref, sc_ref, w_ref, o_ref, h_ref):
    @pl.when(pl.program_id(1) == 0)
    def _():
        _modulate_rows(x_ref, g_ref, sh_ref, sc_ref, h_ref, rows=TM)

    y = jnp.dot(h_ref[...], w_ref[...].astype(BF16), preferred_element_type=F32)
    o_ref[...] = y.astype(BF16)


def _proj(x, norm_g, shift, scale, w, w_col0, n_out, head_g=None, tables=None):
    s, d = x.shape
    vec = lambda col: pl.BlockSpec((1, d), lambda i, j: (0, col))
    in_specs = [
        pl.BlockSpec((TM, d), lambda i, j: (i, 0)),
        pl.BlockSpec((1, d), lambda i, j: (0, 0)),
        vec(shift[1]), vec(scale[1]),
        pl.BlockSpec((d, TN), lambda i, j: (0, w_col0 + j)),
    ]
    args = [x, norm_g, shift[0], scale[0], w]
    if head_g is not None:
        tab = pl.BlockSpec((TM, HEAD_DIM), lambda i, j: (i, 0))
        in_specs += [pl.BlockSpec((1, HEAD_DIM), lambda i, j: (0, 0)), tab, tab, tab]
        args += [head_g, *tables]
        body, name = _proj_rope_kernel, "proj_norm_rope"
    else:
        body, name = _proj_plain_kernel, "proj_plain"
    return pl.pallas_call(
        body,
        grid=(s // TM, n_out // TN),
        in_specs=in_specs,
        out_specs=pl.BlockSpec((TM, TN), lambda i, j: (i, j)),
        out_shape=jax.ShapeDtypeStruct((s, n_out), BF16),
        scratch_shapes=[pltpu.VMEM((TM, d), BF16)],
        compiler_params=_params(2),
        name=name,
    )(*args)


def _band_attn_kernel(q_ref, kp_ref, kc_ref, vp_ref, vc_ref, o_ref, lse_ref, *, span):
    n = pl.program_id(1)
    qi = lax.broadcasted_iota(jnp.int32, (BLK, 2 * BLK), 0)
    kj = lax.broadcasted_iota(jnp.int32, (BLK, 2 * BLK), 1)
    dist = qi + BLK - kj
    mask = (dist >= 0) & (dist <= span) & ((n > 0) | (kj >= BLK))
    lane = lax.broadcasted_iota(jnp.int32, (BLK, V7X_LANES), 1)
    scale = 1.0 / math.sqrt(HEAD_DIM)
    lse_tile = jnp.zeros((BLK, V7X_LANES), F32)
    for h in range(HEADS_PER_GROUP):
        cols = slice(h * HEAD_DIM, (h + 1) * HEAD_DIM)
        q = q_ref[:, cols]
        k = jnp.concatenate([kp_ref[:, cols], kc_ref[:, cols]], axis=0)
        v = jnp.concatenate([vp_ref[:, cols], vc_ref[:, cols]], axis=0)
        sc = lax.dot_general(q, k, (((1,), (1,)), ((), ())),
                             preferred_element_type=F32) * scale
        sc = jnp.where(mask, sc, NEG)
        m = jnp.max(sc, axis=-1, keepdims=True)
        e = jnp.exp(sc - m)
        l = jnp.sum(e, axis=-1, keepdims=True)
        p = (e * (1.0 / l)).astype(BF16)
        o_ref[:, cols] = jnp.dot(p, v, preferred_element_type=F32)
        lse_tile = jnp.where(lane == h, m + jnp.log(l), lse_tile)
    lse_ref[...] = lse_tile


def _band_attn(q, k, v, group):
    s = q.shape[0]
    r = GROUP_DILATIONS[group]
    rows = s // r
    nb = rows // BLK
    n_col = Q_WIDTH // O_WIDTH
    qv, kv, vv = (t.reshape(rows, r * Q_WIDTH) for t in (q, k, v))
    cur = pl.BlockSpec((BLK, O_WIDTH), lambda j, n: (n, j * n_col + group))
    prev = pl.BlockSpec((BLK, O_WIDTH),
                        lambda j, n: (jnp.maximum(n - 1, 0), j * n_col + group))
    o, lse = pl.pallas_call(
        functools.partial(_band_attn_kernel, span=GROUP_SPANS[group]),
        grid=(r, nb),
        in_specs=[cur, prev, cur, prev, cur],
        out_specs=[pl.BlockSpec((BLK, O_WIDTH), lambda j, n: (n, j)),
                   pl.BlockSpec((BLK, V7X_LANES), lambda j, n: (n, j))],
        out_shape=[jax.ShapeDtypeStruct((rows, r * O_WIDTH), F32),
                   jax.ShapeDtypeStruct((rows, r * V7X_LANES), F32)],
        compiler_params=_params(2),
        name=f"band_attn_r{r}",
    )(qv, kv, kv, vv, vv)
    return o.reshape(s, O_WIDTH), lse.reshape(s, V7X_LANES)


TM_MIX = 512


def _mix_wo_kernel(o0_ref, o1_ref, o2_ref, l0_ref, l1_ref, l2_ref, w_ref, x_ref,
                   gate_ref, out_ref, hb_ref):
    @pl.when(pl.program_id(1) == 0)
    def _():
        l0, l1, l2 = l0_ref[...], l1_ref[...], l2_ref[...]
        m = jnp.maximum(jnp.maximum(l0, l1), l2)
        e0, e1, e2 = jnp.exp(l0 - m), jnp.exp(l1 - m), jnp.exp(l2 - m)
        inv = 1.0 / (e0 + e1 + e2)
        a0, a1, a2 = e0 * inv, e1 * inv, e2 * inv
        for h in range(HEADS_PER_GROUP):
            cols = slice(h * HEAD_DIM, (h + 1) * HEAD_DIM)
            o = (a0[:, h:h + 1] * o0_ref[:, cols] + a1[:, h:h + 1] * o1_ref[:, cols]
                 + a2[:, h:h + 1] * o2_ref[:, cols])
            hb_ref[:, cols] = o.astype(BF16)

    y = jnp.dot(hb_ref[...], w_ref[...].astype(BF16), preferred_element_type=F32)
    out_ref[...] = x_ref[...] + gate_ref[...] * y


def _mix_wo(outs, lses, w_o, x, gate):
    s, d = x.shape
    gate_col = gate[1] * (d // TN)
    ospec = pl.BlockSpec((TM_MIX, O_WIDTH), lambda i, j: (i, 0))
    lspec = pl.BlockSpec((TM_MIX, V7X_LANES), lambda i, j: (i, 0))
    return pl.pallas_call(
        _mix_wo_kernel,
        grid=(s // TM_MIX, d // TN),
        in_specs=[ospec, ospec, ospec, lspec, lspec, lspec,
                  pl.BlockSpec((O_WIDTH, TN), lambda i, j: (0, j)),
                  pl.BlockSpec((TM_MIX, TN), lambda i, j: (i, j)),
                  pl.BlockSpec((1, TN), lambda i, j: (0, gate_col + j))],
        out_specs=pl.BlockSpec((TM_MIX, TN), lambda i, j: (i, j)),
        out_shape=jax.ShapeDtypeStruct((s, d), F32),
        scratch_shapes=[pltpu.VMEM((TM_MIX, O_WIDTH), BF16)],
        compiler_params=_params(2),
        name="mix_wo",
    )(*outs, *lses, w_o, x, gate[0])


def kernel(x, c, positions, mod_w, mod_b, norm_mix_g, norm_ffn_g, conv_pw1_w, conv_pw1_b,
           conv_dw_w, conv_dw_b, conv_ln_g, conv_ln_b, conv_pw2_w, conv_pw2_b, kv_mod_w,
           kv_mod_b, kv_norm_g, w_kv, k_norm_g, w_q, q_norm_g, w_o, ffn_up_w, ffn_dw_w,
           ffn_dw_b, ffn_down_w):
    batch, s, d = x.shape
    assert (batch, s, d) == (1, SEQ, D_MODEL)
    x = x[0]
    c_col = c.reshape(d, 1)
    row = lambda v: v.reshape(1, -1)

    mod = _mod_matvec(c_col, mod_w, mod_b[:, None, :])
    kv_mod = _mod_matvec(c_col, kv_mod_w[None], kv_mod_b[None, None, :])[0]

    mvec = lambda l, q: (mod[l], q)

    glu = _pw1_glu(x, row(norm_mix_g[0]), mvec(0, 0), mvec(0, 1),
                   conv_pw1_w[0], row(conv_pw1_b[0]))
    x = _conv_pw2(glu, conv_dw_w[0], row(conv_dw_b[0]), row(conv_ln_g[0]),
                  row(conv_ln_b[0]), conv_pw2_w[0], row(conv_pw2_b[0]), x, mvec(0, 2))
    x = _conv_ffn(x, row(norm_ffn_g[0]), mvec(0, 3), mvec(0, 4), mvec(0, 5),
                  ffn_up_w[0], ffn_dw_w[0], row(ffn_dw_b[0]), ffn_down_w[0])

    tables = _rope_tables(positions[0])
    kv_shift, kv_scale = (kv_mod, 0), (kv_mod, 1)
    k = _proj(x, row(kv_norm_g), kv_shift, kv_scale, w_kv, 0, Q_WIDTH,
              head_g=row(k_norm_g), tables=tables)
    v = _proj(x, row(kv_norm_g), kv_shift, kv_scale, w_kv, Q_WIDTH // TN, Q_WIDTH)
    q = _proj(x, row(norm_mix_g[1]), mvec(1, 0), mvec(1, 1), w_q[0], 0, Q_WIDTH,
              head_g=row(q_norm_g[0]), tables=tables)
    outs, lses = zip(*[_band_attn(q, k, v, g) for g in range(N_GROUPS)])
    x = _mix_wo(outs, lses, w_o[0], x, mvec(1, 2))
    x = _conv_ffn(x, row(norm_ffn_g[1]), mvec(1, 3), mvec(1, 4), mvec(1, 5),
                  ffn_up_w[1], ffn_dw_w[1], row(ffn_dw_b[1]), ffn_down_w[1])
    return x[None]
```

```python
import functools
import math

import jax
import jax.numpy as jnp
from jax import lax
from jax.experimental import pallas as pl
from jax.experimental.pallas import tpu as pltpu

D_MODEL = 2048
SEQ = 8192
CONV_K = 31
FFN_CONV_K = 3
D_FF = 5632
GROUP_DILATIONS = (1, 4, 16)
GROUP_SPANS = (128, 128, 128)
N_GROUPS = 3
HEADS_PER_GROUP = 8
HEAD_DIM = 128
Q_WIDTH = N_GROUPS * HEADS_PER_GROUP * HEAD_DIM
O_WIDTH = HEADS_PER_GROUP * HEAD_DIM
ROT_DIM = HEAD_DIM // 4
ROPE_THETA = 500000.0
BLK = 128
EPS = 1e-6
NEG = -1e30

V7X_LANES = 128
V7X_SUBLANES = 8
V7X_BF16_ROWS_PER_VREG = 16
V7X_VMEM_BYTES = 64 * 1024 * 1024
VMEM_LIMIT = 56 * 1024 * 1024

TM = 1024
TN = 512
TN_HALF = TN // 2
ROW_CHUNK = 64

F32 = jnp.float32
BF16 = jnp.bfloat16


def _params(n_axes):
    return pltpu.CompilerParams(
        dimension_semantics=("arbitrary",) * n_axes,
        vmem_limit_bytes=VMEM_LIMIT)


def _sigmoid(x):
    return 1.0 / (1.0 + jnp.exp(-x))


def _silu(x):
    return x * _sigmoid(x)


def _matvec_kernel(c_ref, w_ref, b_ref, o_ref, sb_ref, *, k_dim, tn):
    first = (pl.program_id(0) == 0) & (pl.program_id(1) == 0)

    @pl.when(first)
    def _():
        c = c_ref[...]
        sb_ref[...] = jnp.broadcast_to(_silu(c), (k_dim, V7X_LANES))

    n_groups = tn // V7X_LANES

    def body(t, accs):
        r0 = pl.multiple_of(t * ROW_CHUNK, ROW_CHUNK)
        s = sb_ref[pl.ds(r0, ROW_CHUNK), :]
        w = w_ref[pl.ds(r0, ROW_CHUNK), :]
        new = []
        for g in range(n_groups):
            p = w[:, g * V7X_LANES:(g + 1) * V7X_LANES] * s
            a = accs[g]
            for u in range(ROW_CHUNK // V7X_SUBLANES):
                a = a + p[u * V7X_SUBLANES:(u + 1) * V7X_SUBLANES, :]
            new.append(a)
        return tuple(new)

    init = tuple(jnp.zeros((V7X_SUBLANES, V7X_LANES), F32) for _ in range(n_groups))
    accs = lax.fori_loop(0, k_dim // ROW_CHUNK, body, init)
    row = jnp.concatenate([jnp.sum(a, axis=0, keepdims=True) for a in accs], axis=1)
    o_ref[...] = row + b_ref[...]


def _mod_matvec(c_col, w, b):
    n_l, k_dim, n = w.shape
    tn = 1024
    return pl.pallas_call(
        functools.partial(_matvec_kernel, k_dim=k_dim, tn=tn),
        grid=(n_l, n // tn),
        in_specs=[
            pl.BlockSpec((k_dim, 1), lambda l, j: (0, 0)),
            pl.BlockSpec((None, k_dim, tn), lambda l, j: (l, 0, j)),
            pl.BlockSpec((None, 1, tn), lambda l, j: (l, 0, j)),
        ],
        out_specs=pl.BlockSpec((None, 1, tn), lambda l, j: (l, 0, j)),
        out_shape=jax.ShapeDtypeStruct((n_l, 1, n), F32),
        scratch_shapes=[pltpu.VMEM((k_dim, V7X_LANES), F32)],
        compiler_params=_params(2),
        name="mod_matvec",
    )(c_col, w, b)


MOD_ROWS = V7X_BF16_ROWS_PER_VREG


def _modulate_rows(x_ref, g_ref, shift_ref, scale_ref, h_ref, *, rows, h_row0=0):
    def body(t, carry):
        r0 = pl.multiple_of(t * MOD_ROWS, MOD_ROWS)
        x = x_ref[pl.ds(r0, MOD_ROWS), :]
        ms = jnp.mean(x * x, axis=-1, keepdims=True)
        y = x * lax.rsqrt(ms + EPS)
        h = (y * g_ref[...]) * (1.0 + scale_ref[...]) + shift_ref[...]
        h0 = pl.multiple_of(h_row0 + t * MOD_ROWS, MOD_ROWS)
        h_ref[pl.ds(h0, MOD_ROWS), :] = h.astype(BF16)
        return carry

    lax.fori_loop(0, rows // MOD_ROWS, body, 0)


def _pw1_glu_kernel(x_ref, g_ref, sh_ref, sc_ref, wa_ref, wg_ref, ba_ref, bg_ref,
                    o_ref, h_ref):
    @pl.when(pl.program_id(1) == 0)
    def _():
        _modulate_rows(x_ref, g_ref, sh_ref, sc_ref, h_ref, rows=TM)

    w = jnp.concatenate([wa_ref[...], wg_ref[...]], axis=1).astype(BF16)
    u = jnp.dot(h_ref[...], w, preferred_element_type=F32)
    a = u[:, :TN_HALF] + ba_ref[...]
    gt = u[:, TN_HALF:] + bg_ref[...]
    o_ref[...] = a * _sigmoid(gt)


def _pw1_glu(x, norm_g, shift, scale, w, b):
    s, d = x.shape
    nj = d // TN_HALF
    vec = lambda col: pl.BlockSpec((1, d), lambda i, j: (0, col))
    return pl.pallas_call(
        _pw1_glu_kernel,
        grid=(s // TM, nj),
        in_specs=[
            pl.BlockSpec((TM, d), lambda i, j: (i, 0)),
            pl.BlockSpec((1, d), lambda i, j: (0, 0)),
            vec(shift[1]), vec(scale[1]),
            pl.BlockSpec((d, TN_HALF), lambda i, j: (0, j)),
            pl.BlockSpec((d, TN_HALF), lambda i, j: (0, nj + j)),
            pl.BlockSpec((1, TN_HALF), lambda i, j: (0, j)),
            pl.BlockSpec((1, TN_HALF), lambda i, j: (0, nj + j)),
        ],
        out_specs=pl.BlockSpec((TM, TN_HALF), lambda i, j: (i, j)),
        out_shape=jax.ShapeDtypeStruct((s, d), F32),
        scratch_shapes=[pltpu.VMEM((TM, d), BF16)],
        compiler_params=_params(2),
        name="pw1_glu",
    )(x, norm_g, shift[0], scale[0], w, w, b, b)


CONV_HALO = 32
CONV_ROWS = 32


def _dwconv_ln_rows(gbuf_ref, dw_ref, db_ref, lg_ref, lb_ref, tmp_ref, hb_ref, *, rows, d):
    off = CONV_HALO - (CONV_K - 1)

    def body(t, carry):
        r0 = pl.multiple_of(t * CONV_ROWS, CONV_ROWS)
        for slab in range(d // V7X_LANES):
            lanes = slice(slab * V7X_LANES, (slab + 1) * V7X_LANES)
            acc = jnp.broadcast_to(db_ref[:, lanes], (CONV_ROWS, V7X_LANES))
            for k in range(CONV_K):
                tap = gbuf_ref[slab, pl.ds(r0 + off + k, CONV_ROWS, stride=1), :]
                acc = acc + tap * dw_ref[k:k + 1, lanes]
            tmp_ref[:, lanes] = acc
        u = tmp_ref[...]
        mu = jnp.mean(u, axis=-1, keepdims=True)
        uc = u - mu
        var = jnp.mean(uc * uc, axis=-1, keepdims=True)
        y = uc * lax.rsqrt(var + EPS) * lg_ref[...] + lb_ref[...]
        hb_ref[pl.ds(r0, CONV_ROWS), :] = _silu(y).astype(BF16)
        return carry

    lax.fori_loop(0, rows // CONV_ROWS, body, 0)


def _conv_pw2_kernel(g_ref, halo_ref, dw_ref, db_ref, lg_ref, lb_ref, w_ref, b_ref,
                     x_ref, gate_ref, o_ref, gbuf_ref, tmp_ref, hb_ref, *, d):
    i = pl.program_id(0)

    @pl.when(pl.program_id(1) == 0)
    def _():
        for slab in range(d // V7X_LANES):
            lanes = slice(slab * V7X_LANES, (slab + 1) * V7X_LANES)
            halo = halo_ref[:, lanes]
            gbuf_ref[slab, 0:CONV_HALO, :] = jnp.where(i == 0, jnp.zeros_like(halo), halo)
            gbuf_ref[slab, CONV_HALO:, :] = g_ref[:, lanes]
        _dwconv_ln_rows(gbuf_ref, dw_ref, db_ref, lg_ref, lb_ref, tmp_ref, hb_ref,
                        rows=TM, d=d)

    y = jnp.dot(hb_ref[...], w_ref[...].astype(BF16), preferred_element_type=F32)
    o_ref[...] = x_ref[...] + gate_ref[...] * (y + b_ref[...])


def _conv_pw2(glu, dw_w, dw_b, ln_g, ln_b, w, b, x, gate):
    s, d = glu.shape
    halo_blocks_per_tile = TM // CONV_HALO
    gate_col = gate[1] * (d // TN)
    return pl.pallas_call(
        functools.partial(_conv_pw2_kernel, d=d),
        grid=(s // TM, d // TN),
        in_specs=[
            pl.BlockSpec((TM, d), lambda i, j: (i, 0)),
            pl.BlockSpec((CONV_HALO, d),
                         lambda i, j: (jnp.maximum(i * halo_blocks_per_tile - 1, 0), 0)),
            pl.BlockSpec((CONV_K, d), lambda i, j: (0, 0)),
            pl.BlockSpec((1, d), lambda i, j: (0, 0)),
            pl.BlockSpec((1, d), lambda i, j: (0, 0)),
            pl.BlockSpec((1, d), lambda i, j: (0, 0)),
            pl.BlockSpec((d, TN), lambda i, j: (0, j)),
            pl.BlockSpec((1, TN), lambda i, j: (0, j)),
            pl.BlockSpec((TM, TN), lambda i, j: (i, j)),
            pl.BlockSpec((1, TN), lambda i, j: (0, gate_col + j)),
        ],
        out_specs=pl.BlockSpec((TM, TN), lambda i, j: (i, j)),
        out_shape=jax.ShapeDtypeStruct((s, d), F32),
        scratch_shapes=[
            pltpu.VMEM((d // V7X_LANES, TM + CONV_HALO, V7X_LANES), F32),
            pltpu.VMEM((CONV_ROWS, d), F32),
            pltpu.VMEM((TM, d), BF16),
        ],
        compiler_params=_params(2),
        name="dwconv_pw2",
    )(glu, glu, dw_w, dw_b, ln_g, ln_b, w, b, x, gate[0])


FFN_HALO = V7X_BF16_ROWS_PER_VREG
TF = 256
FFN_OUT_LANES = 512


def _ffn_kernel(x_ref, xh_ref, g_ref, sh_ref, sc_ref, gate_ref, wg_ref, wv_ref,
                dw_ref, db_ref, wd_ref, o_ref, h_ref, u_ref):
    i = pl.program_id(0)

    @pl.when(pl.program_id(1) == 0)
    def _():
        _modulate_rows(xh_ref, g_ref, sh_ref, sc_ref, h_ref, rows=FFN_HALO)
        _modulate_rows(x_ref, g_ref, sh_ref, sc_ref, h_ref, rows=TM, h_row0=FFN_HALO)
        o_ref[...] = x_ref[...]

    w_up = jnp.concatenate([wg_ref[...], wv_ref[...]], axis=1).astype(BF16)
    u_ref[...] = jnp.dot(h_ref[...], w_up, preferred_element_type=F32)
    @pl.when(i == 0)
    def _():
        u_ref[0:FFN_HALO, 0:TF] = jnp.zeros((FFN_HALO, TF), F32)

    gt = db_ref[...] + dw_ref[2:3, :] * u_ref[pl.ds(FFN_HALO, TM), 0:TF]
    gt = gt + dw_ref[1:2, :] * u_ref[pl.ds(FFN_HALO - 1, TM), 0:TF]
    gt = gt + dw_ref[0:1, :] * u_ref[pl.ds(FFN_HALO - 2, TM), 0:TF]
    val = u_ref[pl.ds(FFN_HALO, TM), TF:2 * TF]
    act = (_silu(gt) * val).astype(BF16)
    for c0 in range(0, o_ref.shape[1], FFN_OUT_LANES):
        cols = slice(c0, c0 + FFN_OUT_LANES)
        y = jnp.dot(act, wd_ref[:, cols].astype(BF16), preferred_element_type=F32)
        o_ref[:, cols] += gate_ref[:, cols] * y


def _conv_ffn(x, norm_g, shift, scale, gate, up_w, dw_w, dw_b, down_w):
    s, d = x.shape
    f = down_w.shape[0]
    nf = f // TF
    halo_blocks_per_tile = TM // FFN_HALO
    vec = lambda col: pl.BlockSpec((1, d), lambda i, j: (0, col))
    return pl.pallas_call(
        _ffn_kernel,
        grid=(s // TM, nf),
        in_specs=[
            pl.BlockSpec((TM, d), lambda i, j: (i, 0), pipeline_mode=pl.Buffered(1)),
            pl.BlockSpec((FFN_HALO, d),
                         lambda i, j: (jnp.maximum(i * halo_blocks_per_tile - 1, 0), 0)),
            pl.BlockSpec((1, d), lambda i, j: (0, 0)),
            vec(shift[1]), vec(scale[1]), vec(gate[1]),
            pl.BlockSpec((d, TF), lambda i, j: (0, j)),
            pl.BlockSpec((d, TF), lambda i, j: (0, nf + j)),
            pl.BlockSpec((FFN_CONV_K, TF), lambda i, j: (0, j)),
            pl.BlockSpec((1, TF), lambda i, j: (0, j)),
            pl.BlockSpec((TF, d), lambda i, j: (j, 0)),
        ],
        out_specs=pl.BlockSpec((TM, d), lambda i, j: (i, 0)),
        out_shape=jax.ShapeDtypeStruct((s, d), F32),
        scratch_shapes=[
            pltpu.VMEM((TM + FFN_HALO, d), BF16),
            pltpu.VMEM((TM + FFN_HALO, 2 * TF), F32),
        ],
        compiler_params=_params(2),
        name="conv_ffn",
    )(x, x, norm_g, shift[0], scale[0], gate[0], up_w, up_w, dw_w, dw_b, down_w)


def _rope_table_kernel(pos_ref, freq_ref, cos_ref, sin_lo_ref, sin_hi_ref):
    pos = pos_ref[...].astype(F32)
    ang = pos * freq_ref[...]
    lane = lax.broadcasted_iota(jnp.int32, ang.shape, 1)
    c = jnp.cos(ang)
    sn = jnp.sin(ang)
    half = ROT_DIM // 2
    cos_ref[...] = jnp.where(lane < ROT_DIM, c, 1.0)
    sin_lo_ref[...] = jnp.where(lane < half, -sn, 0.0)
    sin_hi_ref[...] = jnp.where((lane >= half) & (lane < ROT_DIM), sn, 0.0)


def _rope_tables(positions):
    s = positions.shape[0]
    rows = 1024
    inv_freq = ROPE_THETA ** (-jnp.arange(0, ROT_DIM, 2, dtype=F32) / ROT_DIM)
    lane_freq = jnp.concatenate(
        [inv_freq, inv_freq, jnp.zeros((HEAD_DIM - ROT_DIM,), F32)])[None, :]
    out = jax.ShapeDtypeStruct((s, HEAD_DIM), F32)
    spec = pl.BlockSpec((rows, HEAD_DIM), lambda i: (i, 0))
    return pl.pallas_call(
        _rope_table_kernel,
        grid=(s // rows,),
        in_specs=[pl.BlockSpec((rows, 1), lambda i: (i, 0)),
                  pl.BlockSpec((1, HEAD_DIM), lambda i: (0, 0))],
        out_specs=[spec, spec, spec],
        out_shape=[out, out, out],
        compiler_params=_params(1),
        name="rope_tables",
    )(positions.reshape(s, 1), lane_freq)


def _head_norm_rope(y, hg, cos, sin_lo, sin_hi):
    outs = []
    for h0 in range(0, y.shape[1], HEAD_DIM):
        q = y[:, h0:h0 + HEAD_DIM]
        ms = jnp.mean(q * q, axis=-1, keepdims=True)
        qn = q * lax.rsqrt(ms + EPS) * hg
        hi_to_lo = pltpu.roll(qn, HEAD_DIM - ROT_DIM // 2, axis=1)
        lo_to_hi = pltpu.roll(qn, ROT_DIM // 2, axis=1)
        outs.append(qn * cos + hi_to_lo * sin_lo + lo_to_hi * sin_hi)
    return jnp.concatenate(outs, axis=1)


def _proj_rope_kernel(x_ref, g_ref, sh_ref, sc_ref, w_ref, hg_ref, cos_ref, slo_ref,
                      shi_ref, o_ref, h_ref):
    @pl.when(pl.program_id(1) == 0)
    def _():
        _modulate_rows(x_ref, g_ref, sh_ref, sc_ref, h_ref, rows=TM)

    y = jnp.dot(h_ref[...], w_ref[...].astype(BF16), preferred_element_type=F32)
    o_ref[...] = _head_norm_rope(y, hg_ref[...], cos_ref[...], slo_ref[...],
                                 shi_ref[...]).astype(BF16)


def _proj_plain_kernel(x_ref, g_ref, sh_ref, sc_ref, w_ref, o_ref, h_ref):
    @pl.when(pl.program_id(1) == 0)
    def _():
        _modulate_rows(x_ref, g_ref, sh_ref, sc_ref, h_ref, rows=TM)

    y = jnp.dot(h_ref[...], w_ref[...].astype(BF16), preferred_element_type=F32)
    o_ref[...] = y.astype(BF16)


def _proj(x, norm_g, shift, scale, w, w_col0, n_out, head_g=None, tables=None):
    s, d = x.shape
    vec = lambda col: pl.BlockSpec((1, d), lambda i, j: (0, col))
    in_specs = [
        pl.BlockSpec((TM, d), lambda i, j: (i, 0)),
        pl.BlockSpec((1, d), lambda i, j: (0, 0)),
        vec(shift[1]), vec(scale[1]),
        pl.BlockSpec((d, TN), lambda i, j: (0, w_col0 + j)),
    ]
    args = [x, norm_g, shift[0], scale[0], w]
    if head_g is not None:
        tab = pl.BlockSpec((TM, HEAD_DIM), lambda i, j: (i, 0))
        in_specs += [pl.BlockSpec((1, HEAD_DIM), lambda i, j: (0, 0)), tab, tab, tab]
        args += [head_g, *tables]
        body, name = _proj_rope_kernel, "proj_norm_rope"
    else:
        body, name = _proj_plain_kernel, "proj_plain"
    return pl.pallas_call(
        body,
        grid=(s // TM, n_out // TN),
        in_specs=in_specs,
        out_specs=pl.BlockSpec((TM, TN), lambda i, j: (i, j)),
        out_shape=jax.ShapeDtypeStruct((s, n_out), BF16),
        scratch_shapes=[pltpu.VMEM((TM, d), BF16)],
        compiler_params=_params(2),
        name=name,
    )(*args)


def _band_attn_kernel(q_ref, kp_ref, kc_ref, vp_ref, vc_ref, o_ref, lse_ref, *, span):
    n = pl.program_id(1)
    qi = lax.broadcasted_iota(jnp.int32, (BLK, 2 * BLK), 0)
    kj = lax.broadcasted_iota(jnp.int32, (BLK, 2 * BLK), 1)
    dist = qi + BLK - kj
    mask = (dist >= 0) & (dist <= span) & ((n > 0) | (kj >= BLK))
    lane = lax.broadcasted_iota(jnp.int32, (BLK, V7X_LANES), 1)
    scale = 1.0 / math.sqrt(HEAD_DIM)
    lse_tile = jnp.zeros((BLK, V7X_LANES), F32)
    for h in range(HEADS_PER_GROUP):
        cols = slice(h * HEAD_DIM, (h + 1) * HEAD_DIM)
        q = q_ref[:, cols]
        k = jnp.concatenate([kp_ref[:, cols], kc_ref[:, cols]], axis=0)
        v = jnp.concatenate([vp_ref[:, cols], vc_ref[:, cols]], axis=0)
        sc = lax.dot_general(q, k, (((1,), (1,)), ((), ())),
                             preferred_element_type=F32) * scale
        sc = jnp.where(mask, sc, NEG)
        m = jnp.max(sc, axis=-1, keepdims=True)
        e = jnp.exp(sc - m)
        l = jnp.sum(e, axis=-1, keepdims=True)
        p = (e * (1.0 / l)).astype(BF16)
        o_ref[:, cols] = jnp.dot(p, v, preferred_element_type=F32)
        lse_tile = jnp.where(lane == h, m + jnp.log(l), lse_tile)
    lse_ref[...] = lse_tile


def _band_attn(q, k, v, group):
    s = q.shape[0]
    r = GROUP_DILATIONS[group]
    rows = s // r
    nb = rows // BLK
    n_col = Q_WIDTH // O_WIDTH
    qv, kv, vv = (t.reshape(rows, r * Q_WIDTH) for t in (q, k, v))
    cur = pl.BlockSpec((BLK, O_WIDTH), lambda j, n: (n, j * n_col + group))
    prev = pl.BlockSpec((BLK, O_WIDTH),
                        lambda j, n: (jnp.maximum(n - 1, 0), j * n_col + group))
    o, lse = pl.pallas_call(
        functools.partial(_band_attn_kernel, span=GROUP_SPANS[group]),
        grid=(r, nb),
        in_specs=[cur, prev, cur, prev, cur],
        out_specs=[pl.BlockSpec((BLK, O_WIDTH), lambda j, n: (n, j)),
                   pl.BlockSpec((BLK, V7X_LANES), lambda j, n: (n, j))],
        out_shape=[jax.ShapeDtypeStruct((rows, r * O_WIDTH), F32),
                   jax.ShapeDtypeStruct((rows, r * V7X_LANES), F32)],
        compiler_params=_params(2),
        name=f"band_attn_r{r}",
    )(qv, kv, kv, vv, vv)
    return o.reshape(s, O_WIDTH), lse.reshape(s, V7X_LANES)


TM_MIX = 512


def _mix_wo_kernel(o0_ref, o1_ref, o2_ref, l0_ref, l1_ref, l2_ref, w_ref, x_ref,
                   gate_ref, out_ref, hb_ref):
    @pl.when(pl.program_id(1) == 0)
    def _():
        l0, l1, l2 = l0_ref[...], l1_ref[...], l2_ref[...]
        m = jnp.maximum(jnp.maximum(l0, l1), l2)
        e0, e1, e2 = jnp.exp(l0 - m), jnp.exp(l1 - m), jnp.exp(l2 - m)
        inv = 1.0 / (e0 + e1 + e2)
        a0, a1, a2 = e0 * inv, e1 * inv, e2 * inv
        for h in range(HEADS_PER_GROUP):
            cols = slice(h * HEAD_DIM, (h + 1) * HEAD_DIM)
            o = (a0[:, h:h + 1] * o0_ref[:, cols] + a1[:, h:h + 1] * o1_ref[:, cols]
                 + a2[:, h:h + 1] * o2_ref[:, cols])
            hb_ref[:, cols] = o.astype(BF16)

    y = jnp.dot(hb_ref[...], w_ref[...].astype(BF16), preferred_element_type=F32)
    out_ref[...] = x_ref[...] + gate_ref[...] * y


def _mix_wo(outs, lses, w_o, x, gate):
    s, d = x.shape
    gate_col = gate[1] * (d // TN)
    ospec = pl.BlockSpec((TM_MIX, O_WIDTH), lambda i, j: (i, 0))
    lspec = pl.BlockSpec((TM_MIX, V7X_LANES), lambda i, j: (i, 0))
    return pl.pallas_call(
        _mix_wo_kernel,
        grid=(s // TM_MIX, d // TN),
        in_specs=[ospec, ospec, ospec, lspec, lspec, lspec,
                  pl.BlockSpec((O_WIDTH, TN), lambda i, j: (0, j)),
                  pl.BlockSpec((TM_MIX, TN), lambda i, j: (i, j)),
                  pl.BlockSpec((1, TN), lambda i, j: (0, gate_col + j))],
        out_specs=pl.BlockSpec((TM_MIX, TN), lambda i, j: (i, j)),
        out_shape=jax.ShapeDtypeStruct((s, d), F32),
        scratch_shapes=[pltpu.VMEM((TM_MIX, O_WIDTH), BF16)],
        compiler_params=_params(2),
        name="mix_wo",
    )(*outs, *lses, w_o, x, gate[0])


def kernel(x, c, positions, mod_w, mod_b, norm_mix_g, norm_ffn_g, conv_pw1_w, conv_pw1_b,
           conv_dw_w, conv_dw_b, conv_ln_g, conv_ln_b, conv_pw2_w, conv_pw2_b, kv_mod_w,
           kv_mod_b, kv_norm_g, w_kv, k_norm_g, w_q, q_norm_g, w_o, ffn_up_w, ffn_dw_w,
           ffn_dw_b, ffn_down_w):
    batch, s, d = x.shape
    assert (batch, s, d) == (1, SEQ, D_MODEL)
    x = x[0]
    c_col = c.reshape(d, 1)
    row = lambda v: v.reshape(1, -1)

    mod = _mod_matvec(c_col, mod_w, mod_b[:, None, :])
    kv_mod = _mod_matvec(c_col, kv_mod_w[None], kv_mod_b[None, None, :])[0]

    mvec = lambda l, q: (mod[l], q)

    glu = _pw1_glu(x, row(norm_mix_g[0]), mvec(0, 0), mvec(0, 1),
                   conv_pw1_w[0], row(conv_pw1_b[0]))
    x = _conv_pw2(glu, conv_dw_w[0], row(conv_dw_b[0]), row(conv_ln_g[0]),
                  row(conv_ln_b[0]), conv_pw2_w[0], row(conv_pw2_b[0]), x, mvec(0, 2))
    x = _conv_ffn(x, row(norm_ffn_g[0]), mvec(0, 3), mvec(0, 4), mvec(0, 5),
                  ffn_up_w[0], ffn_dw_w[0], row(ffn_dw_b[0]), ffn_down_w[0])

    tables = _rope_tables(positions[0])
    kv_shift, kv_scale = (kv_mod, 0), (kv_mod, 1)
    k = _proj(x, row(kv_norm_g), kv_shift, kv_scale, w_kv, 0, Q_WIDTH,
              head_g=row(k_norm_g), tables=tables)
    v = _proj(x, row(kv_norm_g), kv_shift, kv_scale, w_kv, Q_WIDTH // TN, Q_WIDTH)
    q = _proj(x, row(norm_mix_g[1]), mvec(1, 0), mvec(1, 1), w_q[0], 0, Q_WIDTH,
              head_g=row(q_norm_g[0]), tables=tables)
    outs, lses = zip(*[_band_attn(q, k, v, g) for g in range(N_GROUPS)])
    x = _mix_wo(outs, lses, w_o[0], x, mvec(1, 2))
    x = _conv_ffn(x, row(norm_ffn_g[1]), mvec(1, 3), mvec(1, 4), mvec(1, 5),
                  ffn_up_w[1], ffn_dw_w[1], row(ffn_dw_b[1]), ffn_down_w[1])
    return x[None]
```

```python
import functools
import math

import jax
import jax.numpy as jnp
from jax import lax
from jax.experimental import pallas as pl
from jax.experimental.pallas import tpu as pltpu

D_MODEL = 2048
SEQ = 8192
CONV_K = 31
FFN_CONV_K = 3
D_FF = 5632
GROUP_DILATIONS = (1, 4, 16)
GROUP_SPANS = (128, 128, 128)
N_GROUPS = 3
HEADS_PER_GROUP = 8
HEAD_DIM = 128
Q_WIDTH = N_GROUPS * HEADS_PER_GROUP * HEAD_DIM
O_WIDTH = HEADS_PER_GROUP * HEAD_DIM
ROT_DIM = HEAD_DIM // 4
ROPE_THETA = 500000.0
BLK = 128
EPS = 1e-6
NEG = -1e30

V7X_LANES = 128
V7X_SUBLANES = 8
V7X_BF16_ROWS_PER_VREG = 16
V7X_VMEM_BYTES = 64 * 1024 * 1024
VMEM_LIMIT = 56 * 1024 * 1024

TM = 1024
TN = 512
TN_HALF = TN // 2
ROW_CHUNK = 64

F32 = jnp.float32
BF16 = jnp.bfloat16


def _params(n_axes):
    return pltpu.CompilerParams(
        dimension_semantics=("arbitrary",) * n_axes,
        vmem_limit_bytes=VMEM_LIMIT)


def _sigmoid(x):
    return 1.0 / (1.0 + jnp.exp(-x))


def _silu(x):
    return x * _sigmoid(x)


def _lane_slab(slab):
    return slice(slab * V7X_LANES, (slab + 1) * V7X_LANES)


def _matvec_kernel(c_ref, w_ref, b_ref, o_ref, sb_ref, *, k_dim, tn):
    first = (pl.program_id(0) == 0) & (pl.program_id(1) == 0)

    @pl.when(first)
    def _():
        c = c_ref[...]
        sb_ref[...] = jnp.broadcast_to(_silu(c), (k_dim, V7X_LANES))

    n_groups = tn // V7X_LANES

    def body(t, accs):
        r0 = pl.multiple_of(t * ROW_CHUNK, ROW_CHUNK)
        s = sb_ref[pl.ds(r0, ROW_CHUNK), :]
        w = w_ref[pl.ds(r0, ROW_CHUNK), :]
        new = []
        for g in range(n_groups):
            p = w[:, _lane_slab(g)] * s
            a = accs[g]
            for u in range(ROW_CHUNK // V7X_SUBLANES):
                a = a + p[u * V7X_SUBLANES:(u + 1) * V7X_SUBLANES, :]
            new.append(a)
        return tuple(new)

    init = tuple(jnp.zeros((V7X_SUBLANES, V7X_LANES), F32) for _ in range(n_groups))
    accs = lax.fori_loop(0, k_dim // ROW_CHUNK, body, init)
    row = jnp.concatenate([jnp.sum(a, axis=0, keepdims=True) for a in accs], axis=1)
    o_ref[...] = row + b_ref[...]


def _mod_matvec(c_col, w, b):
    n_l, k_dim, n = w.shape
    tn = 1024
    return pl.pallas_call(
        functools.partial(_matvec_kernel, k_dim=k_dim, tn=tn),
        grid=(n_l, n // tn),
        in_specs=[
            pl.BlockSpec((k_dim, 1), lambda l, j: (0, 0)),
            pl.BlockSpec((None, k_dim, tn), lambda l, j: (l, 0, j)),
            pl.BlockSpec((None, 1, tn), lambda l, j: (l, 0, j)),
        ],
        out_specs=pl.BlockSpec((None, 1, tn), lambda l, j: (l, 0, j)),
        out_shape=jax.ShapeDtypeStruct((n_l, 1, n), F32),
        scratch_shapes=[pltpu.VMEM((k_dim, V7X_LANES), F32)],
        compiler_params=_params(2),
        name="mod_matvec",
    )(c_col, w, b)


MOD_ROWS = V7X_BF16_ROWS_PER_VREG
MOD_UNROLL = 4


def _modulate_rows(x_ref, targets, *, rows, h_row0=0):
    def body(t, carry):
        r0 = pl.multiple_of(t * MOD_ROWS, MOD_ROWS)
        x = x_ref[pl.ds(r0, MOD_ROWS), :]
        ms = jnp.mean(x * x, axis=-1, keepdims=True)
        y = x * lax.rsqrt(ms + EPS)
        h0 = pl.multiple_of(h_row0 + t * MOD_ROWS, MOD_ROWS)
        for g_ref, shift_ref, scale_ref, h_ref in targets:
            h = (y * g_ref[...]) * (1.0 + scale_ref[...]) + shift_ref[...]
            h_ref[pl.ds(h0, MOD_ROWS), :] = h.astype(BF16)
        return carry

    trips = rows // MOD_ROWS
    lax.fori_loop(0, trips, body, 0, unroll=min(MOD_UNROLL, trips))


def _pw1_glu_kernel(x_ref, g_ref, sh_ref, sc_ref, wa_ref, wg_ref, ba_ref, bg_ref,
                    o_ref, h_ref):
    @pl.when(pl.program_id(1) == 0)
    def _():
        _modulate_rows(x_ref, [(g_ref, sh_ref, sc_ref, h_ref)], rows=TM)

    w = jnp.concatenate([wa_ref[...], wg_ref[...]], axis=1).astype(BF16)
    u = jnp.dot(h_ref[...], w, preferred_element_type=F32)
    a = u[:, :TN_HALF] + ba_ref[...]
    gt = u[:, TN_HALF:] + bg_ref[...]
    o_ref[...] = a * _sigmoid(gt)


def _pw1_glu(x, norm_g, shift, scale, w, b):
    s, d = x.shape
    nj = d // TN_HALF
    vec = lambda col: pl.BlockSpec((1, d), lambda i, j: (0, col))
    return pl.pallas_call(
        _pw1_glu_kernel,
        grid=(s // TM, nj),
        in_specs=[
            pl.BlockSpec((TM, d), lambda i, j: (i, 0)),
            pl.BlockSpec((1, d), lambda i, j: (0, 0)),
            vec(shift[1]), vec(scale[1]),
            pl.BlockSpec((d, TN_HALF), lambda i, j: (0, j)),
            pl.BlockSpec((d, TN_HALF), lambda i, j: (0, nj + j)),
            pl.BlockSpec((1, TN_HALF), lambda i, j: (0, j)),
            pl.BlockSpec((1, TN_HALF), lambda i, j: (0, nj + j)),
        ],
        out_specs=pl.BlockSpec((TM, TN_HALF), lambda i, j: (i, j)),
        out_shape=jax.ShapeDtypeStruct((s, d), F32),
        scratch_shapes=[pltpu.VMEM((TM, d), BF16)],
        compiler_params=_params(2),
        name="pw1_glu",
    )(x, norm_g, shift[0], scale[0], w, w, b, b)


CONV_HALO = 32
CONV_ROWS = 32


def _dwconv_ln_rows(gbuf_ref, dw_ref, db_ref, lg_ref, lb_ref, tmp_ref, hb_ref, *, rows, d):
    off = CONV_HALO - (CONV_K - 1)

    def body(t, carry):
        r0 = pl.multiple_of(t * CONV_ROWS, CONV_ROWS)
        for slab in range(d // V7X_LANES):
            lanes = _lane_slab(slab)
            acc = jnp.broadcast_to(db_ref[:, lanes], (CONV_ROWS, V7X_LANES))
            for k in range(CONV_K):
                tap = gbuf_ref[slab, pl.ds(r0 + off + k, CONV_ROWS, stride=1), :]
                acc = acc + tap * dw_ref[k:k + 1, lanes]
            tmp_ref[:, lanes] = acc
        u = tmp_ref[...]
        mu = jnp.mean(u, axis=-1, keepdims=True)
        uc = u - mu
        var = jnp.mean(uc * uc, axis=-1, keepdims=True)
        y = uc * lax.rsqrt(var + EPS) * lg_ref[...] + lb_ref[...]
        hb_ref[pl.ds(r0, CONV_ROWS), :] = _silu(y).astype(BF16)
        return carry

    lax.fori_loop(0, rows // CONV_ROWS, body, 0)


def _conv_pw2_kernel(g_ref, halo_ref, dw_ref, db_ref, lg_ref, lb_ref, w_ref, b_ref,
                     x_ref, gate_ref, o_ref, gbuf_ref, tmp_ref, hb_ref, *, d):
    i = pl.program_id(0)

    @pl.when(pl.program_id(1) == 0)
    def _():
        for slab in range(d // V7X_LANES):
            lanes = _lane_slab(slab)
            halo = halo_ref[:, lanes]
            gbuf_ref[slab, 0:CONV_HALO, :] = jnp.where(i == 0, jnp.zeros_like(halo), halo)
            gbuf_ref[slab, CONV_HALO:, :] = g_ref[:, lanes]
        _dwconv_ln_rows(gbuf_ref, dw_ref, db_ref, lg_ref, lb_ref, tmp_ref, hb_ref,
                        rows=TM, d=d)

    y = jnp.dot(hb_ref[...], w_ref[...].astype(BF16), preferred_element_type=F32)
    o_ref[...] = x_ref[...] + gate_ref[...] * (y + b_ref[...])


def _conv_pw2(glu, dw_w, dw_b, ln_g, ln_b, w, b, x, gate):
    s, d = glu.shape
    halo_blocks_per_tile = TM // CONV_HALO
    gate_col = gate[1] * (d // TN)
    return pl.pallas_call(
        functools.partial(_conv_pw2_kernel, d=d),
        grid=(s // TM, d // TN),
        in_specs=[
            pl.BlockSpec((TM, d), lambda i, j: (i, 0)),
            pl.BlockSpec((CONV_HALO, d),
                         lambda i, j: (jnp.maximum(i * halo_blocks_per_tile - 1, 0), 0)),
            pl.BlockSpec((CONV_K, d), lambda i, j: (0, 0)),
            pl.BlockSpec((1, d), lambda i, j: (0, 0)),
            pl.BlockSpec((1, d), lambda i, j: (0, 0)),
            pl.BlockSpec((1, d), lambda i, j: (0, 0)),
            pl.BlockSpec((d, TN), lambda i, j: (0, j)),
            pl.BlockSpec((1, TN), lambda i, j: (0, j)),
            pl.BlockSpec((TM, TN), lambda i, j: (i, j)),
            pl.BlockSpec((1, TN), lambda i, j: (0, gate_col + j)),
        ],
        out_specs=pl.BlockSpec((TM, TN), lambda i, j: (i, j)),
        out_shape=jax.ShapeDtypeStruct((s, d), F32),
        scratch_shapes=[
            pltpu.VMEM((d // V7X_LANES, TM + CONV_HALO, V7X_LANES), F32),
            pltpu.VMEM((CONV_ROWS, d), F32),
            pltpu.VMEM((TM, d), BF16),
        ],
        compiler_params=_params(2),
        name="dwconv_pw2",
    )(glu, glu, dw_w, dw_b, ln_g, ln_b, w, b, x, gate[0])


FFN_HALO = V7X_BF16_ROWS_PER_VREG
TF = 256
FFN_OUT_LANES = 512


def _ffn_kernel(x_ref, xh_ref, g_ref, sh_ref, sc_ref, gate_ref, wg_ref, wv_ref,
                dw_ref, db_ref, wd_ref, o_ref, h_ref, u_ref):
    i = pl.program_id(0)

    @pl.when(pl.program_id(1) == 0)
    def _():
        target = [(g_ref, sh_ref, sc_ref, h_ref)]
        _modulate_rows(xh_ref, target, rows=FFN_HALO)
        _modulate_rows(x_ref, target, rows=TM, h_row0=FFN_HALO)
        o_ref[...] = x_ref[...]

    w_up = jnp.concatenate([wg_ref[...], wv_ref[...]], axis=1).astype(BF16)
    u_ref[...] = jnp.dot(h_ref[...], w_up, preferred_element_type=F32)
    @pl.when(i == 0)
    def _():
        u_ref[0:FFN_HALO, 0:TF] = jnp.zeros((FFN_HALO, TF), F32)

    gt = db_ref[...] + dw_ref[2:3, :] * u_ref[pl.ds(FFN_HALO, TM), 0:TF]
    gt = gt + dw_ref[1:2, :] * u_ref[pl.ds(FFN_HALO - 1, TM), 0:TF]
    gt = gt + dw_ref[0:1, :] * u_ref[pl.ds(FFN_HALO - 2, TM), 0:TF]
    val = u_ref[pl.ds(FFN_HALO, TM), TF:2 * TF]
    act = (_silu(gt) * val).astype(BF16)
    for c0 in range(0, o_ref.shape[1], FFN_OUT_LANES):
        cols = slice(c0, c0 + FFN_OUT_LANES)
        y = jnp.dot(act, wd_ref[:, cols].astype(BF16), preferred_element_type=F32)
        o_ref[:, cols] += gate_ref[:, cols] * y


def _conv_ffn(x, norm_g, shift, scale, gate, layer, up_w, dw_w, dw_b, down_w):
    s, d = x.shape
    f = down_w.shape[1]
    nf = f // TF
    halo_blocks_per_tile = TM // FFN_HALO
    vec = lambda col: pl.BlockSpec((1, d), lambda i, j: (0, col))
    return pl.pallas_call(
        _ffn_kernel,
        grid=(s // TM, nf),
        in_specs=[
            pl.BlockSpec((TM, d), lambda i, j: (i, 0), pipeline_mode=pl.Buffered(1)),
            pl.BlockSpec((FFN_HALO, d),
                         lambda i, j: (jnp.maximum(i * halo_blocks_per_tile - 1, 0), 0)),
            pl.BlockSpec((1, d), lambda i, j: (0, 0)),
            vec(shift[1]), vec(scale[1]), vec(gate[1]),
            pl.BlockSpec((None, d, TF), lambda i, j: (layer, 0, j)),
            pl.BlockSpec((None, d, TF), lambda i, j: (layer, 0, nf + j)),
            pl.BlockSpec((None, FFN_CONV_K, TF), lambda i, j: (layer, 0, j)),
            pl.BlockSpec((None, 1, TF), lambda i, j: (layer, 0, j)),
            pl.BlockSpec((None, TF, d), lambda i, j: (layer, j, 0)),
        ],
        out_specs=pl.BlockSpec((TM, d), lambda i, j: (i, 0)),
        out_shape=jax.ShapeDtypeStruct((s, d), F32),
        scratch_shapes=[
            pltpu.VMEM((TM + FFN_HALO, d), BF16),
            pltpu.VMEM((TM + FFN_HALO, 2 * TF), F32),
        ],
        compiler_params=_params(2),
        name="conv_ffn",
    )(x, x, norm_g, shift[0], scale[0], gate[0], up_w, up_w, dw_w, dw_b, down_w)


def _rope_table_kernel(pos_ref, freq_ref, cos_ref, sin_lo_ref, sin_hi_ref):
    pos = pos_ref[...].astype(F32)
    ang = pos * freq_ref[...]
    lane = lax.broadcasted_iota(jnp.int32, ang.shape, 1)
    c = jnp.cos(ang)
    sn = jnp.sin(ang)
    half = ROT_DIM // 2
    cos_ref[...] = jnp.where(lane < ROT_DIM, c, 1.0)
    sin_lo_ref[...] = jnp.where(lane < half, -sn, 0.0)
    sin_hi_ref[...] = jnp.where((lane >= half) & (lane < ROT_DIM), sn, 0.0)


def _rope_tables(positions):
    s = positions.shape[0]
    rows = 1024
    inv_freq = ROPE_THETA ** (-jnp.arange(0, ROT_DIM, 2, dtype=F32) / ROT_DIM)
    lane_freq = jnp.concatenate(
        [inv_freq, inv_freq, jnp.zeros((HEAD_DIM - ROT_DIM,), F32)])[None, :]
    out = jax.ShapeDtypeStruct((s, HEAD_DIM), F32)
    spec = pl.BlockSpec((rows, HEAD_DIM), lambda i: (i, 0))
    return pl.pallas_call(
        _rope_table_kernel,
        grid=(s // rows,),
        in_specs=[pl.BlockSpec((rows, 1), lambda i: (i, 0)),
                  pl.BlockSpec((1, HEAD_DIM), lambda i: (0, 0))],
        out_specs=[spec, spec, spec],
        out_shape=[out, out, out],
        compiler_params=_params(1),
        name="rope_tables",
    )(positions.reshape(s, 1), lane_freq)


TILES_PER_GROUP = O_WIDTH // TN


def _head_norm_rope(y, hg, cos, sin_lo, sin_hi):
    outs = []
    for h0 in range(0, y.shape[1], HEAD_DIM):
        q = y[:, h0:h0 + HEAD_DIM]
        ms = jnp.mean(q * q, axis=-1, keepdims=True)
        qn = q * lax.rsqrt(ms + EPS) * hg
        hi_to_lo = pltpu.roll(qn, HEAD_DIM - ROT_DIM // 2, axis=1)
        lo_to_hi = pltpu.roll(qn, ROT_DIM // 2, axis=1)
        outs.append(qn * cos + hi_to_lo * sin_lo + lo_to_hi * sin_hi)
    return jnp.concatenate(outs, axis=1)


def _store_by_residue(y, out_ref, ys_ref, *, r):
    if r == 1:
        out_ref[0] = y.astype(BF16)
        return
    n_slabs = y.shape[1] // V7X_LANES
    for slab in range(n_slabs):
        ys_ref[slab] = y[:, _lane_slab(slab)]
    rows = y.shape[0] // r
    for res in range(r):
        for slab in range(n_slabs):
            part = ys_ref[slab, pl.ds(res, rows, stride=r), :]
            out_ref[res, :, _lane_slab(slab)] = part.astype(BF16)


def _qkv_group_kernel(x_ref, gkv_ref, shkv_ref, sckv_ref, gq_ref, shq_ref, scq_ref,
                      wkv_ref, wq_ref, kg_ref, qg_ref, cos_ref, slo_ref, shi_ref,
                      k_out, v_out, q_out, hkv_ref, hq_ref, ys_ref, *, r):
    j = pl.program_id(1)

    @pl.when(j == 0)
    def _():
        _modulate_rows(x_ref, [(gkv_ref, shkv_ref, sckv_ref, hkv_ref),
                               (gq_ref, shq_ref, scq_ref, hq_ref)], rows=TM)

    def project(h_ref, w_ref):
        return jnp.dot(h_ref[...], w_ref[...].astype(BF16), preferred_element_type=F32)

    def rope(y, hg_ref):
        return _head_norm_rope(y, hg_ref[...], cos_ref[...], slo_ref[...], shi_ref[...])

    @pl.when(j < TILES_PER_GROUP)
    def _():
        _store_by_residue(rope(project(hkv_ref, wkv_ref), kg_ref), k_out, ys_ref, r=r)

    @pl.when((j >= TILES_PER_GROUP) & (j < 2 * TILES_PER_GROUP))
    def _():
        _store_by_residue(project(hkv_ref, wkv_ref), v_out, ys_ref, r=r)

    @pl.when(j >= 2 * TILES_PER_GROUP)
    def _():
        _store_by_residue(rope(project(hq_ref, wq_ref), qg_ref), q_out, ys_ref, r=r)


def _qkv_group(x, group, kv_norm_g, kv_shift, kv_scale, q_norm_g, q_shift, q_scale,
               w_kv, w_q, k_head_g, q_head_g, tables):
    s, d = x.shape
    r = GROUP_DILATIONS[group]
    t = TILES_PER_GROUP
    k_col0 = group * t
    v_col0 = Q_WIDTH // TN + group * t
    vec = lambda col: pl.BlockSpec((1, d), lambda i, j: (0, col))
    one = pl.BlockSpec((1, d), lambda i, j: (0, 0))
    tab = pl.BlockSpec((TM, HEAD_DIM), lambda i, j: (i, 0))
    head = pl.BlockSpec((1, HEAD_DIM), lambda i, j: (0, 0))

    def kv_col(j):
        return jnp.where(j < t, k_col0 + j, v_col0 + jnp.minimum(j - t, t - 1))

    def out_spec(first_step):
        return pl.BlockSpec(
            (r, TM // r, TN),
            lambda i, j: (0, i, jnp.clip(j - first_step, 0, t - 1)))

    out = jax.ShapeDtypeStruct((r, s // r, O_WIDTH), BF16)
    return pl.pallas_call(
        functools.partial(_qkv_group_kernel, r=r),
        grid=(s // TM, 3 * t),
        in_specs=[
            pl.BlockSpec((TM, d), lambda i, j: (i, 0)),
            one, vec(kv_shift[1]), vec(kv_scale[1]),
            one, vec(q_shift[1]), vec(q_scale[1]),
            pl.BlockSpec((d, TN), lambda i, j: (0, kv_col(j))),
            pl.BlockSpec((d, TN), lambda i, j: (0, k_col0 + jnp.maximum(j - 2 * t, 0))),
            head, head, tab, tab, tab,
        ],
        out_specs=[out_spec(0), out_spec(t), out_spec(2 * t)],
        out_shape=[out, out, out],
        scratch_shapes=[
            pltpu.VMEM((TM, d), BF16),
            pltpu.VMEM((TM, d), BF16),
            pltpu.VMEM((TN // V7X_LANES, TM, V7X_LANES), F32),
        ],
        compiler_params=_params(2),
        name=f"qkv_r{r}",
    )(x, kv_norm_g, kv_shift[0], kv_scale[0], q_norm_g, q_shift[0], q_scale[0],
      w_kv, w_q, k_head_g, q_head_g, *tables)


def _band_attn_kernel(q_ref, kp_ref, kc_ref, vp_ref, vc_ref, o_ref, lse_ref, *, span):
    n = pl.program_id(1)
    qi = lax.broadcasted_iota(jnp.int32, (BLK, 2 * BLK), 0)
    kj = lax.broadcasted_iota(jnp.int32, (BLK, 2 * BLK), 1)
    dist = qi + BLK - kj
    mask = (dist >= 0) & (dist <= span) & ((n > 0) | (kj >= BLK))
    lane = lax.broadcasted_iota(jnp.int32, (BLK, V7X_LANES), 1)
    scale = 1.0 / math.sqrt(HEAD_DIM)
    lse_tile = jnp.zeros((BLK, V7X_LANES), F32)
    for h in range(HEADS_PER_GROUP):
        cols = slice(h * HEAD_DIM, (h + 1) * HEAD_DIM)
        q = q_ref[:, cols]
        k = jnp.concatenate([kp_ref[:, cols], kc_ref[:, cols]], axis=0)
        v = jnp.concatenate([vp_ref[:, cols], vc_ref[:, cols]], axis=0)
        sc = lax.dot_general(q, k, (((1,), (1,)), ((), ())),
                             preferred_element_type=F32) * scale
        sc = jnp.where(mask, sc, NEG)
        m = jnp.max(sc, axis=-1, keepdims=True)
        e = jnp.exp(sc - m)
        l = jnp.sum(e, axis=-1, keepdims=True)
        p = (e * (1.0 / l)).astype(BF16)
        o_ref[:, cols] = jnp.dot(p, v, preferred_element_type=F32)
        lse_tile = jnp.where(lane == h, m + jnp.log(l), lse_tile)
    lse_ref[...] = lse_tile


def _band_attn(q, k, v, group):
    r, rows, _ = q.shape
    cur = pl.BlockSpec((None, BLK, O_WIDTH), lambda j, n: (j, n, 0))
    prev = pl.BlockSpec((None, BLK, O_WIDTH), lambda j, n: (j, jnp.maximum(n - 1, 0), 0))
    return pl.pallas_call(
        functools.partial(_band_attn_kernel, span=GROUP_SPANS[group]),
        grid=(r, rows // BLK),
        in_specs=[cur, prev, cur, prev, cur],
        out_specs=[cur, pl.BlockSpec((None, BLK, V7X_LANES), lambda j, n: (j, n, 0))],
        out_shape=[jax.ShapeDtypeStruct((r, rows, O_WIDTH), F32),
                   jax.ShapeDtypeStruct((r, rows, V7X_LANES), F32)],
        compiler_params=_params(2),
        name=f"band_attn_r{r}",
    )(q, k, k, v, v)


TM_MIX = 512


def _rows_from_residues(src_ref, dst_ref, *, r, lanes=None):
    n = src_ref.shape[1]
    for res in range(r):
        rows = pl.ds(res, n, stride=r)
        if lanes is None:
            dst_ref[rows, :] = src_ref[res]
        else:
            for slab in range(lanes // V7X_LANES):
                dst_ref[slab, rows, :] = src_ref[res, :, _lane_slab(slab)]


def _mix_wo_kernel(o0_ref, o1_ref, o2_ref, l0_ref, l1_ref, l2_ref, w_ref, x_ref,
                   gate_ref, out_ref, hb_ref, on1_ref, on2_ref, ln1_ref, ln2_ref):
    @pl.when(pl.program_id(1) == 0)
    def _():
        r1, r2 = GROUP_DILATIONS[1], GROUP_DILATIONS[2]
        _rows_from_residues(l1_ref, ln1_ref, r=r1)
        _rows_from_residues(l2_ref, ln2_ref, r=r2)
        _rows_from_residues(o1_ref, on1_ref, r=r1, lanes=O_WIDTH)
        _rows_from_residues(o2_ref, on2_ref, r=r2, lanes=O_WIDTH)
        l0, l1, l2 = l0_ref[0], ln1_ref[...], ln2_ref[...]
        m = jnp.maximum(jnp.maximum(l0, l1), l2)
        e0, e1, e2 = jnp.exp(l0 - m), jnp.exp(l1 - m), jnp.exp(l2 - m)
        inv = 1.0 / (e0 + e1 + e2)
        a0, a1, a2 = e0 * inv, e1 * inv, e2 * inv
        for h in range(HEADS_PER_GROUP):
            cols = _lane_slab(h)
            o = (a0[:, h:h + 1] * o0_ref[0, :, cols] + a1[:, h:h + 1] * on1_ref[h]
                 + a2[:, h:h + 1] * on2_ref[h])
            hb_ref[:, cols] = o.astype(BF16)

    y = jnp.dot(hb_ref[...], w_ref[...].astype(BF16), preferred_element_type=F32)
    out_ref[...] = x_ref[...] + gate_ref[...] * y


def _mix_wo(outs, lses, w_o, x, gate):
    s, d = x.shape
    gate_col = gate[1] * (d // TN)

    def planes(width, r):
        return pl.BlockSpec((r, TM_MIX // r, width), lambda i, j: (0, i, 0))

    slabs = pltpu.VMEM((O_WIDTH // V7X_LANES, TM_MIX, V7X_LANES), F32)
    rows = pltpu.VMEM((TM_MIX, V7X_LANES), F32)
    return pl.pallas_call(
        _mix_wo_kernel,
        grid=(s // TM_MIX, d // TN),
        in_specs=[planes(O_WIDTH, r) for r in GROUP_DILATIONS]
        + [planes(V7X_LANES, r) for r in GROUP_DILATIONS]
        + [pl.BlockSpec((O_WIDTH, TN), lambda i, j: (0, j)),
           pl.BlockSpec((TM_MIX, TN), lambda i, j: (i, j)),
           pl.BlockSpec((1, TN), lambda i, j: (0, gate_col + j))],
        out_specs=pl.BlockSpec((TM_MIX, TN), lambda i, j: (i, j)),
        out_shape=jax.ShapeDtypeStruct((s, d), F32),
        scratch_shapes=[pltpu.VMEM((TM_MIX, O_WIDTH), BF16), slabs, slabs, rows, rows],
        compiler_params=_params(2),
        name="mix_wo",
    )(*outs, *lses, w_o, x, gate[0])


def kernel(x, c, positions, mod_w, mod_b, norm_mix_g, norm_ffn_g, conv_pw1_w, conv_pw1_b,
           conv_dw_w, conv_dw_b, conv_ln_g, conv_ln_b, conv_pw2_w, conv_pw2_b, kv_mod_w,
           kv_mod_b, kv_norm_g, w_kv, k_norm_g, w_q, q_norm_g, w_o, ffn_up_w, ffn_dw_w,
           ffn_dw_b, ffn_down_w):
    batch, s, d = x.shape
    assert (batch, s, d) == (1, SEQ, D_MODEL)
    x = x[0]
    c_col = c.reshape(d, 1)
    row = lambda v: v.reshape(1, -1)

    mod = _mod_matvec(c_col, mod_w, mod_b[:, None, :])
    kv_mod = _mod_matvec(c_col, kv_mod_w[None], kv_mod_b[None, None, :])[0]

    mvec = lambda l, q: (mod[l], q)
    ffn_dw_b3 = ffn_dw_b[:, None, :]

    def ffn(x, l):
        return _conv_ffn(x, row(norm_ffn_g[l]), mvec(l, 3), mvec(l, 4), mvec(l, 5), l,
                         ffn_up_w, ffn_dw_w, ffn_dw_b3, ffn_down_w)

    glu = _pw1_glu(x, row(norm_mix_g[0]), mvec(0, 0), mvec(0, 1),
                   conv_pw1_w[0], row(conv_pw1_b[0]))
    x = _conv_pw2(glu, conv_dw_w[0], row(conv_dw_b[0]), row(conv_ln_g[0]),
                  row(conv_ln_b[0]), conv_pw2_w[0], row(conv_pw2_b[0]), x, mvec(0, 2))
    x = ffn(x, 0)

    tables = _rope_tables(positions[0])
    outs, lses = [], []
    for g in range(N_GROUPS):
        k, v, q = _qkv_group(x, g, row(kv_norm_g), (kv_mod, 0), (kv_mod, 1),
                             row(norm_mix_g[1]), mvec(1, 0), mvec(1, 1),
                             w_kv, w_q[0], row(k_norm_g), row(q_norm_g[0]), tables)
        o, lse = _band_attn(q, k, v, g)
        outs.append(o)
        lses.append(lse)
    x = _mix_wo(outs, lses, w_o[0], x, mvec(1, 2))
    x = ffn(x, 1)
    return x[None]
```

```python
import functools
import math

import jax
import jax.numpy as jnp
from jax import lax
from jax.experimental import pallas as pl
from jax.experimental.pallas import tpu as pltpu

D_MODEL = 2048
SEQ = 8192
CONV_K = 31
FFN_CONV_K = 3
D_FF = 5632
GROUP_DILATIONS = (1, 4, 16)
GROUP_SPANS = (128, 128, 128)
N_GROUPS = 3
HEADS_PER_GROUP = 8
HEAD_DIM = 128
Q_WIDTH = N_GROUPS * HEADS_PER_GROUP * HEAD_DIM
O_WIDTH = HEADS_PER_GROUP * HEAD_DIM
ROT_DIM = HEAD_DIM // 4
ROPE_THETA = 500000.0
BLK = 128
EPS = 1e-6
NEG = -1e30

V7X_LANES = 128
V7X_SUBLANES = 8
V7X_BF16_ROWS_PER_VREG = 16
V7X_VMEM_BYTES = 64 * 1024 * 1024
VMEM_LIMIT = 56 * 1024 * 1024

TM = 1024
TN = 512
TN_HALF = TN // 2
ROW_CHUNK = 64

F32 = jnp.float32
BF16 = jnp.bfloat16


def _params(n_axes):
    return pltpu.CompilerParams(
        dimension_semantics=("arbitrary",) * n_axes,
        vmem_limit_bytes=VMEM_LIMIT)


def _sigmoid(x):
    return 1.0 / (1.0 + jnp.exp(-x))


def _silu(x):
    return x * _sigmoid(x)


def _lane_slab(slab):
    return slice(slab * V7X_LANES, (slab + 1) * V7X_LANES)


def _matvec_kernel(c_ref, w_ref, b_ref, o_ref, sb_ref, *, k_dim, tn):
    first = (pl.program_id(0) == 0) & (pl.program_id(1) == 0)

    @pl.when(first)
    def _():
        c = c_ref[...]
        sb_ref[...] = jnp.broadcast_to(_silu(c), (k_dim, V7X_LANES))

    n_groups = tn // V7X_LANES

    def body(t, accs):
        r0 = pl.multiple_of(t * ROW_CHUNK, ROW_CHUNK)
        s = sb_ref[pl.ds(r0, ROW_CHUNK), :]
        w = w_ref[pl.ds(r0, ROW_CHUNK), :]
        new = []
        for g in range(n_groups):
            p = w[:, _lane_slab(g)] * s
            a = accs[g]
            for u in range(ROW_CHUNK // V7X_SUBLANES):
                a = a + p[u * V7X_SUBLANES:(u + 1) * V7X_SUBLANES, :]
            new.append(a)
        return tuple(new)

    init = tuple(jnp.zeros((V7X_SUBLANES, V7X_LANES), F32) for _ in range(n_groups))
    accs = lax.fori_loop(0, k_dim // ROW_CHUNK, body, init)
    row = jnp.concatenate([jnp.sum(a, axis=0, keepdims=True) for a in accs], axis=1)
    o_ref[...] = row + b_ref[...]


def _mod_matvec(c_col, w, b):
    n_l, k_dim, n = w.shape
    tn = 1024
    return pl.pallas_call(
        functools.partial(_matvec_kernel, k_dim=k_dim, tn=tn),
        grid=(n_l, n // tn),
        in_specs=[
            pl.BlockSpec((k_dim, 1), lambda l, j: (0, 0)),
            pl.BlockSpec((None, k_dim, tn), lambda l, j: (l, 0, j)),
            pl.BlockSpec((None, 1, tn), lambda l, j: (l, 0, j)),
        ],
        out_specs=pl.BlockSpec((None, 1, tn), lambda l, j: (l, 0, j)),
        out_shape=jax.ShapeDtypeStruct((n_l, 1, n), F32),
        scratch_shapes=[pltpu.VMEM((k_dim, V7X_LANES), F32)],
        compiler_params=_params(2),
        name="mod_matvec",
    )(c_col, w, b)


MOD_ROWS = V7X_BF16_ROWS_PER_VREG
MOD_UNROLL = 4


def _modulate_rows(x_ref, targets, *, rows, h_row0=0):
    def body(t, carry):
        r0 = pl.multiple_of(t * MOD_ROWS, MOD_ROWS)
        x = x_ref[pl.ds(r0, MOD_ROWS), :]
        ms = jnp.mean(x * x, axis=-1, keepdims=True)
        y = x * lax.rsqrt(ms + EPS)
        h0 = pl.multiple_of(h_row0 + t * MOD_ROWS, MOD_ROWS)
        for g_ref, shift_ref, scale_ref, h_ref in targets:
            h = (y * g_ref[...]) * (1.0 + scale_ref[...]) + shift_ref[...]
            h_ref[pl.ds(h0, MOD_ROWS), :] = h.astype(BF16)
        return carry

    trips = rows // MOD_ROWS
    lax.fori_loop(0, trips, body, 0, unroll=min(MOD_UNROLL, trips))


def _pw1_glu_kernel(x_ref, g_ref, sh_ref, sc_ref, wa_ref, wg_ref, ba_ref, bg_ref,
                    o_ref, h_ref):
    @pl.when(pl.program_id(1) == 0)
    def _():
        _modulate_rows(x_ref, [(g_ref, sh_ref, sc_ref, h_ref)], rows=TM)

    w = jnp.concatenate([wa_ref[...], wg_ref[...]], axis=1).astype(BF16)
    u = jnp.dot(h_ref[...], w, preferred_element_type=F32)
    a = u[:, :TN_HALF] + ba_ref[...]
    gt = u[:, TN_HALF:] + bg_ref[...]
    o_ref[...] = a * _sigmoid(gt)


def _pw1_glu(x, norm_g, shift, scale, w, b):
    s, d = x.shape
    nj = d // TN_HALF
    vec = lambda col: pl.BlockSpec((1, d), lambda i, j: (0, col))
    return pl.pallas_call(
        _pw1_glu_kernel,
        grid=(s // TM, nj),
        in_specs=[
            pl.BlockSpec((TM, d), lambda i, j: (i, 0)),
            pl.BlockSpec((1, d), lambda i, j: (0, 0)),
            vec(shift[1]), vec(scale[1]),
            pl.BlockSpec((d, TN_HALF), lambda i, j: (0, j)),
            pl.BlockSpec((d, TN_HALF), lambda i, j: (0, nj + j)),
            pl.BlockSpec((1, TN_HALF), lambda i, j: (0, j)),
            pl.BlockSpec((1, TN_HALF), lambda i, j: (0, nj + j)),
        ],
        out_specs=pl.BlockSpec((TM, TN_HALF), lambda i, j: (i, j)),
        out_shape=jax.ShapeDtypeStruct((s, d), F32),
        scratch_shapes=[pltpu.VMEM((TM, d), BF16)],
        compiler_params=_params(2),
        name="pw1_glu",
    )(x, norm_g, shift[0], scale[0], w, w, b, b)


CONV_HALO = 32
CONV_ROWS = 32


def _dwconv_ln_rows(gbuf_ref, dw_ref, db_ref, lg_ref, lb_ref, tmp_ref, hb_ref, *, rows, d):
    off = CONV_HALO - (CONV_K - 1)

    def body(t, carry):
        r0 = pl.multiple_of(t * CONV_ROWS, CONV_ROWS)
        for slab in range(d // V7X_LANES):
            lanes = _lane_slab(slab)
            acc = jnp.broadcast_to(db_ref[:, lanes], (CONV_ROWS, V7X_LANES))
            for k in range(CONV_K):
                tap = gbuf_ref[slab, pl.ds(r0 + off + k, CONV_ROWS, stride=1), :]
                acc = acc + tap * dw_ref[k:k + 1, lanes]
            tmp_ref[:, lanes] = acc
        u = tmp_ref[...]
        mu = jnp.mean(u, axis=-1, keepdims=True)
        uc = u - mu
        var = jnp.mean(uc * uc, axis=-1, keepdims=True)
        y = uc * lax.rsqrt(var + EPS) * lg_ref[...] + lb_ref[...]
        hb_ref[pl.ds(r0, CONV_ROWS), :] = _silu(y).astype(BF16)
        return carry

    lax.fori_loop(0, rows // CONV_ROWS, body, 0)


def _conv_pw2_kernel(g_ref, halo_ref, dw_ref, db_ref, lg_ref, lb_ref, w_ref, b_ref,
                     x_ref, gate_ref, o_ref, gbuf_ref, tmp_ref, hb_ref, *, d):
    i = pl.program_id(0)

    @pl.when(pl.program_id(1) == 0)
    def _():
        for slab in range(d // V7X_LANES):
            lanes = _lane_slab(slab)
            halo = halo_ref[:, lanes]
            gbuf_ref[slab, 0:CONV_HALO, :] = jnp.where(i == 0, jnp.zeros_like(halo), halo)
            gbuf_ref[slab, CONV_HALO:, :] = g_ref[:, lanes]
        _dwconv_ln_rows(gbuf_ref, dw_ref, db_ref, lg_ref, lb_ref, tmp_ref, hb_ref,
                        rows=TM, d=d)

    y = jnp.dot(hb_ref[...], w_ref[...].astype(BF16), preferred_element_type=F32)
    o_ref[...] = x_ref[...] + gate_ref[...] * (y + b_ref[...])


def _conv_pw2(glu, dw_w, dw_b, ln_g, ln_b, w, b, x, gate):
    s, d = glu.shape
    halo_blocks_per_tile = TM // CONV_HALO
    gate_col = gate[1] * (d // TN)
    return pl.pallas_call(
        functools.partial(_conv_pw2_kernel, d=d),
        grid=(s // TM, d // TN),
        in_specs=[
            pl.BlockSpec((TM, d), lambda i, j: (i, 0)),
            pl.BlockSpec((CONV_HALO, d),
                         lambda i, j: (jnp.maximum(i * halo_blocks_per_tile - 1, 0), 0)),
            pl.BlockSpec((CONV_K, d), lambda i, j: (0, 0)),
            pl.BlockSpec((1, d), lambda i, j: (0, 0)),
            pl.BlockSpec((1, d), lambda i, j: (0, 0)),
            pl.BlockSpec((1, d), lambda i, j: (0, 0)),
            pl.BlockSpec((d, TN), lambda i, j: (0, j)),
            pl.BlockSpec((1, TN), lambda i, j: (0, j)),
            pl.BlockSpec((TM, TN), lambda i, j: (i, j)),
            pl.BlockSpec((1, TN), lambda i, j: (0, gate_col + j)),
        ],
        out_specs=pl.BlockSpec((TM, TN), lambda i, j: (i, j)),
        out_shape=jax.ShapeDtypeStruct((s, d), F32),
        scratch_shapes=[
            pltpu.VMEM((d // V7X_LANES, TM + CONV_HALO, V7X_LANES), F32),
            pltpu.VMEM((CONV_ROWS, d), F32),
            pltpu.VMEM((TM, d), BF16),
        ],
        compiler_params=_params(2),
        name="dwconv_pw2",
    )(glu, glu, dw_w, dw_b, ln_g, ln_b, w, b, x, gate[0])


FFN_HALO = V7X_BF16_ROWS_PER_VREG
TF = 256
FFN_OUT_LANES = 512


def _ffn_kernel(x_ref, xh_ref, g_ref, sh_ref, sc_ref, gate_ref, wg_ref, wv_ref,
                dw_ref, db_ref, wd_ref, o_ref, h_ref, u_ref):
    i = pl.program_id(0)

    @pl.when(pl.program_id(1) == 0)
    def _():
        target = [(g_ref, sh_ref, sc_ref, h_ref)]
        _modulate_rows(xh_ref, target, rows=FFN_HALO)
        _modulate_rows(x_ref, target, rows=TM, h_row0=FFN_HALO)
        o_ref[...] = x_ref[...]

    w_up = jnp.concatenate([wg_ref[...], wv_ref[...]], axis=1).astype(BF16)
    u = jnp.dot(h_ref[...], w_up, preferred_element_type=F32)
    u_ref[...] = u
    keep_halo = jnp.where(i == 0, 0.0, 1.0)
    u_ref[0:FFN_HALO, 0:TF] = u[0:FFN_HALO, 0:TF] * keep_halo

    gt = db_ref[...] + dw_ref[2:3, :] * u_ref[pl.ds(FFN_HALO, TM), 0:TF]
    gt = gt + dw_ref[1:2, :] * u_ref[pl.ds(FFN_HALO - 1, TM), 0:TF]
    gt = gt + dw_ref[0:1, :] * u_ref[pl.ds(FFN_HALO - 2, TM), 0:TF]
    val = u_ref[pl.ds(FFN_HALO, TM), TF:2 * TF]
    act = (_silu(gt) * val).astype(BF16)
    for c0 in range(0, o_ref.shape[1], FFN_OUT_LANES):
        cols = slice(c0, c0 + FFN_OUT_LANES)
        y = jnp.dot(act, wd_ref[:, cols].astype(BF16), preferred_element_type=F32)
        o_ref[:, cols] += gate_ref[:, cols] * y


def _conv_ffn(x, norm_g, shift, scale, gate, layer, up_w, dw_w, dw_b, down_w):
    s, d = x.shape
    f = down_w.shape[1]
    nf = f // TF
    halo_blocks_per_tile = TM // FFN_HALO
    vec = lambda col: pl.BlockSpec((1, d), lambda i, j: (0, col))
    return pl.pallas_call(
        _ffn_kernel,
        grid=(s // TM, nf),
        in_specs=[
            pl.BlockSpec((TM, d), lambda i, j: (i, 0), pipeline_mode=pl.Buffered(1)),
            pl.BlockSpec((FFN_HALO, d),
                         lambda i, j: (jnp.maximum(i * halo_blocks_per_tile - 1, 0), 0)),
            pl.BlockSpec((1, d), lambda i, j: (0, 0)),
            vec(shift[1]), vec(scale[1]), vec(gate[1]),
            pl.BlockSpec((None, d, TF), lambda i, j: (layer, 0, j)),
            pl.BlockSpec((None, d, TF), lambda i, j: (layer, 0, nf + j)),
            pl.BlockSpec((None, FFN_CONV_K, TF), lambda i, j: (layer, 0, j)),
            pl.BlockSpec((None, 1, TF), lambda i, j: (layer, 0, j)),
            pl.BlockSpec((None, TF, d), lambda i, j: (layer, j, 0)),
        ],
        out_specs=pl.BlockSpec((TM, d), lambda i, j: (i, 0)),
        out_shape=jax.ShapeDtypeStruct((s, d), F32),
        scratch_shapes=[
            pltpu.VMEM((TM + FFN_HALO, d), BF16),
            pltpu.VMEM((TM + FFN_HALO, 2 * TF), F32),
        ],
        compiler_params=_params(2),
        name="conv_ffn",
    )(x, x, norm_g, shift[0], scale[0], gate[0], up_w, up_w, dw_w, dw_b, down_w)


def _rope_table_kernel(pos_ref, freq_ref, cos_ref, sin_lo_ref, sin_hi_ref):
    pos = pos_ref[...].astype(F32)
    ang = pos * freq_ref[...]
    lane = lax.broadcasted_iota(jnp.int32, ang.shape, 1)
    c = jnp.cos(ang)
    sn = jnp.sin(ang)
    half = ROT_DIM // 2
    cos_ref[...] = jnp.where(lane < ROT_DIM, c, 1.0)
    sin_lo_ref[...] = jnp.where(lane < half, -sn, 0.0)
    sin_hi_ref[...] = jnp.where((lane >= half) & (lane < ROT_DIM), sn, 0.0)


def _rope_tables(positions):
    s = positions.shape[0]
    rows = 1024
    inv_freq = ROPE_THETA ** (-jnp.arange(0, ROT_DIM, 2, dtype=F32) / ROT_DIM)
    lane_freq = jnp.concatenate(
        [inv_freq, inv_freq, jnp.zeros((HEAD_DIM - ROT_DIM,), F32)])[None, :]
    out = jax.ShapeDtypeStruct((s, HEAD_DIM), F32)
    spec = pl.BlockSpec((rows, HEAD_DIM), lambda i: (i, 0))
    return pl.pallas_call(
        _rope_table_kernel,
        grid=(s // rows,),
        in_specs=[pl.BlockSpec((rows, 1), lambda i: (i, 0)),
                  pl.BlockSpec((1, HEAD_DIM), lambda i: (0, 0))],
        out_specs=[spec, spec, spec],
        out_shape=[out, out, out],
        compiler_params=_params(1),
        name="rope_tables",
    )(positions.reshape(s, 1), lane_freq)


TILES_PER_GROUP = O_WIDTH // TN


def _norm_rope_head(q, hg, cos, sin_lo, sin_hi):
    ms = jnp.mean(q * q, axis=-1, keepdims=True)
    qn = q * lax.rsqrt(ms + EPS) * hg
    hi_to_lo = pltpu.roll(qn, HEAD_DIM - ROT_DIM // 2, axis=1)
    lo_to_hi = pltpu.roll(qn, ROT_DIM // 2, axis=1)
    return qn * cos + hi_to_lo * sin_lo + lo_to_hi * sin_hi


SPLIT_STRIDE = 4


def _finish_tile(y_ref, out_ref, hg_ref, table_refs, y4_ref, *, r):
    rows = TM // r
    two_pass = r > SPLIT_STRIDE
    r_outer = r // SPLIT_STRIDE
    for head in range(TN // HEAD_DIM):
        if hg_ref is not None:
            cos, sin_lo, sin_hi = (t[...] for t in table_refs)
            y_ref[head] = _norm_rope_head(y_ref[head], hg_ref[...], cos, sin_lo, sin_hi)
        if two_pass:
            for p in range(SPLIT_STRIDE):
                y4_ref[head % 2, p] = y_ref[head, pl.ds(p, TM // SPLIT_STRIDE,
                                                        stride=SPLIT_STRIDE), :]
        for res in range(r):
            if r == 1:
                y = y_ref[head]
            elif two_pass:
                p, q = res % SPLIT_STRIDE, res // SPLIT_STRIDE
                y = y4_ref[head % 2, p, pl.ds(q, rows, stride=r_outer), :]
            else:
                y = y_ref[head, pl.ds(res, rows, stride=r), :]
            out_ref[res, :, _lane_slab(head)] = y.astype(BF16)


def _qkv_group_kernel(x_ref, gkv_ref, shkv_ref, sckv_ref, gq_ref, shq_ref, scq_ref,
                      wkv_ref, wq_ref, kg_ref, qg_ref, cos_ref, slo_ref, shi_ref,
                      k_out, v_out, q_out, hkv_ref, hq_ref, ya_ref, yb_ref, y4_ref, *, r):
    s = pl.program_id(1)
    t = TILES_PER_GROUP
    y_slots = (ya_ref, yb_ref)
    outs = (k_out, v_out, q_out)
    head_gains = (kg_ref, None, qg_ref)
    tables = (cos_ref, slo_ref, shi_ref)

    @pl.when(s == 0)
    def _():
        _modulate_rows(x_ref, [(gkv_ref, shkv_ref, sckv_ref, hkv_ref),
                               (gq_ref, shq_ref, scq_ref, hq_ref)], rows=TM)

    for step in range(3 * t + 1):
        @pl.when(s == step)
        def _(step=step):
            if step < 3 * t:
                h_ref, w_ref = (hq_ref, wq_ref) if step // t == 2 else (hkv_ref, wkv_ref)
                y = jnp.dot(h_ref[...], w_ref[...].astype(BF16), preferred_element_type=F32)
                for head in range(TN // HEAD_DIM):
                    y_slots[step % 2][head] = y[:, _lane_slab(head)]
            if step > 0:
                kind = (step - 1) // t
                _finish_tile(y_slots[(step - 1) % 2], outs[kind], head_gains[kind], tables,
                             y4_ref, r=r)


def _qkv_group(x, group, kv_norm_g, kv_shift, kv_scale, q_norm_g, q_shift, q_scale,
               w_kv, w_q, k_head_g, q_head_g, tables):
    s, d = x.shape
    r = GROUP_DILATIONS[group]
    t = TILES_PER_GROUP
    k_col0 = group * t
    v_col0 = Q_WIDTH // TN + group * t
    vec = lambda col: pl.BlockSpec((1, d), lambda i, j: (0, col))
    one = pl.BlockSpec((1, d), lambda i, j: (0, 0))
    tab = pl.BlockSpec((TM, HEAD_DIM), lambda i, j: (i, 0))
    head =pl.BlockSpec((1, HEAD_DIM), lambda i, j: (0, 0))

    y4_rows = TM // SPLIT_STRIDE if r > SPLIT_STRIDE else V7X_SUBLANES

    def kv_col(j):
        return jnp.where(j < t, k_col0 + j, v_col0 + jnp.minimum(j - t, t - 1))

    def out_spec(first_step):
        return pl.BlockSpec(
            (r, TM // r, TN),
            lambda i, j: (0, i, jnp.clip(j - first_step, 0, t - 1)))

    out = jax.ShapeDtypeStruct((r, s // r, O_WIDTH), BF16)
    return pl.pallas_call(
        functools.partial(_qkv_group_kernel, r=r),
        grid=(s // TM, 3 * t + 1),
        in_specs=[
            pl.BlockSpec((TM, d), lambda i, j: (i, 0)),
            one, vec(kv_shift[1]), vec(kv_scale[1]),
            one, vec(q_shift[1]), vec(q_scale[1]),
            pl.BlockSpec((d, TN), lambda i, j: (0, kv_col(j))),
            pl.BlockSpec((d, TN), lambda i, j: (0, k_col0 + jnp.clip(j - 2 * t, 0, t - 1))),
            head, head, tab, tab, tab,
        ],
        out_specs=[out_spec(1), out_spec(t + 1), out_spec(2 * t + 1)],
        out_shape=[out, out, out],
        scratch_shapes=[
            pltpu.VMEM((TM, d), BF16),
            pltpu.VMEM((TM, d), BF16),
            pltpu.VMEM((TN // V7X_LANES, TM, V7X_LANES), F32),
            pltpu.VMEM((TN // V7X_LANES, TM, V7X_LANES), F32),
            pltpu.VMEM((2, SPLIT_STRIDE, y4_rows, V7X_LANES), F32),
        ],
        compiler_params=_params(2),
        name=f"qkv_r{r}",
    )(x, kv_norm_g, kv_shift[0], kv_scale[0], q_norm_g, q_shift[0], q_scale[0],
      w_kv, w_q, k_head_g, q_head_g, *tables)


def _band_attn_kernel(q_ref, kp_ref, kc_ref, vp_ref, vc_ref, o_ref, lse_ref, *, span):
    n = pl.program_id(1)
    qi = lax.broadcasted_iota(jnp.int32, (BLK, 2 * BLK), 0)
    kj = lax.broadcasted_iota(jnp.int32, (BLK, 2 * BLK), 1)
    dist = qi + BLK - kj
    mask = (dist >= 0) & (dist <= span) & ((n > 0) | (kj >= BLK))
    lane = lax.broadcasted_iota(jnp.int32, (BLK, V7X_LANES), 1)
    scale = 1.0 / math.sqrt(HEAD_DIM)
    lse_tile = jnp.zeros((BLK, V7X_LANES), F32)
    for h in range(HEADS_PER_GROUP):
        cols = slice(h * HEAD_DIM, (h + 1) * HEAD_DIM)
        q = q_ref[:, cols]
        k = jnp.concatenate([kp_ref[:, cols], kc_ref[:, cols]], axis=0)
        v = jnp.concatenate([vp_ref[:, cols], vc_ref[:, cols]], axis=0)
        sc = lax.dot_general(q, k, (((1,), (1,)), ((), ())),
                             preferred_element_type=F32) * scale
        sc = jnp.where(mask, sc, NEG)
        m = jnp.max(sc, axis=-1, keepdims=True)
        e = jnp.exp(sc - m)
        l = jnp.sum(e, axis=-1, keepdims=True)
        p = (e * (1.0 / l)).astype(BF16)
        o_ref[:, cols] = jnp.dot(p, v, preferred_element_type=F32)
        lse_tile = jnp.where(lane == h, m + jnp.log(l), lse_tile)
    lse_ref[...] = lse_tile


def _band_attn(q, k, v, group):
    r, rows, _ = q.shape
    cur = pl.BlockSpec((None, BLK, O_WIDTH), lambda j, n: (j, n, 0))
    prev = pl.BlockSpec((None, BLK, O_WIDTH), lambda j, n: (j, jnp.maximum(n - 1, 0), 0))
    return pl.pallas_call(
        functools.partial(_band_attn_kernel, span=GROUP_SPANS[group]),
        grid=(r, rows // BLK),
        in_specs=[cur, prev, cur, prev, cur],
        out_specs=[cur, pl.BlockSpec((None, BLK, V7X_LANES), lambda j, n: (j, n, 0))],
        out_shape=[jax.ShapeDtypeStruct((r, rows, O_WIDTH), F32),
                   jax.ShapeDtypeStruct((r, rows, V7X_LANES), F32)],
        compiler_params=_params(2),
        name=f"band_attn_r{r}",
    )(q, k, k, v, v)


TM_MIX = 512


def _rows_from_residues(src_ref, dst_ref, *, r, lanes=None):
    n = src_ref.shape[1]
    for res in range(r):
        rows = pl.ds(res, n, stride=r)
        if lanes is None:
            dst_ref[rows, :] = src_ref[res]
        else:
            for slab in range(lanes // V7X_LANES):
                dst_ref[slab, rows, :] = src_ref[res, :, _lane_slab(slab)]


def _mix_wo_kernel(o0_ref, o1_ref, o2_ref, l0_ref, l1_ref, l2_ref, w_ref, x_ref,
                   gate_ref, out_ref, hb_ref, on1_ref, on2_ref, ln1_ref, ln2_ref):
    @pl.when(pl.program_id(1) == 0)
    def _():
        r1, r2 = GROUP_DILATIONS[1], GROUP_DILATIONS[2]
        _rows_from_residues(l1_ref, ln1_ref, r=r1)
        _rows_from_residues(l2_ref, ln2_ref, r=r2)
        _rows_from_residues(o1_ref, on1_ref, r=r1, lanes=O_WIDTH)
        _rows_from_residues(o2_ref, on2_ref, r=r2, lanes=O_WIDTH)
        l0, l1, l2 = l0_ref[0], ln1_ref[...], ln2_ref[...]
        m = jnp.maximum(jnp.maximum(l0, l1), l2)
        e0, e1, e2 = jnp.exp(l0 - m), jnp.exp(l1 - m), jnp.exp(l2 - m)
        inv = 1.0 / (e0 + e1 + e2)
        a0, a1, a2 = e0 * inv, e1 * inv, e2 * inv
        for h in range(HEADS_PER_GROUP):
            cols = _lane_slab(h)
            o = (a0[:, h:h + 1] * o0_ref[0, :, cols] + a1[:, h:h + 1] * on1_ref[h]
                 + a2[:, h:h + 1] * on2_ref[h])
            hb_ref[:, cols] = o.astype(BF16)

    y = jnp.dot(hb_ref[...], w_ref[...].astype(BF16), preferred_element_type=F32)
    out_ref[...] = x_ref[...] + gate_ref[...] * y


def _mix_wo(outs, lses, w_o, x, gate):
    s, d = x.shape
    gate_col = gate[1] * (d // TN)

    def planes(width, r):
        return pl.BlockSpec((r, TM_MIX // r, width), lambda i, j: (0, i, 0))

    slabs = pltpu.VMEM((O_WIDTH // V7X_LANES, TM_MIX, V7X_LANES), F32)
    rows = pltpu.VMEM((TM_MIX, V7X_LANES), F32)
    return pl.pallas_call(
        _mix_wo_kernel,
        grid=(s // TM_MIX, d // TN),
        in_specs=[planes(O_WIDTH, r) for r in GROUP_DILATIONS]
        + [planes(V7X_LANES, r) for r in GROUP_DILATIONS]
        + [pl.BlockSpec((O_WIDTH, TN), lambda i, j: (0, j)),
           pl.BlockSpec((TM_MIX, TN), lambda i, j: (i, j)),
           pl.BlockSpec((1, TN), lambda i, j: (0, gate_col + j))],
        out_specs=pl.BlockSpec((TM_MIX, TN), lambda i, j: (i, j)),
        out_shape=jax.ShapeDtypeStruct((s, d), F32),
        scratch_shapes=[pltpu.VMEM((TM_MIX, O_WIDTH), BF16), slabs, slabs, rows, rows],
        compiler_params=_params(2),
        name="mix_wo",
    )(*outs, *lses, w_o, x, gate[0])


def kernel(x, c, positions, mod_w, mod_b, norm_mix_g, norm_ffn_g, conv_pw1_w, conv_pw1_b,
           conv_dw_w, conv_dw_b, conv_ln_g, conv_ln_b, conv_pw2_w, conv_pw2_b, kv_mod_w,
           kv_mod_b, kv_norm_g, w_kv, k_norm_g, w_q, q_norm_g, w_o, ffn_up_w, ffn_dw_w,
           ffn_dw_b, ffn_down_w):
    batch, s, d = x.shape
    assert (batch, s, d) == (1, SEQ, D_MODEL)
    x = x[0]
    c_col = c.reshape(d, 1)
    row = lambda v: v.reshape(1, -1)

    mod = _mod_matvec(c_col, mod_w, mod_b[:, None, :])
    kv_mod = _mod_matvec(c_col, kv_mod_w[None], kv_mod_b[None, None, :])[0]

    mvec = lambda l, q: (mod[l], q)
    ffn_dw_b3 = ffn_dw_b[:, None, :]

    def ffn(x, l):
        return _conv_ffn(x, row(norm_ffn_g[l]), mvec(l, 3), mvec(l, 4), mvec(l, 5), l,
                         ffn_up_w, ffn_dw_w, ffn_dw_b3, ffn_down_w)

    glu = _pw1_glu(x, row(norm_mix_g[0]), mvec(0, 0), mvec(0, 1),
                   conv_pw1_w[0], row(conv_pw1_b[0]))
    x = _conv_pw2(glu, conv_dw_w[0], row(conv_dw_b[0]), row(conv_ln_g[0]),
                  row(conv_ln_b[0]), conv_pw2_w[0], row(conv_pw2_b[0]), x, mvec(0, 2))
    x = ffn(x, 0)

    tables = _rope_tables(positions[0])
    outs, lses = [], []
    for g in range(N_GROUPS):
        k, v, q = _qkv_group(x, g, row(kv_norm_g), (kv_mod, 0), (kv_mod, 1),
                             row(norm_mix_g[1]), mvec(1, 0), mvec(1, 1),
                             w_kv, w_q[0], row(k_norm_g), row(q_norm_g[0]), tables)
        o, lse = _band_attn(q, k, v, g)
        outs.append(o)
        lses.append(lse)
    x = _mix_wo(outs, lses, w_o[0], x, mvec(1, 2))
    x = ffn(x, 1)
    return x[None]
```

```python
import functools
import math

import jax
import jax.numpy as jnp
from jax import lax
from jax.experimental import pallas as pl
from jax.experimental.pallas import tpu as pltpu

D_MODEL = 2048
SEQ = 8192
CONV_K = 31
FFN_CONV_K = 3
D_FF = 5632
GROUP_DILATIONS = (1, 4, 16)
GROUP_SPANS = (128, 128, 128)
N_GROUPS = 3
HEADS_PER_GROUP = 8
HEAD_DIM = 128
Q_WIDTH = N_GROUPS * HEADS_PER_GROUP * HEAD_DIM
O_WIDTH = HEADS_PER_GROUP * HEAD_DIM
ROT_DIM = HEAD_DIM // 4
ROPE_THETA = 500000.0
BLK = 128
EPS = 1e-6
NEG = -1e30

V7X_LANES = 128
V7X_SUBLANES = 8
V7X_BF16_ROWS_PER_VREG = 16
V7X_VMEM_BYTES = 64 * 1024 * 1024
VMEM_LIMIT = 56 * 1024 * 1024

TM = 1024
TN = 512
TN_HALF = TN // 2
ROW_CHUNK = 64

F32 = jnp.float32
BF16 = jnp.bfloat16


def _params(n_axes):
    return pltpu.CompilerParams(
        dimension_semantics=("arbitrary",) * n_axes,
        vmem_limit_bytes=VMEM_LIMIT)


def _sigmoid(x):
    return 1.0 / (1.0 + jnp.exp(-x))


def _silu(x):
    return x * _sigmoid(x)


def _lane_slab(slab):
    return slice(slab * V7X_LANES, (slab + 1) * V7X_LANES)


def _matvec_kernel(c_ref, w_ref, b_ref, o_ref, sb_ref, *, k_dim, tn):
    first = (pl.program_id(0) == 0) & (pl.program_id(1) == 0)

    @pl.when(first)
    def _():
        c = c_ref[...]
        sb_ref[...] = jnp.broadcast_to(_silu(c), (k_dim, V7X_LANES))

    n_groups = tn // V7X_LANES

    def body(t, accs):
        r0 = pl.multiple_of(t * ROW_CHUNK, ROW_CHUNK)
        s = sb_ref[pl.ds(r0, ROW_CHUNK), :]
        w = w_ref[pl.ds(r0, ROW_CHUNK), :]
        new = []
        for g in range(n_groups):
            p = w[:, _lane_slab(g)] * s
            a = accs[g]
            for u in range(ROW_CHUNK // V7X_SUBLANES):
                a = a + p[u * V7X_SUBLANES:(u + 1) * V7X_SUBLANES, :]
            new.append(a)
        return tuple(new)

    init = tuple(jnp.zeros((V7X_SUBLANES, V7X_LANES), F32) for _ in range(n_groups))
    accs = lax.fori_loop(0, k_dim // ROW_CHUNK, body, init)
    row = jnp.concatenate([jnp.sum(a, axis=0, keepdims=True) for a in accs], axis=1)
    o_ref[...] = row + b_ref[...]


def _mod_matvec(c_col, w, b):
    n_l, k_dim, n = w.shape
    tn = 1024
    return pl.pallas_call(
        functools.partial(_matvec_kernel, k_dim=k_dim, tn=tn),
        grid=(n_l, n // tn),
        in_specs=[
            pl.BlockSpec((k_dim, 1), lambda l, j: (0, 0)),
            pl.BlockSpec((None, k_dim, tn), lambda l, j: (l, 0, j)),
            pl.BlockSpec((None, 1, tn), lambda l, j: (l, 0, j)),
        ],
        out_specs=pl.BlockSpec((None, 1, tn), lambda l, j: (l, 0, j)),
        out_shape=jax.ShapeDtypeStruct((n_l, 1, n), F32),
        scratch_shapes=[pltpu.VMEM((k_dim, V7X_LANES), F32)],
        compiler_params=_params(2),
        name="mod_matvec",
    )(c_col, w, b)


MOD_ROWS = V7X_BF16_ROWS_PER_VREG
MOD_UNROLL = 4


def _modulate_rows(x_ref, targets, *, rows, h_row0=0):
    def body(t, carry):
        r0 = pl.multiple_of(t * MOD_ROWS, MOD_ROWS)
        x = x_ref[pl.ds(r0, MOD_ROWS), :]
        ms = jnp.mean(x * x, axis=-1, keepdims=True)
        y = x * lax.rsqrt(ms + EPS)
        h0 = pl.multiple_of(h_row0 + t * MOD_ROWS, MOD_ROWS)
        for g_ref, shift_ref, scale_ref, h_ref in targets:
            h = (y * g_ref[...]) * (1.0 + scale_ref[...]) + shift_ref[...]
            h_ref[pl.ds(h0, MOD_ROWS), :] = h.astype(BF16)
        return carry

    trips = rows // MOD_ROWS
    lax.fori_loop(0, trips, body, 0, unroll=min(MOD_UNROLL, trips))


def _pw1_glu_kernel(x_ref, g_ref, sh_ref, sc_ref, wa_ref, wg_ref, ba_ref, bg_ref,
                    o_ref, h_ref):
    @pl.when(pl.program_id(1) == 0)
    def _():
        _modulate_rows(x_ref, [(g_ref, sh_ref, sc_ref, h_ref)], rows=TM)

    w = jnp.concatenate([wa_ref[...], wg_ref[...]], axis=1)
    u = jnp.dot(h_ref[...], w, preferred_element_type=F32)
    a = u[:, :TN_HALF] + ba_ref[...]
    gt = u[:, TN_HALF:] + bg_ref[...]
    o_ref[...] = a * _sigmoid(gt)


def _pw1_glu(x, norm_g, shift, scale, w, b):
    s, d = x.shape
    nj = d // TN_HALF
    vec = lambda col: pl.BlockSpec((1, d), lambda i, j: (0, col))
    return pl.pallas_call(
        _pw1_glu_kernel,
        grid=(s // TM, nj),
        in_specs=[
            pl.BlockSpec((TM, d), lambda i, j: (i, 0)),
            pl.BlockSpec((1, d), lambda i, j: (0, 0)),
            vec(shift[1]), vec(scale[1]),
            pl.BlockSpec((d, TN_HALF), lambda i, j: (0, j)),
            pl.BlockSpec((d, TN_HALF), lambda i, j: (0, nj + j)),
            pl.BlockSpec((1, TN_HALF), lambda i, j: (0, j)),
            pl.BlockSpec((1, TN_HALF), lambda i, j: (0, nj + j)),
        ],
        out_specs=pl.BlockSpec((TM, TN_HALF), lambda i, j: (i, j)),
        out_shape=jax.ShapeDtypeStruct((s, d), F32),
        scratch_shapes=[pltpu.VMEM((TM, d), BF16)],
        compiler_params=_params(2),
        name="pw1_glu",
    )(x, norm_g, shift[0], scale[0], w, w, b, b)


CONV_HALO = 32
CONV_ROWS = 32


def _dwconv_ln_rows(gbuf_ref, dw_ref, db_ref, lg_ref, lb_ref, tmp_ref, hb_ref, *, rows, d):
    off = CONV_HALO - (CONV_K - 1)

    def body(t, carry):
        r0 = pl.multiple_of(t * CONV_ROWS, CONV_ROWS)
        for slab in range(d // V7X_LANES):
            lanes = _lane_slab(slab)
            acc = jnp.broadcast_to(db_ref[:, lanes], (CONV_ROWS, V7X_LANES))
            for k in range(CONV_K):
                tap = gbuf_ref[slab, pl.ds(r0 + off + k, CONV_ROWS, stride=1), :]
                acc = acc + tap * dw_ref[k:k + 1, lanes]
            tmp_ref[:, lanes] = acc
        u = tmp_ref[...]
        mu = jnp.mean(u, axis=-1, keepdims=True)
        uc = u - mu
        var = jnp.mean(uc * uc, axis=-1, keepdims=True)
        y = uc * lax.rsqrt(var + EPS) * lg_ref[...] + lb_ref[...]
        hb_ref[pl.ds(r0, CONV_ROWS), :] = _silu(y).astype(BF16)
        return carry

    lax.fori_loop(0, rows // CONV_ROWS, body, 0)


def _conv_pw2_kernel(g_ref, halo_ref, dw_ref, db_ref, lg_ref, lb_ref, w_ref, b_ref,
                     x_ref, gate_ref, o_ref, gbuf_ref, tmp_ref, hb_ref, *, d):
    i = pl.program_id(0)

    @pl.when(pl.program_id(1) == 0)
    def _():
        for slab in range(d // V7X_LANES):
            lanes = _lane_slab(slab)
            halo = halo_ref[:, lanes]
            gbuf_ref[slab, 0:CONV_HALO, :] = jnp.where(i == 0, jnp.zeros_like(halo), halo)
            gbuf_ref[slab, CONV_HALO:, :] = g_ref[:, lanes]
        _dwconv_ln_rows(gbuf_ref, dw_ref, db_ref, lg_ref, lb_ref, tmp_ref, hb_ref,
                        rows=TM, d=d)

    y = jnp.dot(hb_ref[...], w_ref[...], preferred_element_type=F32)
    o_ref[...] = x_ref[...] + gate_ref[...] * (y + b_ref[...])


def _conv_pw2(glu, dw_w, dw_b, ln_g, ln_b, w, b, x, gate):
    s, d = glu.shape
    halo_blocks_per_tile = TM // CONV_HALO
    gate_col = gate[1] * (d // TN)
    return pl.pallas_call(
        functools.partial(_conv_pw2_kernel, d=d),
        grid=(s // TM, d // TN),
        in_specs=[
            pl.BlockSpec((TM, d), lambda i, j: (i, 0)),
            pl.BlockSpec((CONV_HALO, d),
                         lambda i, j: (jnp.maximum(i * halo_blocks_per_tile - 1, 0), 0)),
            pl.BlockSpec((CONV_K, d), lambda i, j: (0, 0)),
            pl.BlockSpec((1, d), lambda i, j: (0, 0)),
            pl.BlockSpec((1, d), lambda i, j: (0, 0)),
            pl.BlockSpec((1, d), lambda i, j: (0, 0)),
            pl.BlockSpec((d, TN), lambda i, j: (0, j)),
            pl.BlockSpec((1, TN), lambda i, j: (0, j)),
            pl.BlockSpec((TM, TN), lambda i, j: (i, j)),
            pl.BlockSpec((1, TN), lambda i, j: (0, gate_col + j)),
        ],
        out_specs=pl.BlockSpec((TM, TN), lambda i, j: (i, j)),
        out_shape=jax.ShapeDtypeStruct((s, d), F32),
        scratch_shapes=[
            pltpu.VMEM((d // V7X_LANES, TM + CONV_HALO, V7X_LANES), F32),
            pltpu.VMEM((CONV_ROWS, d), F32),
            pltpu.VMEM((TM, d), BF16),
        ],
        compiler_params=_params(2),
        name="dwconv_pw2",
    )(glu, glu, dw_w, dw_b, ln_g, ln_b, w, b, x, gate[0])


FFN_HALO = V7X_BF16_ROWS_PER_VREG
TF = 512
FFN_OUT_LANES = 512


def _ffn_kernel(x_ref, xh_ref, g_ref, sh_ref, sc_ref, gate_ref, wg_ref, wv_ref,
                dw_ref, db_ref, wd_ref, o_ref, h_ref, u_ref):
    i = pl.program_id(0)

    @pl.when(pl.program_id(1) == 0)
    def _():
        target = [(g_ref, sh_ref, sc_ref, h_ref)]
        _modulate_rows(xh_ref, target, rows=FFN_HALO)
        _modulate_rows(x_ref, target, rows=TM, h_row0=FFN_HALO)
        o_ref[...] = x_ref[...]

    w_up = jnp.concatenate([wg_ref[...], wv_ref[...]], axis=1)
    u = jnp.dot(h_ref[...], w_up, preferred_element_type=F32)
    u_ref[...] = u
    keep_halo = jnp.where(i == 0, 0.0, 1.0)
    u_ref[0:FFN_HALO, 0:TF] = u[0:FFN_HALO, 0:TF] * keep_halo

    gt = db_ref[...] + dw_ref[2:3, :] * u_ref[pl.ds(FFN_HALO, TM), 0:TF]
    gt = gt + dw_ref[1:2, :] * u_ref[pl.ds(FFN_HALO - 1, TM), 0:TF]
    gt = gt + dw_ref[0:1, :] * u_ref[pl.ds(FFN_HALO - 2, TM), 0:TF]
    val = u_ref[pl.ds(FFN_HALO, TM), TF:2 * TF]
    act = (_silu(gt) * val).astype(BF16)
    for c0 in range(0, o_ref.shape[1], FFN_OUT_LANES):
        cols = slice(c0, c0 + FFN_OUT_LANES)
        y = jnp.dot(act, wd_ref[:, cols], preferred_element_type=F32)
        o_ref[:, cols] += gate_ref[:, cols] * y


def _conv_ffn(x, norm_g, shift, scale, gate, layer, up_w, dw_w, dw_b, down_w):
    s, d = x.shape
    f = down_w.shape[1]
    nf = f // TF
    halo_blocks_per_tile = TM // FFN_HALO
    vec = lambda col: pl.BlockSpec((1, d), lambda i, j: (0, col))
    return pl.pallas_call(
        _ffn_kernel,
        grid=(s // TM, nf),
        in_specs=[
            pl.BlockSpec((TM, d), lambda i, j: (i, 0), pipeline_mode=pl.Buffered(1)),
            pl.BlockSpec((FFN_HALO, d),
                         lambda i, j: (jnp.maximum(i * halo_blocks_per_tile - 1, 0), 0)),
            pl.BlockSpec((1, d), lambda i, j: (0, 0)),
            vec(shift[1]), vec(scale[1]), vec(gate[1]),
            pl.BlockSpec((None, d, TF), lambda i, j: (layer, 0, j)),
            pl.BlockSpec((None, d, TF), lambda i, j: (layer, 0, nf + j)),
            pl.BlockSpec((None, FFN_CONV_K, TF), lambda i, j: (layer, 0, j)),
            pl.BlockSpec((None, 1, TF), lambda i, j: (layer, 0, j)),
            pl.BlockSpec((None, TF, d), lambda i, j: (layer, j, 0)),
        ],
        out_specs=pl.BlockSpec((TM, d), lambda i, j: (i, 0)),
        out_shape=jax.ShapeDtypeStruct((s, d), F32),
        scratch_shapes=[
            pltpu.VMEM((TM + FFN_HALO, d), BF16),
            pltpu.VMEM((TM + FFN_HALO, 2 * TF), F32),
        ],
        compiler_params=_params(2),
        name="conv_ffn",
    )(x, x, norm_g, shift[0], scale[0], gate[0], up_w, up_w, dw_w, dw_b, down_w)


def _rope_table_kernel(pos_ref, freq_ref, cos_ref, sin_lo_ref, sin_hi_ref):
    pos = pos_ref[...].astype(F32)
    ang = pos * freq_ref[...]
    lane = lax.broadcasted_iota(jnp.int32, ang.shape, 1)
    c = jnp.cos(ang)
    sn = jnp.sin(ang)
    half = ROT_DIM // 2
    cos_ref[...] = jnp.where(lane < ROT_DIM, c, 1.0)
    sin_lo_ref[...] = jnp.where(lane < half, -sn, 0.0)
    sin_hi_ref[...] = jnp.where((lane >= half) & (lane < ROT_DIM), sn, 0.0)


def _rope_tables(positions):
    s = positions.shape[0]
    rows = 1024
    inv_freq = ROPE_THETA ** (-jnp.arange(0, ROT_DIM, 2, dtype=F32) / ROT_DIM)
    lane_freq = jnp.concatenate(
        [inv_freq, inv_freq, jnp.zeros((HEAD_DIM - ROT_DIM,), F32)])[None, :]
    out = jax.ShapeDtypeStruct((s, HEAD_DIM), F32)
    spec = pl.BlockSpec((rows, HEAD_DIM), lambda i: (i, 0))
    return pl.pallas_call(
        _rope_table_kernel,
        grid=(s // rows,),
        in_specs=[pl.BlockSpec((rows, 1), lambda i: (i, 0)),
                  pl.BlockSpec((1, HEAD_DIM), lambda i: (0, 0))],
        out_specs=[spec, spec, spec],
        out_shape=[out, out, out],
        compiler_params=_params(1),
        name="rope_tables",
    )(positions.reshape(s, 1), lane_freq)


TILES_PER_GROUP = O_WIDTH // TN


def _norm_rope_head(q, hg, cos, sin_lo, sin_hi):
    ms = jnp.mean(q * q, axis=-1, keepdims=True)
    qn = q * lax.rsqrt(ms + EPS) * hg
    hi_to_lo = pltpu.roll(qn, HEAD_DIM - ROT_DIM // 2, axis=1)
    lo_to_hi = pltpu.roll(qn, ROT_DIM // 2, axis=1)
    return qn * cos + hi_to_lo * sin_lo + lo_to_hi * sin_hi


SPLIT_STRIDE = 4


def _finish_tile(y_ref, out_ref, hg_ref, table_refs, y4_ref, *, r):
    rows = TM // r
    two_pass = r > SPLIT_STRIDE
    r_outer = r // SPLIT_STRIDE
    for head in range(TN // HEAD_DIM):
        if hg_ref is not None:
            cos, sin_lo, sin_hi = (t[...] for t in table_refs)
            y_ref[head] = _norm_rope_head(y_ref[head], hg_ref[...], cos, sin_lo, sin_hi)
        if two_pass:
            for p in range(SPLIT_STRIDE):
                y4_ref[head % 2, p] = y_ref[head, pl.ds(p, TM // SPLIT_STRIDE,
                                                        stride=SPLIT_STRIDE), :]
        for res in range(r):
            if r == 1:
                y = y_ref[head]
            elif two_pass:
                p, q = res % SPLIT_STRIDE, res // SPLIT_STRIDE
                y = y4_ref[head % 2, p, pl.ds(q, rows, stride=r_outer), :]
            else:
                y = y_ref[head, pl.ds(res, rows, stride=r), :]
            out_ref[res, :, _lane_slab(head)] = y.astype(BF16)


def _qkv_group_kernel(x_ref, gkv_ref, shkv_ref, sckv_ref, gq_ref, shq_ref, scq_ref,
                      wkv_ref, wq_ref, kg_ref, qg_ref, cos_ref, slo_ref, shi_ref,
                      k_out, v_out, q_out, hkv_ref, hq_ref, ya_ref, yb_ref, y4_ref, *, r):
    s = pl.program_id(1)
    t = TILES_PER_GROUP
    y_slots = (ya_ref, yb_ref)
    outs = (k_out, v_out, q_out)
    head_gains = (kg_ref, None, qg_ref)
    tables = (cos_ref, slo_ref, shi_ref)

    @pl.when(s == 0)
    def _():
        _modulate_rows(x_ref, [(gkv_ref, shkv_ref, sckv_ref, hkv_ref),
                               (gq_ref, shq_ref, scq_ref, hq_ref)], rows=TM)

    for step in range(3 * t + 1):
        @pl.when(s == step)
        def _(step=step):
            if step < 3 * t:
                h_ref, w_ref = (hq_ref, wq_ref) if step // t == 2 else (hkv_ref, wkv_ref)
                y = jnp.dot(h_ref[...], w_ref[...], preferred_element_type=F32)
                for head in range(TN // HEAD_DIM):
                    y_slots[step % 2][head] = y[:, _lane_slab(head)]
            if step > 0:
                kind = (step - 1) // t
                _finish_tile(y_slots[(step - 1) % 2], outs[kind], head_gains[kind], tables,
                             y4_ref, r=r)


def _qkv_group(x, group, kv_norm_g, kv_shift, kv_scale, q_norm_g, q_shift, q_scale,
               w_kv, w_q, k_head_g, q_head_g, tables):
    s, d = x.shape
    r = GROUP_DILATIONS[group]
    t = TILES_PER_GROUP
    k_col0 = group * t
    v_col0 = Q_WIDTH // TN + group * t
    vec = lambda col: pl.BlockSpec((1, d), lambda i, j: (0, col))
    one = pl.BlockSpec((1, d), lambda i, j: (0, 0))
    tab = pl.BlockSpec((TM, HEAD_DIM), lambda i, j: (i, 0))
    head =pl.BlockSpec((1, HEAD_DIM), lambda i, j: (0, 0))

    y4_rows = TM // SPLIT_STRIDE if r > SPLIT_STRIDE else V7X_SUBLANES

    def kv_col(j):
        return jnp.where(j < t, k_col0 + j, v_col0 + jnp.minimum(j - t, t - 1))

    def out_spec(first_step):
        return pl.BlockSpec(
            (r, TM // r, TN),
            lambda i, j: (0, i, jnp.clip(j - first_step, 0, t - 1)))

    out = jax.ShapeDtypeStruct((r, s // r, O_WIDTH), BF16)
    return pl.pallas_call(
        functools.partial(_qkv_group_kernel, r=r),
        grid=(s // TM, 3 * t + 1),
        in_specs=[
            pl.BlockSpec((TM, d), lambda i, j: (i, 0)),
            one, vec(kv_shift[1]), vec(kv_scale[1]),
            one, vec(q_shift[1]), vec(q_scale[1]),
            pl.BlockSpec((d, TN), lambda i, j: (0, kv_col(j))),
            pl.BlockSpec((d, TN), lambda i, j: (0, k_col0 + jnp.clip(j - 2 * t, 0, t - 1))),
            head, head, tab, tab, tab,
        ],
        out_specs=[out_spec(1), out_spec(t + 1), out_spec(2 * t + 1)],
        out_shape=[out, out, out],
        scratch_shapes=[
            pltpu.VMEM((TM, d), BF16),
            pltpu.VMEM((TM, d), BF16),
            pltpu.VMEM((TN // V7X_LANES, TM, V7X_LANES), F32),
            pltpu.VMEM((TN // V7X_LANES, TM, V7X_LANES), F32),
            pltpu.VMEM((2, SPLIT_STRIDE, y4_rows, V7X_LANES), F32),
        ],
        compiler_params=_params(2),
        name=f"qkv_r{r}",
    )(x, kv_norm_g, kv_shift[0], kv_scale[0], q_norm_g, q_shift[0], q_scale[0],
      w_kv, w_q, k_head_g, q_head_g, *tables)


def _band_attn_kernel(q_ref, kp_ref, kc_ref, vp_ref, vc_ref, o_ref, lse_ref, *, span):
    n = pl.program_id(1)
    qi = lax.broadcasted_iota(jnp.int32, (BLK, 2 * BLK), 0)
    kj = lax.broadcasted_iota(jnp.int32, (BLK, 2 * BLK), 1)
    dist = qi + BLK - kj
    mask = (dist >= 0) & (dist <= span) & ((n > 0) | (kj >= BLK))
    lane = lax.broadcasted_iota(jnp.int32, (BLK, V7X_LANES), 1)
    scale = 1.0 / math.sqrt(HEAD_DIM)
    lse_tile = jnp.zeros((BLK, V7X_LANES), F32)
    for h in range(HEADS_PER_GROUP):
        cols = slice(h * HEAD_DIM, (h + 1) * HEAD_DIM)
        q = q_ref[:, cols]
        k = jnp.concatenate([kp_ref[:, cols], kc_ref[:, cols]], axis=0)
        v = jnp.concatenate([vp_ref[:, cols], vc_ref[:, cols]], axis=0)
        sc = lax.dot_general(q, k, (((1,), (1,)), ((), ())),
                             preferred_element_type=F32) * scale
        sc = jnp.where(mask, sc, NEG)
        m = jnp.max(sc, axis=-1, keepdims=True)
        e = jnp.exp(sc - m)
        l = jnp.sum(e, axis=-1, keepdims=True)
        p = (e * (1.0 / l)).astype(BF16)
        o_ref[:, cols] = jnp.dot(p, v, preferred_element_type=F32)
        lse_tile = jnp.where(lane == h, m + jnp.log(l), lse_tile)
    lse_ref[...] = lse_tile


def _band_attn(q, k, v, group):
    r, rows, _ = q.shape
    cur = pl.BlockSpec((None, BLK, O_WIDTH), lambda j, n: (j, n, 0))
    prev = pl.BlockSpec((None, BLK, O_WIDTH), lambda j, n: (j, jnp.maximum(n - 1, 0), 0))
    return pl.pallas_call(
        functools.partial(_band_attn_kernel, span=GROUP_SPANS[group]),
        grid=(r, rows // BLK),
        in_specs=[cur, prev, cur, prev, cur],
        out_specs=[cur, pl.BlockSpec((None, BLK, V7X_LANES), lambda j, n: (j, n, 0))],
        out_shape=[jax.ShapeDtypeStruct((r, rows, O_WIDTH), F32),
                   jax.ShapeDtypeStruct((r, rows, V7X_LANES), F32)],
        compiler_params=_params(2),
        name=f"band_attn_r{r}",
    )(q, k, k, v, v)


TM_MIX = 512


def _rows_from_residues(src_ref, dst_ref, *, r, lanes=None):
    n = src_ref.shape[1]
    for res in range(r):
        rows = pl.ds(res, n, stride=r)
        if lanes is None:
            dst_ref[rows, :] = src_ref[res]
        else:
            for slab in range(lanes // V7X_LANES):
                dst_ref[slab, rows, :] = src_ref[res, :, _lane_slab(slab)]


def _mix_wo_kernel(o0_ref, o1_ref, o2_ref, l0_ref, l1_ref, l2_ref, w_ref, x_ref,
                   gate_ref, out_ref, hb_ref, on1_ref, on2_ref, ln1_ref, ln2_ref):
    @pl.when(pl.program_id(1) == 0)
    def _():
        r1, r2 = GROUP_DILATIONS[1], GROUP_DILATIONS[2]
        _rows_from_residues(l1_ref, ln1_ref, r=r1)
        _rows_from_residues(l2_ref, ln2_ref, r=r2)
        _rows_from_residues(o1_ref, on1_ref, r=r1, lanes=O_WIDTH)
        _rows_from_residues(o2_ref, on2_ref, r=r2, lanes=O_WIDTH)
        l0, l1, l2 = l0_ref[0], ln1_ref[...], ln2_ref[...]
        m = jnp.maximum(jnp.maximum(l0, l1), l2)
        e0, e1, e2 = jnp.exp(l0 - m), jnp.exp(l1 - m), jnp.exp(l2 - m)
        inv = 1.0 / (e0 + e1 + e2)
        a0, a1, a2 = e0 * inv, e1 * inv, e2 * inv
        for h in range(HEADS_PER_GROUP):
            cols = _lane_slab(h)
            o = (a0[:, h:h + 1] * o0_ref[0, :, cols] + a1[:, h:h + 1] * on1_ref[h]
                 + a2[:, h:h + 1] * on2_ref[h])
            hb_ref[:, cols] = o.astype(BF16)

    y = jnp.dot(hb_ref[...], w_ref[...], preferred_element_type=F32)
    out_ref[...] = x_ref[...] + gate_ref[...] * y


def _mix_wo(outs, lses, w_o, x, gate):
    s, d = x.shape
    gate_col = gate[1] * (d // TN)

    def planes(width, r):
        return pl.BlockSpec((r, TM_MIX // r, width), lambda i, j: (0, i, 0))

    slabs = pltpu.VMEM((O_WIDTH // V7X_LANES, TM_MIX, V7X_LANES), F32)
    rows = pltpu.VMEM((TM_MIX, V7X_LANES), F32)
    return pl.pallas_call(
        _mix_wo_kernel,
        grid=(s // TM_MIX, d // TN),
        in_specs=[planes(O_WIDTH, r) for r in GROUP_DILATIONS]
        + [planes(V7X_LANES, r) for r in GROUP_DILATIONS]
        + [pl.BlockSpec((O_WIDTH, TN), lambda i, j: (0, j)),
           pl.BlockSpec((TM_MIX, TN), lambda i, j: (i, j)),
           pl.BlockSpec((1, TN), lambda i, j: (0, gate_col + j))],
        out_specs=pl.BlockSpec((TM_MIX, TN), lambda i, j: (i, j)),
        out_shape=jax.ShapeDtypeStruct((s, d), F32),
        scratch_shapes=[pltpu.VMEM((TM_MIX, O_WIDTH), BF16), slabs, slabs, rows, rows],
        compiler_params=_params(2),
        name="mix_wo",
    )(*outs, *lses, w_o, x, gate[0])


def kernel(x, c, positions, mod_w, mod_b, norm_mix_g, norm_ffn_g, conv_pw1_w, conv_pw1_b,
           conv_dw_w, conv_dw_b, conv_ln_g, conv_ln_b, conv_pw2_w, conv_pw2_b, kv_mod_w,
           kv_mod_b, kv_norm_g, w_kv, k_norm_g, w_q, q_norm_g, w_o, ffn_up_w, ffn_dw_w,
           ffn_dw_b, ffn_down_w):
    batch, s, d = x.shape
    assert (batch, s, d) == (1, SEQ, D_MODEL)
    x = x[0]
    c_col = c.reshape(d, 1)
    row = lambda v: v.reshape(1, -1)

    mod = _mod_matvec(c_col, mod_w, mod_b[:, None, :])
    kv_mod = _mod_matvec(c_col, kv_mod_w[None], kv_mod_b[None, None, :])[0]

    mvec = lambda l, q: (mod[l], q)
    ffn_dw_b3 = ffn_dw_b[:, None, :]

    conv_pw1_w, conv_pw2_w, w_kv, w_q, w_o, ffn_up_w, ffn_down_w = (
        w.astype(BF16)
        for w in (conv_pw1_w, conv_pw2_w, w_kv, w_q, w_o, ffn_up_w, ffn_down_w))

    def ffn(x, l):
        return _conv_ffn(x, row(norm_ffn_g[l]), mvec(l, 3), mvec(l, 4), mvec(l, 5), l,
                         ffn_up_w, ffn_dw_w, ffn_dw_b3, ffn_down_w)

    glu = _pw1_glu(x, row(norm_mix_g[0]), mvec(0, 0), mvec(0, 1),
                   conv_pw1_w[0], row(conv_pw1_b[0]))
    x = _conv_pw2(glu, conv_dw_w[0], row(conv_dw_b[0]), row(conv_ln_g[0]),
                  row(conv_ln_b[0]), conv_pw2_w[0], row(conv_pw2_b[0]), x, mvec(0, 2))
    x = ffn(x, 0)

    tables = _rope_tables(positions[0])
    outs, lses = [], []
    for g in range(N_GROUPS):
        k, v, q = _qkv_group(x, g, row(kv_norm_g), (kv_mod, 0), (kv_mod, 1),
                             row(norm_mix_g[1]), mvec(1, 0), mvec(1, 1),
                             w_kv, w_q[0], row(k_norm_g), row(q_norm_g[0]), tables)
        o, lse = _band_attn(q, k, v, g)
        outs.append(o)
        lses.append(lse)
    x = _mix_wo(outs, lses, w_o[0], x, mvec(1, 2))
    x = ffn(x, 1)
    return x[None]
```

```python
import functools
import math

import jax
import jax.numpy as jnp
from jax import lax
from jax.experimental import pallas as pl
from jax.experimental.pallas import tpu as pltpu

D_MODEL = 2048
SEQ = 8192
CONV_K = 31
FFN_CONV_K = 3
D_FF = 5632
GROUP_DILATIONS = (1, 4, 16)
GROUP_SPANS = (128, 128, 128)
N_GROUPS = 3
HEADS_PER_GROUP = 8
HEAD_DIM = 128
Q_WIDTH = N_GROUPS * HEADS_PER_GROUP * HEAD_DIM
O_WIDTH = HEADS_PER_GROUP * HEAD_DIM
ROT_DIM = HEAD_DIM // 4
ROPE_THETA = 500000.0
BLK = 128
EPS = 1e-6
NEG = -1e30

V7X_LANES = 128
V7X_SUBLANES = 8
V7X_BF16_ROWS_PER_VREG = 16
V7X_VMEM_BYTES = 64 * 1024 * 1024
VMEM_LIMIT = 56 * 1024 * 1024

TM = 1024
TN = 512
TN_HALF = TN // 2
ROW_CHUNK = 64

F32 = jnp.float32
BF16 = jnp.bfloat16


def _params(n_axes):
    return pltpu.CompilerParams(
        dimension_semantics=("arbitrary",) * n_axes,
        vmem_limit_bytes=VMEM_LIMIT)


def _sigmoid(x):
    return 1.0 / (1.0 + jnp.exp(-x))


def _silu(x):
    return x * _sigmoid(x)


def _lane_slab(slab):
    return slice(slab * V7X_LANES, (slab + 1) * V7X_LANES)


def _matvec_kernel(c_ref, w_ref, b_ref, o_ref, sb_ref, *, k_dim, tn):
    first = (pl.program_id(0) == 0) & (pl.program_id(1) == 0)

    @pl.when(first)
    def _():
        c = c_ref[...]
        sb_ref[...] = jnp.broadcast_to(_silu(c), (k_dim, V7X_LANES))

    n_groups = tn // V7X_LANES

    def body(t, accs):
        r0 = pl.multiple_of(t * ROW_CHUNK, ROW_CHUNK)
        s = sb_ref[pl.ds(r0, ROW_CHUNK), :]
        w = w_ref[pl.ds(r0, ROW_CHUNK), :]
        new = []
        for g in range(n_groups):
            p = w[:, _lane_slab(g)] * s
            a = accs[g]
            for u in range(ROW_CHUNK // V7X_SUBLANES):
                a = a + p[u * V7X_SUBLANES:(u + 1) * V7X_SUBLANES, :]
            new.append(a)
        return tuple(new)

    init = tuple(jnp.zeros((V7X_SUBLANES, V7X_LANES), F32) for _ in range(n_groups))
    accs = lax.fori_loop(0, k_dim // ROW_CHUNK, body, init)
    row = jnp.concatenate([jnp.sum(a, axis=0, keepdims=True) for a in accs], axis=1)
    o_ref[...] = row + b_ref[...]


def _mod_matvec(c_col, w, b):
    n_l, k_dim, n = w.shape
    tn = 1024
    return pl.pallas_call(
        functools.partial(_matvec_kernel, k_dim=k_dim, tn=tn),
        grid=(n_l, n // tn),
        in_specs=[
            pl.BlockSpec((k_dim, 1), lambda l, j: (0, 0)),
            pl.BlockSpec((None, k_dim, tn), lambda l, j: (l, 0, j)),
            pl.BlockSpec((None, 1, tn), lambda l, j: (l, 0, j)),
        ],
        out_specs=pl.BlockSpec((None, 1, tn), lambda l, j: (l, 0, j)),
        out_shape=jax.ShapeDtypeStruct((n_l, 1, n), F32),
        scratch_shapes=[pltpu.VMEM((k_dim, V7X_LANES), F32)],
        compiler_params=_params(2),
        name="mod_matvec",
    )(c_col, w, b)


MOD_ROWS = V7X_BF16_ROWS_PER_VREG
MOD_UNROLL = 4


def _modulate_rows(x_ref, targets, *, rows, h_row0=0):
    def body(t, carry):
        r0 = pl.multiple_of(t * MOD_ROWS, MOD_ROWS)
        x = x_ref[pl.ds(r0, MOD_ROWS), :]
        ms = jnp.mean(x * x, axis=-1, keepdims=True)
        y = x * lax.rsqrt(ms + EPS)
        h0 = pl.multiple_of(h_row0 + t * MOD_ROWS, MOD_ROWS)
        for g_ref, shift_ref, scale_ref, h_ref in targets:
            h = (y * g_ref[...]) * (1.0 + scale_ref[...]) + shift_ref[...]
            h_ref[pl.ds(h0, MOD_ROWS), :] = h.astype(BF16)
        return carry

    trips = rows // MOD_ROWS
    lax.fori_loop(0, trips, body, 0, unroll=min(MOD_UNROLL, trips))


def _pw1_glu_kernel(x_ref, g_ref, sh_ref, sc_ref, wa_ref, wg_ref, ba_ref, bg_ref,
                    o_ref, h_ref):
    @pl.when(pl.program_id(1) == 0)
    def _():
        _modulate_rows(x_ref, [(g_ref, sh_ref, sc_ref, h_ref)], rows=TM)

    w = jnp.concatenate([wa_ref[...], wg_ref[...]], axis=1)
    u = jnp.dot(h_ref[...], w, preferred_element_type=F32)
    a = u[:, :TN_HALF] + ba_ref[...]
    gt = u[:, TN_HALF:] + bg_ref[...]
    o_ref[...] = a * _sigmoid(gt)


def _pw1_glu(x, norm_g, shift, scale, w, b):
    s, d = x.shape
    nj = d // TN_HALF
    vec = lambda col: pl.BlockSpec((1, d), lambda i, j: (0, col))
    return pl.pallas_call(
        _pw1_glu_kernel,
        grid=(s // TM, nj),
        in_specs=[
            pl.BlockSpec((TM, d), lambda i, j: (i, 0)),
            pl.BlockSpec((1, d), lambda i, j: (0, 0)),
            vec(shift[1]), vec(scale[1]),
            pl.BlockSpec((d, TN_HALF), lambda i, j: (0, j)),
            pl.BlockSpec((d, TN_HALF), lambda i, j: (0, nj + j)),
            pl.BlockSpec((1, TN_HALF), lambda i, j: (0, j)),
            pl.BlockSpec((1, TN_HALF), lambda i, j: (0, nj + j)),
        ],
        out_specs=pl.BlockSpec((TM, TN_HALF), lambda i, j: (i, j)),
        out_shape=jax.ShapeDtypeStruct((s, d), F32),
        scratch_shapes=[pltpu.VMEM((TM, d), BF16)],
        compiler_params=_params(2),
        name="pw1_glu",
    )(x, norm_g, shift[0], scale[0], w, w, b, b)


CONV_HALO = 32
CONV_ROWS = 32


def _dwconv_ln_rows(gbuf_ref, dw_ref, db_ref, lg_ref, lb_ref, tmp_ref, hb_ref, *, rows, d):
    off = CONV_HALO - (CONV_K - 1)

    def body(t, carry):
        r0 = pl.multiple_of(t * CONV_ROWS, CONV_ROWS)
        for slab in range(d // V7X_LANES):
            lanes = _lane_slab(slab)
            acc = jnp.broadcast_to(db_ref[:, lanes], (CONV_ROWS, V7X_LANES))
            for k in range(CONV_K):
                tap = gbuf_ref[slab, pl.ds(r0 + off + k, CONV_ROWS, stride=1), :]
                acc = acc + tap * dw_ref[k:k + 1, lanes]
            tmp_ref[:, lanes] = acc
        u = tmp_ref[...]
        mu = jnp.mean(u, axis=-1, keepdims=True)
        uc = u - mu
        var = jnp.mean(uc * uc, axis=-1, keepdims=True)
        y = uc * lax.rsqrt(var + EPS) * lg_ref[...] + lb_ref[...]
        hb_ref[pl.ds(r0, CONV_ROWS), :] = _silu(y).astype(BF16)
        return carry

    lax.fori_loop(0, rows // CONV_ROWS, body, 0)


def _conv_pw2_kernel(g_ref, halo_ref, dw_ref, db_ref, lg_ref, lb_ref, w_ref, b_ref,
                     x_ref, gate_ref, o_ref, gbuf_ref, tmp_ref, hb_ref, *, d):
    i = pl.program_id(0)

    @pl.when(pl.program_id(1) == 0)
    def _():
        for slab in range(d // V7X_LANES):
            lanes = _lane_slab(slab)
            halo = halo_ref[:, lanes]
            gbuf_ref[slab, 0:CONV_HALO, :] = jnp.where(i == 0, jnp.zeros_like(halo), halo)
            gbuf_ref[slab, CONV_HALO:, :] = g_ref[:, lanes]
        _dwconv_ln_rows(gbuf_ref, dw_ref, db_ref, lg_ref, lb_ref, tmp_ref, hb_ref,
                        rows=TM, d=d)

    y = jnp.dot(hb_ref[...], w_ref[...], preferred_element_type=F32)
    o_ref[...] = x_ref[...] + gate_ref[...] * (y + b_ref[...])


def _conv_pw2(glu, dw_w, dw_b, ln_g, ln_b, w, b, x, gate):
    s, d = glu.shape
    halo_blocks_per_tile = TM // CONV_HALO
    gate_col = gate[1] * (d // TN)
    return pl.pallas_call(
        functools.partial(_conv_pw2_kernel, d=d),
        grid=(s // TM, d // TN),
        in_specs=[
            pl.BlockSpec((TM, d), lambda i, j: (i, 0)),
            pl.BlockSpec((CONV_HALO, d),
                         lambda i, j: (jnp.maximum(i * halo_blocks_per_tile - 1, 0), 0)),
            pl.BlockSpec((CONV_K, d), lambda i, j: (0, 0)),
            pl.BlockSpec((1, d), lambda i, j: (0, 0)),
            pl.BlockSpec((1, d), lambda i, j: (0, 0)),
            pl.BlockSpec((1, d), lambda i, j: (0, 0)),
            pl.BlockSpec((d, TN), lambda i, j: (0, j)),
            pl.BlockSpec((1, TN), lambda i, j: (0, j)),
            pl.BlockSpec((TM, TN), lambda i, j: (i, j)),
            pl.BlockSpec((1, TN), lambda i, j: (0, gate_col + j)),
        ],
        out_specs=pl.BlockSpec((TM, TN), lambda i, j: (i, j)),
        out_shape=jax.ShapeDtypeStruct((s, d), F32),
        scratch_shapes=[
            pltpu.VMEM((d // V7X_LANES, TM + CONV_HALO, V7X_LANES), F32),
            pltpu.VMEM((CONV_ROWS, d), F32),
            pltpu.VMEM((TM, d), BF16),
        ],
        compiler_params=_params(2),
        name="dwconv_pw2",
    )(glu, glu, dw_w, dw_b, ln_g, ln_b, w, b, x, gate[0])


FFN_HALO = V7X_BF16_ROWS_PER_VREG
TF = 512
FFN_OUT_LANES = 512


def _ffn_kernel(x_ref, xh_ref, g_ref, sh_ref, sc_ref, gate_ref, wg_ref, wv_ref,
                dw_ref, db_ref, wd_ref, o_ref, h_ref, u_ref):
    i = pl.program_id(0)

    @pl.when(pl.program_id(1) == 0)
    def _():
        target = [(g_ref, sh_ref, sc_ref, h_ref)]
        _modulate_rows(xh_ref, target, rows=FFN_HALO)
        _modulate_rows(x_ref, target, rows=TM, h_row0=FFN_HALO)
        o_ref[...] = x_ref[...]

    w_up = jnp.concatenate([wg_ref[...], wv_ref[...]], axis=1)
    u = jnp.dot(h_ref[...], w_up, preferred_element_type=F32)
    u_ref[...] = u
    keep_halo = jnp.where(i == 0, 0.0, 1.0)
    u_ref[0:FFN_HALO, 0:TF] = u[0:FFN_HALO, 0:TF] * keep_halo

    gt = db_ref[...] + dw_ref[2:3, :] * u_ref[pl.ds(FFN_HALO, TM), 0:TF]
    gt = gt + dw_ref[1:2, :] * u_ref[pl.ds(FFN_HALO - 1, TM), 0:TF]
    gt = gt + dw_ref[0:1, :] * u_ref[pl.ds(FFN_HALO - 2, TM), 0:TF]
    val = u_ref[pl.ds(FFN_HALO, TM), TF:2 * TF]
    act = (_silu(gt) * val).astype(BF16)
    for c0 in range(0, o_ref.shape[1], FFN_OUT_LANES):
        cols = slice(c0, c0 + FFN_OUT_LANES)
        y = jnp.dot(act, wd_ref[:, cols], preferred_element_type=F32)
        o_ref[:, cols] += gate_ref[:, cols] * y


def _conv_ffn(x, norm_g, shift, scale, gate, layer, up_w, dw_w, dw_b, down_w):
    s, d = x.shape
    f = down_w.shape[1]
    nf = f // TF
    halo_blocks_per_tile = TM // FFN_HALO
    vec = lambda col: pl.BlockSpec((1, d), lambda i, j: (0, col))
    return pl.pallas_call(
        _ffn_kernel,
        grid=(s // TM, nf),
        in_specs=[
            pl.BlockSpec((TM, d), lambda i, j: (i, 0), pipeline_mode=pl.Buffered(1)),
            pl.BlockSpec((FFN_HALO, d),
                         lambda i, j: (jnp.maximum(i * halo_blocks_per_tile - 1, 0), 0)),
            pl.BlockSpec((1, d), lambda i, j: (0, 0)),
            vec(shift[1]), vec(scale[1]), vec(gate[1]),
            pl.BlockSpec((None, d, TF), lambda i, j: (layer, 0, j)),
            pl.BlockSpec((None, d, TF), lambda i, j: (layer, 0, nf + j)),
            pl.BlockSpec((None, FFN_CONV_K, TF), lambda i, j: (layer, 0, j)),
            pl.BlockSpec((None, 1, TF), lambda i, j: (layer, 0, j)),
            pl.BlockSpec((None, TF, d), lambda i, j: (layer, j, 0)),
        ],
        out_specs=pl.BlockSpec((TM, d), lambda i, j: (i, 0)),
        out_shape=jax.ShapeDtypeStruct((s, d), F32),
        scratch_shapes=[
            pltpu.VMEM((TM + FFN_HALO, d), BF16),
            pltpu.VMEM((TM + FFN_HALO, 2 * TF), F32),
        ],
        compiler_params=_params(2),
        name="conv_ffn",
    )(x, x, norm_g, shift[0], scale[0], gate[0], up_w, up_w, dw_w, dw_b, down_w)


def _rope_table_kernel(pos_ref, freq_ref, cos_ref, sin_lo_ref, sin_hi_ref):
    pos = pos_ref[...].astype(F32)
    ang = pos * freq_ref[...]
    lane = lax.broadcasted_iota(jnp.int32, ang.shape, 1)
    c = jnp.cos(ang)
    sn = jnp.sin(ang)
    half = ROT_DIM // 2
    cos_ref[...] = jnp.where(lane < ROT_DIM, c, 1.0)
    sin_lo_ref[...] = jnp.where(lane < half, -sn, 0.0)
    sin_hi_ref[...] = jnp.where((lane >= half) & (lane < ROT_DIM), sn, 0.0)


def _rope_tables(positions):
    s = positions.shape[0]
    rows = 1024
    inv_freq = ROPE_THETA ** (-jnp.arange(0, ROT_DIM, 2, dtype=F32) / ROT_DIM)
    lane_freq = jnp.concatenate(
        [inv_freq, inv_freq, jnp.zeros((HEAD_DIM - ROT_DIM,), F32)])[None, :]
    out = jax.ShapeDtypeStruct((s, HEAD_DIM), F32)
    spec = pl.BlockSpec((rows, HEAD_DIM), lambda i: (i, 0))
    return pl.pallas_call(
        _rope_table_kernel,
        grid=(s // rows,),
        in_specs=[pl.BlockSpec((rows, 1), lambda i: (i, 0)),
                  pl.BlockSpec((1, HEAD_DIM), lambda i: (0, 0))],
        out_specs=[spec, spec, spec],
        out_shape=[out, out, out],
        compiler_params=_params(1),
        name="rope_tables",
    )(positions.reshape(s, 1), lane_freq)


TILES_PER_GROUP = O_WIDTH // TN


def _norm_rope_head(q, hg, cos, sin_lo, sin_hi):
    ms = jnp.mean(q * q, axis=-1, keepdims=True)
    qn = q * lax.rsqrt(ms + EPS) * hg
    hi_to_lo = pltpu.roll(qn, HEAD_DIM - ROT_DIM // 2, axis=1)
    lo_to_hi = pltpu.roll(qn, ROT_DIM // 2, axis=1)
    return qn * cos + hi_to_lo * sin_lo + lo_to_hi * sin_hi


SPLIT_STRIDE = 4


def _finish_tile(y_ref, out_ref, hg_ref, table_refs, y4_ref, *, r):
    rows = TM // r
    two_pass = r > SPLIT_STRIDE
    r_outer = r // SPLIT_STRIDE
    for head in range(TN // HEAD_DIM):
        if hg_ref is not None:
            cos, sin_lo, sin_hi = (t[...] for t in table_refs)
            y_ref[head] = _norm_rope_head(y_ref[head], hg_ref[...], cos, sin_lo, sin_hi)
        if two_pass:
            for p in range(SPLIT_STRIDE):
                y4_ref[head % 2, p] = y_ref[head, pl.ds(p, TM // SPLIT_STRIDE,
                                                        stride=SPLIT_STRIDE), :]
        for res in range(r):
            if r == 1:
                y = y_ref[head]
            elif two_pass:
                p, q = res % SPLIT_STRIDE, res // SPLIT_STRIDE
                y = y4_ref[head % 2, p, pl.ds(q, rows, stride=r_outer), :]
            else:
                y = y_ref[head, pl.ds(res, rows, stride=r), :]
            out_ref[res, :, _lane_slab(head)] = y.astype(BF16)


KINDS_PER_GROUP = 3
GROUP_TILES = KINDS_PER_GROUP * TILES_PER_GROUP
N_QKV_TILES = N_GROUPS * GROUP_TILES


def _qkv_kernel(x_ref, gkv_ref, shkv_ref, sckv_ref, gq_ref, shq_ref, scq_ref,
                wkv_ref, wq_ref, kg_ref, qg_ref, cos_ref, slo_ref, shi_ref,
                *refs):
    outs = refs[:N_GROUPS]
    hkv_ref, hq_ref, ya_ref, yb_ref, y4_ref = refs[N_GROUPS:]
    s = pl.program_id(1)
    t = TILES_PER_GROUP
    y_slots = (ya_ref, yb_ref)
    head_gains = (kg_ref, None, qg_ref)
    tables = (cos_ref, slo_ref, shi_ref)

    @pl.when(s == 0)
    def _():
        _modulate_rows(x_ref, [(gkv_ref, shkv_ref, sckv_ref, hkv_ref),
                               (gq_ref, shq_ref, scq_ref, hq_ref)], rows=TM)

    for step in range(N_QKV_TILES + 1):
        @pl.when(s == step)
        def _(step=step):
            if step < N_QKV_TILES:
                is_q = (step % GROUP_TILES) // t == KINDS_PER_GROUP - 1
                h_ref, w_ref = (hq_ref, wq_ref) if is_q else (hkv_ref, wkv_ref)
                y = jnp.dot(h_ref[...], w_ref[...], preferred_element_type=F32)
                for head in range(TN // HEAD_DIM):
                    y_slots[step % 2][head] = y[:, _lane_slab(head)]
            if step > 0:
                group, rem = divmod(step - 1, GROUP_TILES)
                _finish_tile(y_slots[(step - 1) % 2], outs[group], head_gains[rem // t],
                             tables, y4_ref, r=GROUP_DILATIONS[group])


def _qkv(x, kv_norm_g, kv_shift, kv_scale, q_norm_g, q_shift, q_scale,
         w_kv, w_q, k_head_g, q_head_g, tables):
    s, d = x.shape
    t = TILES_PER_GROUP
    v_col0 = Q_WIDTH // TN
    vec = lambda col: pl.BlockSpec((1, d), lambda i, j: (0, col))
    one = pl.BlockSpec((1, d), lambda i, j: (0, 0))
    tab = pl.BlockSpec((TM, HEAD_DIM), lambda i, j: (i, 0))
    head = pl.BlockSpec((1, HEAD_DIM), lambda i, j: (0, 0))

    def kv_col(j):
        tile = jnp.minimum(j, N_QKV_TILES - 1)
        group, rem = tile // GROUP_TILES, tile % GROUP_TILES
        kv = jnp.minimum(rem, 2 * t - 1)
        return jnp.where(kv < t, group * t + kv, v_col0 + group * t + kv - t)

    def q_col(j):
        tile = jnp.minimum(j, N_QKV_TILES - 1)
        group, rem = tile // GROUP_TILES, tile % GROUP_TILES
        return group * t + jnp.clip(rem - 2 * t, 0, t - 1)

    def out_spec(group):
        r = GROUP_DILATIONS[group]
        first_step = group * GROUP_TILES + 1
        return pl.BlockSpec(
            (r, TM // r, TN),
            lambda i, j: (0, i, jnp.clip(j - first_step, 0, GROUP_TILES - 1)))

    return pl.pallas_call(
        _qkv_kernel,
        grid=(s // TM, N_QKV_TILES + 1),
        in_specs=[
            pl.BlockSpec((TM, d), lambda i, j: (i, 0)),
            one, vec(kv_shift[1]), vec(kv_scale[1]),
            one, vec(q_shift[1]), vec(q_scale[1]),
            pl.BlockSpec((d, TN), lambda i, j: (0, kv_col(j))),
            pl.BlockSpec((d, TN), lambda i, j: (0, q_col(j))),
            head, head, tab, tab, tab,
        ],
        out_specs=[out_spec(g) for g in range(N_GROUPS)],
        out_shape=[jax.ShapeDtypeStruct((r, s // r, KINDS_PER_GROUP * O_WIDTH), BF16)
                   for r in GROUP_DILATIONS],
        scratch_shapes=[
            pltpu.VMEM((TM, d), BF16),
            pltpu.VMEM((TM, d), BF16),
            pltpu.VMEM((TN // V7X_LANES, TM, V7X_LANES), F32),
            pltpu.VMEM((TN // V7X_LANES, TM, V7X_LANES), F32),
            pltpu.VMEM((2, SPLIT_STRIDE, TM // SPLIT_STRIDE, V7X_LANES), F32),
        ],
        compiler_params=_params(2),
        name="qkv",
    )(x, kv_norm_g, kv_shift[0], kv_scale[0], q_norm_g, q_shift[0], q_scale[0],
      w_kv, w_q, k_head_g, q_head_g, *tables)


ATTN_Q_BLOCKS = 2


def _band_attn_kernel(q_ref, kp_ref, kc_ref, vp_ref, vc_ref, o_ref, lse_ref, *, span):
    n = pl.program_id(1)
    qi = lax.broadcasted_iota(jnp.int32, (BLK, 2 * BLK), 0)
    kj = lax.broadcasted_iota(jnp.int32, (BLK, 2 * BLK), 1)
    dist = qi + BLK - kj
    band = (dist >= 0) & (dist <= span)
    first_band = band & ((n > 0) | (kj >= BLK))
    lane = lax.broadcasted_iota(jnp.int32, (BLK, V7X_LANES), 1)
    scale = 1.0 / math.sqrt(HEAD_DIM)
    for blk in range(ATTN_Q_BLOCKS):
        rows = slice(blk * BLK, (blk + 1) * BLK)
        lse_tile = jnp.zeros((BLK, V7X_LANES), F32)
        for h in range(HEADS_PER_GROUP):
            cols = slice(h * HEAD_DIM, (h + 1) * HEAD_DIM)
            q = q_ref[rows, cols]
            if blk == 0:
                k = jnp.concatenate([kp_ref[:, cols], kc_ref[rows, cols]], axis=0)
                v = jnp.concatenate([vp_ref[:, cols], vc_ref[rows, cols]], axis=0)
                mask = first_band
            else:
                keys = slice((blk - 1) * BLK, (blk + 1) * BLK)
                k, v, mask = kc_ref[keys, cols], vc_ref[keys, cols], band
            sc = lax.dot_general(q, k, (((1,), (1,)), ((), ())),
                                 preferred_element_type=F32) * scale
            sc = jnp.where(mask, sc, NEG)
            m = jnp.max(sc, axis=-1, keepdims=True)
            e = jnp.exp(sc - m)
            l = jnp.sum(e, axis=-1, keepdims=True)
            p = (e * (1.0 / l)).astype(BF16)
            o_ref[rows, cols] = jnp.dot(p, v, preferred_element_type=F32)
            lse_tile = jnp.where(lane == h, m + jnp.log(l), lse_tile)
        lse_ref[rows, :] = lse_tile


def _band_attn(kvq, group):
    r, rows, _ = kvq.shape
    k_col, v_col, q_col = range(KINDS_PER_GROUP)

    step_rows = ATTN_Q_BLOCKS * BLK

    def cur(col):
        return pl.BlockSpec((None, step_rows, O_WIDTH), lambda j, n: (j, n, col))

    def prev(col):
        return pl.BlockSpec((None, BLK, O_WIDTH),
                            lambda j, n: (j, jnp.maximum(n * ATTN_Q_BLOCKS - 1, 0), col))

    return pl.pallas_call(
        functools.partial(_band_attn_kernel, span=GROUP_SPANS[group]),
        grid=(r, rows // step_rows),
        in_specs=[cur(q_col), prev(k_col), cur(k_col), prev(v_col), cur(v_col)],
        out_specs=[cur(0),
                   pl.BlockSpec((None, step_rows, V7X_LANES), lambda j, n: (j, n, 0))],
        out_shape=[jax.ShapeDtypeStruct((r, rows, O_WIDTH), F32),
                   jax.ShapeDtypeStruct((r, rows, V7X_LANES), F32)],
        compiler_params=_params(2),
        name=f"band_attn_r{r}",
    )(kvq, kvq, kvq, kvq, kvq)


TM_MIX = 512


def _rows_from_residues(src_ref, dst_ref, *, r, lanes=None):
    n = src_ref.shape[1]
    for res in range(r):
        rows = pl.ds(res, n, stride=r)
        if lanes is None:
            dst_ref[rows, :] = src_ref[res]
        else:
            for slab in range(lanes // V7X_LANES):
                dst_ref[slab, rows, :] = src_ref[res, :, _lane_slab(slab)]


def _mix_wo_kernel(o0_ref, o1_ref, o2_ref, l0_ref, l1_ref, l2_ref, w_ref, x_ref,
                   gate_ref, out_ref, hb_ref, on1_ref, on2_ref, ln1_ref, ln2_ref):
    @pl.when(pl.program_id(1) == 0)
    def _():
        r1, r2 = GROUP_DILATIONS[1], GROUP_DILATIONS[2]
        _rows_from_residues(l1_ref, ln1_ref, r=r1)
        _rows_from_residues(l2_ref, ln2_ref, r=r2)
        _rows_from_residues(o1_ref, on1_ref, r=r1, lanes=O_WIDTH)
        _rows_from_residues(o2_ref, on2_ref, r=r2, lanes=O_WIDTH)
        l0, l1, l2 = l0_ref[0], ln1_ref[...], ln2_ref[...]
        m = jnp.maximum(jnp.maximum(l0, l1), l2)
        e0, e1, e2 = jnp.exp(l0 - m), jnp.exp(l1 - m), jnp.exp(l2 - m)
        inv = 1.0 / (e0 + e1 + e2)
        a0, a1, a2 = e0 * inv, e1 * inv, e2 * inv
        for h in range(HEADS_PER_GROUP):
            cols = _lane_slab(h)
            o = (a0[:, h:h + 1] * o0_ref[0, :, cols] + a1[:, h:h + 1] * on1_ref[h]
                 + a2[:, h:h + 1] * on2_ref[h])
            hb_ref[:, cols] = o.astype(BF16)

    y = jnp.dot(hb_ref[...], w_ref[...], preferred_element_type=F32)
    out_ref[...] = x_ref[...] + gate_ref[...] * y


def _mix_wo(outs, lses, w_o, x, gate):
    s, d = x.shape
    gate_col = gate[1] * (d // TN)

    def planes(width, r):
        return pl.BlockSpec((r, TM_MIX // r, width), lambda i, j: (0, i, 0))

    slabs = pltpu.VMEM((O_WIDTH // V7X_LANES, TM_MIX, V7X_LANES), F32)
    rows = pltpu.VMEM((TM_MIX, V7X_LANES), F32)
    return pl.pallas_call(
        _mix_wo_kernel,
        grid=(s // TM_MIX, d // TN),
        in_specs=[planes(O_WIDTH, r) for r in GROUP_DILATIONS]
        + [planes(V7X_LANES, r) for r in GROUP_DILATIONS]
        + [pl.BlockSpec((O_WIDTH, TN), lambda i, j: (0, j)),
           pl.BlockSpec((TM_MIX, TN), lambda i, j: (i, j)),
           pl.BlockSpec((1, TN), lambda i, j: (0, gate_col + j))],
        out_specs=pl.BlockSpec((TM_MIX, TN), lambda i, j: (i, j)),
        out_shape=jax.ShapeDtypeStruct((s, d), F32),
        scratch_shapes=[pltpu.VMEM((TM_MIX, O_WIDTH), BF16), slabs, slabs, rows, rows],
        compiler_params=_params(2),
        name="mix_wo",
    )(*outs, *lses, w_o, x, gate[0])


def kernel(x, c, positions, mod_w, mod_b, norm_mix_g, norm_ffn_g, conv_pw1_w, conv_pw1_b,
           conv_dw_w, conv_dw_b, conv_ln_g, conv_ln_b, conv_pw2_w, conv_pw2_b, kv_mod_w,
           kv_mod_b, kv_norm_g, w_kv, k_norm_g, w_q, q_norm_g, w_o, ffn_up_w, ffn_dw_w,
           ffn_dw_b, ffn_down_w):
    batch, s, d = x.shape
    assert (batch, s, d) == (1, SEQ, D_MODEL)
    x = x[0]
    c_col = c.reshape(d, 1)
    row = lambda v: v.reshape(1, -1)

    mod = _mod_matvec(c_col, mod_w, mod_b[:, None, :])
    kv_mod = _mod_matvec(c_col, kv_mod_w[None], kv_mod_b[None, None, :])[0]

    mvec = lambda l, q: (mod[l], q)
    ffn_dw_b3 = ffn_dw_b[:, None, :]

    conv_pw1_w, conv_pw2_w, w_kv, w_q, w_o, ffn_up_w, ffn_down_w = (
        w.astype(BF16)
        for w in (conv_pw1_w, conv_pw2_w, w_kv, w_q, w_o, ffn_up_w, ffn_down_w))

    def ffn(x, l):
        return _conv_ffn(x, row(norm_ffn_g[l]), mvec(l, 3), mvec(l, 4), mvec(l, 5), l,
                         ffn_up_w, ffn_dw_w, ffn_dw_b3, ffn_down_w)

    glu = _pw1_glu(x, row(norm_mix_g[0]), mvec(0, 0), mvec(0, 1),
                   conv_pw1_w[0], row(conv_pw1_b[0]))
    x = _conv_pw2(glu, conv_dw_w[0], row(conv_dw_b[0]), row(conv_ln_g[0]),
                  row(conv_ln_b[0]), conv_pw2_w[0], row(conv_pw2_b[0]), x, mvec(0, 2))
    x = ffn(x, 0)

    tables = _rope_tables(positions[0])
    kvq = _qkv(x, row(kv_norm_g), (kv_mod, 0), (kv_mod, 1),
               row(norm_mix_g[1]), mvec(1, 0), mvec(1, 1),
               w_kv, w_q[0], row(k_norm_g), row(q_norm_g[0]), tables)
    outs, lses = zip(*[_band_attn(kvq[g], g) for g in range(N_GROUPS)])
    x = _mix_wo(outs, lses, w_o[0], x, mvec(1, 2))
    x = ffn(x, 1)
    return x[None]
```

```python
import functools
import math

import jax
import jax.numpy as jnp
from jax import lax
from jax.experimental import pallas as pl
from jax.experimental.pallas import tpu as pltpu

D_MODEL = 2048
SEQ = 8192
CONV_K = 31
FFN_CONV_K = 3
D_FF = 5632
GROUP_DILATIONS = (1, 4, 16)
GROUP_SPANS = (128, 128, 128)
N_GROUPS = 3
HEADS_PER_GROUP = 8
HEAD_DIM = 128
Q_WIDTH = N_GROUPS * HEADS_PER_GROUP * HEAD_DIM
O_WIDTH = HEADS_PER_GROUP * HEAD_DIM
ROT_DIM = HEAD_DIM // 4
ROPE_THETA = 500000.0
BLK = 128
EPS = 1e-6
NEG = -1e30

V7X_LANES = 128
V7X_SUBLANES = 8
V7X_BF16_ROWS_PER_VREG = 16
V7X_VMEM_BYTES = 64 * 1024 * 1024
VMEM_LIMIT = 56 * 1024 * 1024

TM = 1024
TN = 512
TN_GLU = TN
ROW_CHUNK = 64

F32 = jnp.float32
BF16 = jnp.bfloat16


def _params(n_axes):
    return pltpu.CompilerParams(
        dimension_semantics=("arbitrary",) * n_axes,
        vmem_limit_bytes=VMEM_LIMIT)


def _sigmoid(x):
    return 1.0 / (1.0 + jnp.exp(-x))


def _silu(x):
    return x * _sigmoid(x)


def _lane_slab(slab):
    return slice(slab * V7X_LANES, (slab + 1) * V7X_LANES)


def _matvec_kernel(c_ref, w_ref, b_ref, o_ref, sb_ref, *, k_dim, tn):
    first = (pl.program_id(0) == 0) & (pl.program_id(1) == 0)

    @pl.when(first)
    def _():
        c = c_ref[...]
        sb_ref[...] = jnp.broadcast_to(_silu(c), (k_dim, V7X_LANES))

    n_groups = tn // V7X_LANES

    def body(t, accs):
        r0 = pl.multiple_of(t * ROW_CHUNK, ROW_CHUNK)
        s = sb_ref[pl.ds(r0, ROW_CHUNK), :]
        w = w_ref[pl.ds(r0, ROW_CHUNK), :]
        new = []
        for g in range(n_groups):
            p = w[:, _lane_slab(g)] * s
            a = accs[g]
            for u in range(ROW_CHUNK // V7X_SUBLANES):
                a = a + p[u * V7X_SUBLANES:(u + 1) * V7X_SUBLANES, :]
            new.append(a)
        return tuple(new)

    init = tuple(jnp.zeros((V7X_SUBLANES, V7X_LANES), F32) for _ in range(n_groups))
    accs = lax.fori_loop(0, k_dim // ROW_CHUNK, body, init)
    row = jnp.concatenate([jnp.sum(a, axis=0, keepdims=True) for a in accs], axis=1)
    o_ref[...] = row + b_ref[...]


def _mod_matvec(c_col, w, b):
    n_l, k_dim, n = w.shape
    tn = 1024
    return pl.pallas_call(
        functools.partial(_matvec_kernel, k_dim=k_dim, tn=tn),
        grid=(n_l, n // tn),
        in_specs=[
            pl.BlockSpec((k_dim, 1), lambda l, j: (0, 0)),
            pl.BlockSpec((None, k_dim, tn), lambda l, j: (l, 0, j)),
            pl.BlockSpec((None, 1, tn), lambda l, j: (l, 0, j)),
        ],
        out_specs=pl.BlockSpec((None, 1, tn), lambda l, j: (l, 0, j)),
        out_shape=jax.ShapeDtypeStruct((n_l, 1, n), F32),
        scratch_shapes=[pltpu.VMEM((k_dim, V7X_LANES), F32)],
        compiler_params=_params(2),
        name="mod_matvec",
    )(c_col, w, b)


MOD_ROWS = V7X_BF16_ROWS_PER_VREG
MOD_UNROLL = 4


def _modulate_rows(x_ref, targets, *, rows, h_row0=0):
    def body(t, carry):
        r0 = pl.multiple_of(t * MOD_ROWS, MOD_ROWS)
        x = x_ref[pl.ds(r0, MOD_ROWS), :]
        ms = jnp.mean(x * x, axis=-1, keepdims=True)
        y = x * lax.rsqrt(ms + EPS)
        h0 = pl.multiple_of(h_row0 + t * MOD_ROWS, MOD_ROWS)
        for g_ref, shift_ref, scale_ref, h_ref in targets:
            h = (y * g_ref[...]) * (1.0 + scale_ref[...]) + shift_ref[...]
            h_ref[pl.ds(h0, MOD_ROWS), :] = h.astype(BF16)
        return carry

    trips = rows // MOD_ROWS
    lax.fori_loop(0, trips, body, 0, unroll=min(MOD_UNROLL, trips))


def _pw1_glu_kernel(x_ref, g_ref, sh_ref, sc_ref, wa_ref, wg_ref, ba_ref, bg_ref,
                    o_ref, h_ref):
    @pl.when(pl.program_id(1) == 0)
    def _():
        _modulate_rows(x_ref, [(g_ref, sh_ref, sc_ref, h_ref)], rows=TM)

    w = jnp.concatenate([wa_ref[...], wg_ref[...]], axis=1)
    u = jnp.dot(h_ref[...], w, preferred_element_type=F32)
    a = u[:, :TN_GLU] + ba_ref[...]
    gt = u[:, TN_GLU:] + bg_ref[...]
    o_ref[...] = a * _sigmoid(gt)


def _pw1_glu(x, norm_g, shift, scale, w, b):
    s, d = x.shape
    nj = d // TN_GLU
    vec = lambda col: pl.BlockSpec((1, d), lambda i, j: (0, col))
    return pl.pallas_call(
        _pw1_glu_kernel,
        grid=(s // TM, nj),
        in_specs=[
            pl.BlockSpec((TM, d), lambda i, j: (i, 0)),
            pl.BlockSpec((1, d), lambda i, j: (0, 0)),
            vec(shift[1]), vec(scale[1]),
            pl.BlockSpec((d, TN_GLU), lambda i, j: (0, j)),
            pl.BlockSpec((d, TN_GLU), lambda i, j: (0, nj + j)),
            pl.BlockSpec((1, TN_GLU), lambda i, j: (0, j)),
            pl.BlockSpec((1, TN_GLU), lambda i, j: (0, nj + j)),
        ],
        out_specs=pl.BlockSpec((TM, TN_GLU), lambda i, j: (i, j)),
        out_shape=jax.ShapeDtypeStruct((s, d), F32),
        scratch_shapes=[pltpu.VMEM((TM, d), BF16)],
        compiler_params=_params(2),
        name="pw1_glu",
    )(x, norm_g, shift[0], scale[0], w, w, b, b)


CONV_HALO = 32
CONV_ROWS = 32


def _dwconv_ln_rows(gbuf_ref, dw_ref, db_ref, lg_ref, lb_ref, tmp_ref, hb_ref, *, rows, d):
    off = CONV_HALO - (CONV_K - 1)

    def body(t, carry):
        r0 = pl.multiple_of(t * CONV_ROWS, CONV_ROWS)
        for slab in range(d // V7X_LANES):
            lanes = _lane_slab(slab)
            acc = jnp.broadcast_to(db_ref[:, lanes], (CONV_ROWS, V7X_LANES))
            for k in range(CONV_K):
                tap = gbuf_ref[slab, pl.ds(r0 + off + k, CONV_ROWS, stride=1), :]
                acc = acc + tap * dw_ref[k:k + 1, lanes]
            tmp_ref[:, lanes] = acc
        u = tmp_ref[...]
        mu = jnp.mean(u, axis=-1, keepdims=True)
        uc = u - mu
        var = jnp.mean(uc * uc, axis=-1, keepdims=True)
        y = uc * lax.rsqrt(var + EPS) * lg_ref[...] + lb_ref[...]
        hb_ref[pl.ds(r0, CONV_ROWS), :] = _silu(y).astype(BF16)
        return carry

    lax.fori_loop(0, rows // CONV_ROWS, body, 0)


def _conv_pw2_kernel(g_ref, halo_ref, dw_ref, db_ref, lg_ref, lb_ref, w_ref, b_ref,
                     x_ref, gate_ref, o_ref, gbuf_ref, tmp_ref, hb_ref, *, d):
    i = pl.program_id(0)

    @pl.when(pl.program_id(1) == 0)
    def _():
        for slab in range(d // V7X_LANES):
            lanes = _lane_slab(slab)
            halo = halo_ref[:, lanes]
            gbuf_ref[slab, 0:CONV_HALO, :] = jnp.where(i == 0, jnp.zeros_like(halo), halo)
            gbuf_ref[slab, CONV_HALO:, :] = g_ref[:, lanes]
        _dwconv_ln_rows(gbuf_ref, dw_ref, db_ref, lg_ref, lb_ref, tmp_ref, hb_ref,
                        rows=TM, d=d)

    y = jnp.dot(hb_ref[...], w_ref[...], preferred_element_type=F32)
    o_ref[...] = x_ref[...] + gate_ref[...] * (y + b_ref[...])


def _conv_pw2(glu, dw_w, dw_b, ln_g, ln_b, w, b, x, gate):
    s, d = glu.shape
    halo_blocks_per_tile = TM // CONV_HALO
    gate_col = gate[1] * (d // TN)
    return pl.pallas_call(
        functools.partial(_conv_pw2_kernel, d=d),
        grid=(s // TM, d // TN),
        in_specs=[
            pl.BlockSpec((TM, d), lambda i, j: (i, 0)),
            pl.BlockSpec((CONV_HALO, d),
                         lambda i, j: (jnp.maximum(i * halo_blocks_per_tile - 1, 0), 0)),
            pl.BlockSpec((CONV_K, d), lambda i, j: (0, 0)),
            pl.BlockSpec((1, d), lambda i, j: (0, 0)),
            pl.BlockSpec((1, d), lambda i, j: (0, 0)),
            pl.BlockSpec((1, d), lambda i, j: (0, 0)),
            pl.BlockSpec((d, TN), lambda i, j: (0, j)),
            pl.BlockSpec((1, TN), lambda i, j: (0, j)),
            pl.BlockSpec((TM, TN), lambda i, j: (i, j)),
            pl.BlockSpec((1, TN), lambda i, j: (0, gate_col + j)),
        ],
        out_specs=pl.BlockSpec((TM, TN), lambda i, j: (i, j)),
        out_shape=jax.ShapeDtypeStruct((s, d), F32),
        scratch_shapes=[
            pltpu.VMEM((d // V7X_LANES, TM + CONV_HALO, V7X_LANES), F32),
            pltpu.VMEM((CONV_ROWS, d), F32),
            pltpu.VMEM((TM, d), BF16),
        ],
        compiler_params=_params(2),
        name="dwconv_pw2",
    )(glu, glu, dw_w, dw_b, ln_g, ln_b, w, b, x, gate[0])


FFN_HALO = V7X_BF16_ROWS_PER_VREG
TF = 512
FFN_OUT_LANES = 512


def _ffn_kernel(x_ref, xh_ref, g_ref, sh_ref, sc_ref, gate_ref, wg_ref, wv_ref,
                dw_ref, db_ref, wd_ref, o_ref, h_ref, u_ref):
    i = pl.program_id(0)

    @pl.when(pl.program_id(1) == 0)
    def _():
        target = [(g_ref, sh_ref, sc_ref, h_ref)]
        _modulate_rows(xh_ref, target, rows=FFN_HALO)
        _modulate_rows(x_ref, target, rows=TM, h_row0=FFN_HALO)
        o_ref[...] = x_ref[...]

    w_up = jnp.concatenate([wg_ref[...], wv_ref[...]], axis=1)
    u = jnp.dot(h_ref[...], w_up, preferred_element_type=F32)
    u_ref[...] = u
    keep_halo = jnp.where(i == 0, 0.0, 1.0)
    u_ref[0:FFN_HALO, 0:TF] = u[0:FFN_HALO, 0:TF] * keep_halo

    gt = db_ref[...] + dw_ref[2:3, :] * u_ref[pl.ds(FFN_HALO, TM), 0:TF]
    gt = gt + dw_ref[1:2, :] * u_ref[pl.ds(FFN_HALO - 1, TM), 0:TF]
    gt = gt + dw_ref[0:1, :] * u_ref[pl.ds(FFN_HALO - 2, TM), 0:TF]
    val = u_ref[pl.ds(FFN_HALO, TM), TF:2 * TF]
    act = (_silu(gt) * val).astype(BF16)
    for c0 in range(0, o_ref.shape[1], FFN_OUT_LANES):
        cols = slice(c0, c0 + FFN_OUT_LANES)
        y = jnp.dot(act, wd_ref[:, cols], preferred_element_type=F32)
        o_ref[:, cols] += gate_ref[:, cols] * y


def _conv_ffn(x, norm_g, shift, scale, gate, layer, up_w, dw_w, dw_b, down_w):
    s, d = x.shape
    f = down_w.shape[1]
    nf = f // TF
    halo_blocks_per_tile = TM // FFN_HALO
    vec = lambda col: pl.BlockSpec((1, d), lambda i, j: (0, col))
    return pl.pallas_call(
        _ffn_kernel,
        grid=(s // TM, nf),
        in_specs=[
            pl.BlockSpec((TM, d), lambda i, j: (i, 0), pipeline_mode=pl.Buffered(1)),
            pl.BlockSpec((FFN_HALO, d),
                         lambda i, j: (jnp.maximum(i * halo_blocks_per_tile - 1, 0), 0)),
            pl.BlockSpec((1, d), lambda i, j: (0, 0)),
            vec(shift[1]), vec(scale[1]), vec(gate[1]),
            pl.BlockSpec((None, d, TF), lambda i, j: (layer, 0, j)),
            pl.BlockSpec((None, d, TF), lambda i, j: (layer, 0, nf + j)),
            pl.BlockSpec((None, FFN_CONV_K, TF), lambda i, j: (layer, 0, j)),
            pl.BlockSpec((None, 1, TF), lambda i, j: (layer, 0, j)),
            pl.BlockSpec((None, TF, d), lambda i, j: (layer, j, 0)),
        ],
        out_specs=pl.BlockSpec((TM, d), lambda i, j: (i, 0)),
        out_shape=jax.ShapeDtypeStruct((s, d), F32),
        scratch_shapes=[
            pltpu.VMEM((TM + FFN_HALO, d), BF16),
            pltpu.VMEM((TM + FFN_HALO, 2 * TF), F32),
        ],
        compiler_params=_params(2),
        name="conv_ffn",
    )(x, x, norm_g, shift[0], scale[0], gate[0], up_w, up_w, dw_w, dw_b, down_w)


def _rope_table_kernel(pos_ref, freq_ref, cos_ref, sin_lo_ref, sin_hi_ref):
    pos = pos_ref[...].astype(F32)
    ang = pos * freq_ref[...]
    lane = lax.broadcasted_iota(jnp.int32, ang.shape, 1)
    c = jnp.cos(ang)
    sn = jnp.sin(ang)
    half = ROT_DIM // 2
    cos_ref[...] = jnp.where(lane < ROT_DIM, c, 1.0)
    sin_lo_ref[...] = jnp.where(lane < half, -sn, 0.0)
    sin_hi_ref[...] = jnp.where((lane >= half) & (lane < ROT_DIM), sn, 0.0)


def _rope_tables(positions):
    s = positions.shape[0]
    rows = 1024
    inv_freq = ROPE_THETA ** (-jnp.arange(0, ROT_DIM, 2, dtype=F32) / ROT_DIM)
    lane_freq = jnp.concatenate(
        [inv_freq, inv_freq, jnp.zeros((HEAD_DIM - ROT_DIM,), F32)])[None, :]
    out = jax.ShapeDtypeStruct((s, HEAD_DIM), F32)
    spec = pl.BlockSpec((rows, HEAD_DIM), lambda i: (i, 0))
    return pl.pallas_call(
        _rope_table_kernel,
        grid=(s // rows,),
        in_specs=[pl.BlockSpec((rows, 1), lambda i: (i, 0)),
                  pl.BlockSpec((1, HEAD_DIM), lambda i: (0, 0))],
        out_specs=[spec, spec, spec],
        out_shape=[out, out, out],
        compiler_params=_params(1),
        name="rope_tables",
    )(positions.reshape(s, 1), lane_freq)


TILES_PER_GROUP = O_WIDTH // TN


def _norm_rope_head(q, hg, cos, sin_lo, sin_hi):
    ms = jnp.mean(q * q, axis=-1, keepdims=True)
    qn = q * lax.rsqrt(ms + EPS) * hg
    hi_to_lo = pltpu.roll(qn, HEAD_DIM - ROT_DIM // 2, axis=1)
    lo_to_hi = pltpu.roll(qn, ROT_DIM // 2, axis=1)
    return qn * cos + hi_to_lo * sin_lo + lo_to_hi * sin_hi


SPLIT_STRIDE = 4


def _finish_tile(y_ref, out_ref, hg_ref, table_refs, y4_ref, *, r):
    rows = TM // r
    two_pass = r > SPLIT_STRIDE
    r_outer = r // SPLIT_STRIDE
    for head in range(TN // HEAD_DIM):
        if hg_ref is not None:
            cos, sin_lo, sin_hi = (t[...] for t in table_refs)
            y_ref[head] = _norm_rope_head(y_ref[head], hg_ref[...], cos, sin_lo, sin_hi)
        if two_pass:
            for p in range(SPLIT_STRIDE):
                y4_ref[head % 2, p] = y_ref[head, pl.ds(p, TM // SPLIT_STRIDE,
                                                        stride=SPLIT_STRIDE), :]
        for res in range(r):
            if r == 1:
                y = y_ref[head]
            elif two_pass:
                p, q = res % SPLIT_STRIDE, res // SPLIT_STRIDE
                y = y4_ref[head % 2, p, pl.ds(q, rows, stride=r_outer), :]
            else:
                y = y_ref[head, pl.ds(res, rows, stride=r), :]
            out_ref[res, :, _lane_slab(head)] = y.astype(BF16)


def _qkv_group_kernel(x_ref, gkv_ref, shkv_ref, sckv_ref, gq_ref, shq_ref, scq_ref,
                      wkv_ref, wq_ref, kg_ref, qg_ref, cos_ref, slo_ref, shi_ref,
                      k_out, v_out, q_out, hkv_ref, hq_ref, ya_ref, yb_ref, y4_ref, *, r):
    s = pl.program_id(1)
    t = TILES_PER_GROUP
    y_slots = (ya_ref, yb_ref)
    outs = (k_out, v_out, q_out)
    head_gains = (kg_ref, None, qg_ref)
    tables = (cos_ref, slo_ref, shi_ref)

    @pl.when(s == 0)
    def _():
        _modulate_rows(x_ref, [(gkv_ref, shkv_ref, sckv_ref, hkv_ref),
                               (gq_ref, shq_ref, scq_ref, hq_ref)], rows=TM)

    for step in range(3 * t + 1):
        @pl.when(s == step)
        def _(step=step):
            if step < 3 * t:
                h_ref, w_ref = (hq_ref, wq_ref) if step // t == 2 else (hkv_ref, wkv_ref)
                y = jnp.dot(h_ref[...], w_ref[...], preferred_element_type=F32)
                for head in range(TN // HEAD_DIM):
                    y_slots[step % 2][head] = y[:, _lane_slab(head)]
            if step > 0:
                kind = (step - 1) // t
                _finish_tile(y_slots[(step - 1) % 2], outs[kind], head_gains[kind], tables,
                             y4_ref, r=r)


def _qkv_group(x, group, kv_norm_g, kv_shift, kv_scale, q_norm_g, q_shift, q_scale,
               w_kv, w_q, k_head_g, q_head_g, tables):
    s, d = x.shape
    r = GROUP_DILATIONS[group]
    t = TILES_PER_GROUP
    k_col0 = group * t
    v_col0 = Q_WIDTH // TN + group * t
    vec = lambda col: pl.BlockSpec((1, d), lambda i, j: (0, col))
    one = pl.BlockSpec((1, d), lambda i, j: (0, 0))
    tab = pl.BlockSpec((TM, HEAD_DIM), lambda i, j: (i, 0))
    head = pl.BlockSpec((1, HEAD_DIM), lambda i, j: (0, 0))

    y4_rows = TM // SPLIT_STRIDE if r > SPLIT_STRIDE else V7X_SUBLANES

    def kv_col(j):
        return jnp.where(j < t, k_col0 + j, v_col0 + jnp.minimum(j - t, t - 1))

    def out_spec(first_step):
        return pl.BlockSpec(
            (r, TM // r, TN),
            lambda i, j: (0, i, jnp.clip(j - first_step, 0, t - 1)))

    out = jax.ShapeDtypeStruct((r, s // r, O_WIDTH), BF16)
    return pl.pallas_call(
        functools.partial(_qkv_group_kernel, r=r),
        grid=(s // TM, 3 * t + 1),
        in_specs=[
            pl.BlockSpec((TM, d), lambda i, j: (i, 0)),
            one, vec(kv_shift[1]), vec(kv_scale[1]),
            one, vec(q_shift[1]), vec(q_scale[1]),
            pl.BlockSpec((d, TN), lambda i, j: (0, kv_col(j))),
            pl.BlockSpec((d, TN), lambda i, j: (0, k_col0 + jnp.clip(j - 2 * t, 0, t - 1))),
            head, head, tab, tab, tab,
        ],
        out_specs=[out_spec(1), out_spec(t + 1), out_spec(2 * t + 1)],
        out_shape=[out, out, out],
        scratch_shapes=[
            pltpu.VMEM((TM, d), BF16),
            pltpu.VMEM((TM, d), BF16),
            pltpu.VMEM((TN // V7X_LANES, TM, V7X_LANES), F32),
            pltpu.VMEM((TN // V7X_LANES, TM, V7X_LANES), F32),
            pltpu.VMEM((2, SPLIT_STRIDE, y4_rows, V7X_LANES), F32),
        ],
        compiler_params=_params(2),
        name=f"qkv_r{r}",
    )(x, kv_norm_g, kv_shift[0], kv_scale[0], q_norm_g, q_shift[0], q_scale[0],
      w_kv, w_q, k_head_g, q_head_g, *tables)


ATTN_Q_BLOCKS = 2


def _band_attn_kernel(q_ref, kp_ref, kc_ref, vp_ref, vc_ref, o_ref, lse_ref, *, span):
    n = pl.program_id(1)
    qi = lax.broadcasted_iota(jnp.int32, (BLK, 2 * BLK), 0)
    kj = lax.broadcasted_iota(jnp.int32, (BLK, 2 * BLK), 1)
    dist = qi + BLK - kj
    band = (dist >= 0) & (dist <= span)
    first_band = band & ((n > 0) | (kj >= BLK))
    lane = lax.broadcasted_iota(jnp.int32, (BLK, V7X_LANES), 1)
    scale = 1.0 / math.sqrt(HEAD_DIM)
    for blk in range(ATTN_Q_BLOCKS):
        rows = slice(blk * BLK, (blk + 1) * BLK)
        lse_tile = jnp.zeros((BLK, V7X_LANES), F32)
        for h in range(HEADS_PER_GROUP):
            cols = slice(h * HEAD_DIM, (h + 1) * HEAD_DIM)
            q = q_ref[rows, cols]
            if blk == 0:
                k = jnp.concatenate([kp_ref[:, cols], kc_ref[rows, cols]], axis=0)
                v = jnp.concatenate([vp_ref[:, cols], vc_ref[rows, cols]], axis=0)
                mask = first_band
            else:
                keys = slice((blk - 1) * BLK, (blk + 1) * BLK)
                k, v, mask = kc_ref[keys, cols], vc_ref[keys, cols], band
            sc = lax.dot_general(q, k, (((1,), (1,)), ((), ())),
                                 preferred_element_type=F32) * scale
            sc = jnp.where(mask, sc, NEG)
            m = jnp.max(sc, axis=-1, keepdims=True)
            e = jnp.exp(sc - m)
            l = jnp.sum(e, axis=-1, keepdims=True)
            p = (e * (1.0 / l)).astype(BF16)
            o_ref[rows, cols] = jnp.dot(p, v, preferred_element_type=F32)
            lse_tile = jnp.where(lane == h, m + jnp.log(l), lse_tile)
        lse_ref[rows, :] = lse_tile


def _band_attn(q, k, v, group):
    r, rows, _ = q.shape
    step_rows = ATTN_Q_BLOCKS * BLK
    cur = pl.BlockSpec((None, step_rows, O_WIDTH), lambda j, n: (j, n, 0))
    prev = pl.BlockSpec((None, BLK, O_WIDTH),
                        lambda j, n: (j, jnp.maximum(n * ATTN_Q_BLOCKS - 1, 0), 0))
    return pl.pallas_call(
        functools.partial(_band_attn_kernel, span=GROUP_SPANS[group]),
        grid=(r, rows // step_rows),
        in_specs=[cur, prev, cur, prev, cur],
        out_specs=[cur, pl.BlockSpec((None, step_rows, V7X_LANES), lambda j, n: (j, n, 0))],
        out_shape=[jax.ShapeDtypeStruct((r, rows, O_WIDTH), F32),
                   jax.ShapeDtypeStruct((r, rows, V7X_LANES), F32)],
        compiler_params=_params(2),
        name=f"band_attn_r{r}",
    )(q, k, k, v, v)


TM_MIX = 512


def _rows_from_residues(src_ref, dst_ref, *, r, lanes=None):
    n = src_ref.shape[1]
    for res in range(r):
        rows = pl.ds(res, n, stride=r)
        if lanes is None:
            dst_ref[rows, :] = src_ref[res]
        else:
            for slab in range(lanes // V7X_LANES):
                dst_ref[slab, rows, :] = src_ref[res, :, _lane_slab(slab)]


def _mix_wo_kernel(o0_ref, o1_ref, o2_ref, l0_ref, l1_ref, l2_ref, w_ref, x_ref,
                   gate_ref, out_ref, hb_ref, on1_ref, on2_ref, ln1_ref, ln2_ref):
    @pl.when(pl.program_id(1) == 0)
    def _():
        r1, r2 = GROUP_DILATIONS[1], GROUP_DILATIONS[2]
        _rows_from_residues(l1_ref, ln1_ref, r=r1)
        _rows_from_residues(l2_ref, ln2_ref, r=r2)
        _rows_from_residues(o1_ref, on1_ref, r=r1, lanes=O_WIDTH)
        _rows_from_residues(o2_ref, on2_ref, r=r2, lanes=O_WIDTH)
        l0, l1, l2 = l0_ref[0], ln1_ref[...], ln2_ref[...]
        m = jnp.maximum(jnp.maximum(l0, l1), l2)
        e0, e1, e2 = jnp.exp(l0 - m), jnp.exp(l1 - m), jnp.exp(l2 - m)
        inv = 1.0 / (e0 + e1 + e2)
        a0, a1, a2 = e0 * inv, e1 * inv, e2 * inv
        for h in range(HEADS_PER_GROUP):
            cols = _lane_slab(h)
            o = (a0[:, h:h + 1] * o0_ref[0, :, cols] + a1[:, h:h + 1] * on1_ref[h]
                 + a2[:, h:h + 1] * on2_ref[h])
            hb_ref[:, cols] = o.astype(BF16)

    y = jnp.dot(hb_ref[...], w_ref[...], preferred_element_type=F32)
    out_ref[...] = x_ref[...] + gate_ref[...] * y


def _mix_wo(outs, lses, w_o, x, gate):
    s, d = x.shape
    gate_col = gate[1] * (d // TN)

    def planes(width, r):
        return pl.BlockSpec((r, TM_MIX // r, width), lambda i, j: (0, i, 0))

    slabs = pltpu.VMEM((O_WIDTH // V7X_LANES, TM_MIX, V7X_LANES), F32)
    rows = pltpu.VMEM((TM_MIX, V7X_LANES), F32)
    return pl.pallas_call(
        _mix_wo_kernel,
        grid=(s // TM_MIX, d // TN),
        in_specs=[planes(O_WIDTH, r) for r in GROUP_DILATIONS]
        + [planes(V7X_LANES, r) for r in GROUP_DILATIONS]
        + [pl.BlockSpec((O_WIDTH, TN), lambda i, j: (0, j)),
           pl.BlockSpec((TM_MIX, TN), lambda i, j: (i, j)),
           pl.BlockSpec((1, TN), lambda i, j: (0, gate_col + j))],
        out_specs=pl.BlockSpec((TM_MIX, TN), lambda i, j: (i, j)),
        out_shape=jax.ShapeDtypeStruct((s, d), F32),
        scratch_shapes=[pltpu.VMEM((TM_MIX, O_WIDTH), BF16), slabs, slabs, rows, rows],
        compiler_params=_params(2),
        name="mix_wo",
    )(*outs, *lses, w_o, x, gate[0])


def kernel(x, c, positions, mod_w, mod_b, norm_mix_g, norm_ffn_g, conv_pw1_w, conv_pw1_b,
           conv_dw_w, conv_dw_b, conv_ln_g, conv_ln_b, conv_pw2_w, conv_pw2_b, kv_mod_w,
           kv_mod_b, kv_norm_g, w_kv, k_norm_g, w_q, q_norm_g, w_o, ffn_up_w, ffn_dw_w,
           ffn_dw_b, ffn_down_w):
    batch, s, d = x.shape
    assert (batch, s, d) == (1, SEQ, D_MODEL)
    x = x[0]
    c_col = c.reshape(d, 1)
    row = lambda v: v.reshape(1, -1)

    mod = _mod_matvec(c_col, mod_w, mod_b[:, None, :])
    kv_mod = _mod_matvec(c_col, kv_mod_w[None], kv_mod_b[None, None, :])[0]

    mvec = lambda l, q: (mod[l], q)
    ffn_dw_b3 = ffn_dw_b[:, None, :]

    conv_pw1_w, conv_pw2_w, w_kv, w_q, w_o, ffn_up_w, ffn_down_w = (
        w.astype(BF16)
        for w in (conv_pw1_w, conv_pw2_w, w_kv, w_q, w_o, ffn_up_w, ffn_down_w))

    def ffn(x, l):
        return _conv_ffn(x, row(norm_ffn_g[l]), mvec(l, 3), mvec(l, 4), mvec(l, 5), l,
                         ffn_up_w, ffn_dw_w, ffn_dw_b3, ffn_down_w)

    glu = _pw1_glu(x, row(norm_mix_g[0]), mvec(0, 0), mvec(0, 1),
                   conv_pw1_w[0], row(conv_pw1_b[0]))
    x = _conv_pw2(glu, conv_dw_w[0], row(conv_dw_b[0]), row(conv_ln_g[0]),
                  row(conv_ln_b[0]), conv_pw2_w[0], row(conv_pw2_b[0]), x, mvec(0, 2))
    x = ffn(x, 0)

    tables = _rope_tables(positions[0])
    outs, lses = [], []
    for g in range(N_GROUPS):
        k, v, q = _qkv_group(x, g, row(kv_norm_g), (kv_mod, 0), (kv_mod, 1),
                             row(norm_mix_g[1]), mvec(1, 0), mvec(1, 1),
                             w_kv, w_q[0], row(k_norm_g), row(q_norm_g[0]), tables)
        o, lse = _band_attn(q, k, v, g)
        outs.append(o)
        lses.append(lse)
    x = _mix_wo(outs, lses, w_o[0], x, mvec(1, 2))
    x = ffn(x, 1)
    return x[None]
```

```python
import functools
import math

import jax
import jax.numpy as jnp
from jax import lax
from jax.experimental import pallas as pl
from jax.experimental.pallas import tpu as pltpu

D_MODEL = 2048
SEQ = 8192
CONV_K = 31
FFN_CONV_K = 3
D_FF = 5632
GROUP_DILATIONS = (1, 4, 16)
GROUP_SPANS = (128, 128, 128)
N_GROUPS = 3
HEADS_PER_GROUP = 8
HEAD_DIM = 128
Q_WIDTH = N_GROUPS * HEADS_PER_GROUP * HEAD_DIM
O_WIDTH = HEADS_PER_GROUP * HEAD_DIM
ROT_DIM = HEAD_DIM // 4
ROPE_THETA = 500000.0
BLK = 128
EPS = 1e-6
NEG = -1e30

V7X_LANES = 128
V7X_SUBLANES = 8
V7X_BF16_ROWS_PER_VREG = 16
V7X_VMEM_BYTES = 64 * 1024 * 1024
VMEM_LIMIT = 56 * 1024 * 1024

TM = 1024
TN = 512
TN_GLU = TN
ROW_CHUNK = 64

F32 = jnp.float32
BF16 = jnp.bfloat16


def _params(n_axes):
    return pltpu.CompilerParams(
        dimension_semantics=("arbitrary",) * n_axes,
        vmem_limit_bytes=VMEM_LIMIT)


def _sigmoid(x):
    return 1.0 / (1.0 + jnp.exp(-x))


def _silu(x):
    return x * _sigmoid(x)


def _lane_slab(slab):
    return slice(slab * V7X_LANES, (slab + 1) * V7X_LANES)


def _matvec_kernel(c_ref, w_ref, b_ref, o_ref, sb_ref, *, k_dim, tn):
    first = (pl.program_id(0) == 0) & (pl.program_id(1) == 0)

    @pl.when(first)
    def _():
        c = c_ref[...]
        sb_ref[...] = jnp.broadcast_to(_silu(c), (k_dim, V7X_LANES))

    n_groups = tn // V7X_LANES

    def body(t, accs):
        r0 = pl.multiple_of(t * ROW_CHUNK, ROW_CHUNK)
        s = sb_ref[pl.ds(r0, ROW_CHUNK), :]
        w = w_ref[pl.ds(r0, ROW_CHUNK), :]
        new = []
        for g in range(n_groups):
            p = w[:, _lane_slab(g)] * s
            a = accs[g]
            for u in range(ROW_CHUNK // V7X_SUBLANES):
                a = a + p[u * V7X_SUBLANES:(u + 1) * V7X_SUBLANES, :]
            new.append(a)
        return tuple(new)

    init = tuple(jnp.zeros((V7X_SUBLANES, V7X_LANES), F32) for _ in range(n_groups))
    accs = lax.fori_loop(0, k_dim // ROW_CHUNK, body, init)
    row = jnp.concatenate([jnp.sum(a, axis=0, keepdims=True) for a in accs], axis=1)
    o_ref[...] = row + b_ref[...]


def _mod_matvec(c_col, w, b):
    n_l, k_dim, n = w.shape
    tn = 1024
    return pl.pallas_call(
        functools.partial(_matvec_kernel, k_dim=k_dim, tn=tn),
        grid=(n_l, n // tn),
        in_specs=[
            pl.BlockSpec((k_dim, 1), lambda l, j: (0, 0)),
            pl.BlockSpec((None, k_dim, tn), lambda l, j: (l, 0, j)),
            pl.BlockSpec((None, 1, tn), lambda l, j: (l, 0, j)),
        ],
        out_specs=pl.BlockSpec((None, 1, tn), lambda l, j: (l, 0, j)),
        out_shape=jax.ShapeDtypeStruct((n_l, 1, n), F32),
        scratch_shapes=[pltpu.VMEM((k_dim, V7X_LANES), F32)],
        compiler_params=_params(2),
        name="mod_matvec",
    )(c_col, w, b)


MOD_ROWS = V7X_BF16_ROWS_PER_VREG
MOD_UNROLL = 4


def _modulate_rows(x_ref, targets, *, rows, h_row0=0):
    def body(t, carry):
        r0 = pl.multiple_of(t * MOD_ROWS, MOD_ROWS)
        x = x_ref[pl.ds(r0, MOD_ROWS), :]
        ms = jnp.mean(x * x, axis=-1, keepdims=True)
        y = x * lax.rsqrt(ms + EPS)
        h0 = pl.multiple_of(h_row0 + t * MOD_ROWS, MOD_ROWS)
        for g_ref, shift_ref, scale_ref, h_ref in targets:
            h = (y * g_ref[...]) * (1.0 + scale_ref[...]) + shift_ref[...]
            h_ref[pl.ds(h0, MOD_ROWS), :] = h.astype(BF16)
        return carry

    trips = rows // MOD_ROWS
    lax.fori_loop(0, trips, body, 0, unroll=min(MOD_UNROLL, trips))


def _pw1_glu_kernel(x_ref, g_ref, sh_ref, sc_ref, wa_ref, wg_ref, ba_ref, bg_ref,
                    o_ref, h_ref):
    @pl.when(pl.program_id(1) == 0)
    def _():
        _modulate_rows(x_ref, [(g_ref, sh_ref, sc_ref, h_ref)], rows=TM)

    w = jnp.concatenate([wa_ref[...], wg_ref[...]], axis=1)
    u = jnp.dot(h_ref[...], w, preferred_element_type=F32)
    a = u[:, :TN_GLU] + ba_ref[...]
    gt = u[:, TN_GLU:] + bg_ref[...]
    o_ref[...] = a * _sigmoid(gt)


def _pw1_glu(x, norm_g, shift, scale, w, b):
    s, d = x.shape
    nj = d // TN_GLU
    vec = lambda col: pl.BlockSpec((1, d), lambda i, j: (0, col))
    return pl.pallas_call(
        _pw1_glu_kernel,
        grid=(s // TM, nj),
        in_specs=[
            pl.BlockSpec((TM, d), lambda i, j: (i, 0)),
            pl.BlockSpec((1, d), lambda i, j: (0, 0)),
            vec(shift[1]), vec(scale[1]),
            pl.BlockSpec((d, TN_GLU), lambda i, j: (0, j)),
            pl.BlockSpec((d, TN_GLU), lambda i, j: (0, nj + j)),
            pl.BlockSpec((1, TN_GLU), lambda i, j: (0, j)),
            pl.BlockSpec((1, TN_GLU), lambda i, j: (0, nj + j)),
        ],
        out_specs=pl.BlockSpec((TM, TN_GLU), lambda i, j: (i, j)),
        out_shape=jax.ShapeDtypeStruct((s, d), F32),
        scratch_shapes=[pltpu.VMEM((TM, d), BF16)],
        compiler_params=_params(2),
        name="pw1_glu",
    )(x, norm_g, shift[0], scale[0], w, w, b, b)


CONV_HALO = 32
CONV_ROWS = 32


def _dwconv_ln_rows(gbuf_ref, dw_ref, db_ref, lg_ref, lb_ref, tmp_ref, hb_ref, *, rows, d):
    off = CONV_HALO - (CONV_K - 1)

    def body(t, carry):
        r0 = pl.multiple_of(t * CONV_ROWS, CONV_ROWS)
        for slab in range(d // V7X_LANES):
            lanes = _lane_slab(slab)
            acc = jnp.broadcast_to(db_ref[:, lanes], (CONV_ROWS, V7X_LANES))
            for k in range(CONV_K):
                tap = gbuf_ref[slab, pl.ds(r0 + off + k, CONV_ROWS, stride=1), :]
                acc = acc + tap * dw_ref[k:k + 1, lanes]
            tmp_ref[:, lanes] = acc
        u = tmp_ref[...]
        mu = jnp.mean(u, axis=-1, keepdims=True)
        uc = u - mu
        var = jnp.mean(uc * uc, axis=-1, keepdims=True)
        y = uc * lax.rsqrt(var + EPS) * lg_ref[...] + lb_ref[...]
        hb_ref[pl.ds(r0, CONV_ROWS), :] = _silu(y).astype(BF16)
        return carry

    lax.fori_loop(0, rows // CONV_ROWS, body, 0)


def _conv_pw2_kernel(g_ref, halo_ref, dw_ref, db_ref, lg_ref, lb_ref, w_ref, b_ref,
                     x_ref, gate_ref, o_ref, gbuf_ref, tmp_ref, hb_ref, *, d):
    i = pl.program_id(0)

    @pl.when(pl.program_id(1) == 0)
    def _():
        for slab in range(d // V7X_LANES):
            lanes = _lane_slab(slab)
            halo = halo_ref[:, lanes]
            gbuf_ref[slab, 0:CONV_HALO, :] = jnp.where(i == 0, jnp.zeros_like(halo), halo)
            gbuf_ref[slab, CONV_HALO:, :] = g_ref[:, lanes]
        _dwconv_ln_rows(gbuf_ref, dw_ref, db_ref, lg_ref, lb_ref, tmp_ref, hb_ref,
                        rows=TM, d=d)

    y = jnp.dot(hb_ref[...], w_ref[...], preferred_element_type=F32)
    o_ref[...] = x_ref[...] + gate_ref[...] * (y + b_ref[...])


def _conv_pw2(glu, dw_w, dw_b, ln_g, ln_b, w, b, x, gate):
    s, d = glu.shape
    halo_blocks_per_tile = TM // CONV_HALO
    gate_col = gate[1] * (d // TN)
    return pl.pallas_call(
        functools.partial(_conv_pw2_kernel, d=d),
        grid=(s // TM, d // TN),
        in_specs=[
            pl.BlockSpec((TM, d), lambda i, j: (i, 0)),
            pl.BlockSpec((CONV_HALO, d),
                         lambda i, j: (jnp.maximum(i * halo_blocks_per_tile - 1, 0), 0)),
            pl.BlockSpec((CONV_K, d), lambda i, j: (0, 0)),
            pl.BlockSpec((1, d), lambda i, j: (0, 0)),
            pl.BlockSpec((1, d), lambda i, j: (0, 0)),
            pl.BlockSpec((1, d), lambda i, j: (0, 0)),
            pl.BlockSpec((d, TN), lambda i, j: (0, j)),
            pl.BlockSpec((1, TN), lambda i, j: (0, j)),
            pl.BlockSpec((TM, TN), lambda i, j: (i, j)),
            pl.BlockSpec((1, TN), lambda i, j: (0, gate_col + j)),
        ],
        out_specs=pl.BlockSpec((TM, TN), lambda i, j: (i, j)),
        out_shape=jax.ShapeDtypeStruct((s, d), F32),
        scratch_shapes=[
            pltpu.VMEM((d // V7X_LANES, TM + CONV_HALO, V7X_LANES), F32),
            pltpu.VMEM((CONV_ROWS, d), F32),
            pltpu.VMEM((TM, d), BF16),
        ],
        compiler_params=_params(2),
        name="dwconv_pw2",
    )(glu, glu, dw_w, dw_b, ln_g, ln_b, w, b, x, gate[0])


FFN_HALO = V7X_BF16_ROWS_PER_VREG
TF = 512
FFN_OUT_LANES = 512


def _ffn_kernel(x_ref, xh_ref, g_ref, sh_ref, sc_ref, gate_ref, wg_ref, wv_ref,
                dw_ref, db_ref, wd_ref, o_ref, h_ref, u_ref):
    i = pl.program_id(0)

    @pl.when(pl.program_id(1) == 0)
    def _():
        target = [(g_ref, sh_ref, sc_ref, h_ref)]
        _modulate_rows(xh_ref, target, rows=FFN_HALO)
        _modulate_rows(x_ref, target, rows=TM, h_row0=FFN_HALO)
        o_ref[...] = x_ref[...]

    w_up = jnp.concatenate([wg_ref[...], wv_ref[...]], axis=1)
    u = jnp.dot(h_ref[...], w_up, preferred_element_type=F32)
    u_ref[...] = u
    keep_halo = jnp.where(i == 0, 0.0, 1.0)
    u_ref[0:FFN_HALO, 0:TF] = u[0:FFN_HALO, 0:TF] * keep_halo

    gt = db_ref[...] + dw_ref[2:3, :] * u_ref[pl.ds(FFN_HALO, TM), 0:TF]
    gt = gt + dw_ref[1:2, :] * u_ref[pl.ds(FFN_HALO - 1, TM), 0:TF]
    gt = gt + dw_ref[0:1, :] * u_ref[pl.ds(FFN_HALO - 2, TM), 0:TF]
    val = u_ref[pl.ds(FFN_HALO, TM), TF:2 * TF]
    act = (_silu(gt) * val).astype(BF16)
    for c0 in range(0, o_ref.shape[1], FFN_OUT_LANES):
        cols = slice(c0, c0 + FFN_OUT_LANES)
        y = jnp.dot(act, wd_ref[:, cols], preferred_element_type=F32)
        o_ref[:, cols] += gate_ref[:, cols] * y


def _conv_ffn(x, norm_g, shift, scale, gate, layer, up_w, dw_w, dw_b, down_w):
    s, d = x.shape
    f = down_w.shape[1]
    nf = f // TF
    halo_blocks_per_tile = TM // FFN_HALO
    vec = lambda col: pl.BlockSpec((1, d), lambda i, j: (0, col))
    return pl.pallas_call(
        _ffn_kernel,
        grid=(s // TM, nf),
        in_specs=[
            pl.BlockSpec((TM, d), lambda i, j: (i, 0), pipeline_mode=pl.Buffered(1)),
            pl.BlockSpec((FFN_HALO, d),
                         lambda i, j: (jnp.maximum(i * halo_blocks_per_tile - 1, 0), 0)),
            pl.BlockSpec((1, d), lambda i, j: (0, 0)),
            vec(shift[1]), vec(scale[1]), vec(gate[1]),
            pl.BlockSpec((None, d, TF), lambda i, j: (layer, 0, j)),
            pl.BlockSpec((None, d, TF), lambda i, j: (layer, 0, nf + j)),
            pl.BlockSpec((None, FFN_CONV_K, TF), lambda i, j: (layer, 0, j)),
            pl.BlockSpec((None, 1, TF), lambda i, j: (layer, 0, j)),
            pl.BlockSpec((None, TF, d), lambda i, j: (layer, j, 0)),
        ],
        out_specs=pl.BlockSpec((TM, d), lambda i, j: (i, 0)),
        out_shape=jax.ShapeDtypeStruct((s, d), F32),
        scratch_shapes=[
            pltpu.VMEM((TM + FFN_HALO, d), BF16),
            pltpu.VMEM((TM + FFN_HALO, 2 * TF), F32),
        ],
        compiler_params=_params(2),
        name="conv_ffn",
    )(x, x, norm_g, shift[0], scale[0], gate[0], up_w, up_w, dw_w, dw_b, down_w)


def _rope_table_kernel(pos_ref, freq_ref, cos_ref, sin_lo_ref, sin_hi_ref):
    pos = pos_ref[...].astype(F32)
    ang = pos * freq_ref[...]
    lane = lax.broadcasted_iota(jnp.int32, ang.shape, 1)
    c = jnp.cos(ang)
    sn = jnp.sin(ang)
    half = ROT_DIM // 2
    cos_ref[...] = jnp.where(lane < ROT_DIM, c, 1.0)
    sin_lo_ref[...] = jnp.where(lane < half, -sn, 0.0)
    sin_hi_ref[...] = jnp.where((lane >= half) & (lane < ROT_DIM), sn, 0.0)


def _rope_tables(positions):
    s = positions.shape[0]
    rows = 1024
    inv_freq = ROPE_THETA ** (-jnp.arange(0, ROT_DIM, 2, dtype=F32) / ROT_DIM)
    lane_freq = jnp.concatenate(
        [inv_freq, inv_freq, jnp.zeros((HEAD_DIM - ROT_DIM,), F32)])[None, :]
    out = jax.ShapeDtypeStruct((s, HEAD_DIM), F32)
    spec = pl.BlockSpec((rows, HEAD_DIM), lambda i: (i, 0))
    return pl.pallas_call(
        _rope_table_kernel,
        grid=(s // rows,),
        in_specs=[pl.BlockSpec((rows, 1), lambda i: (i, 0)),
                  pl.BlockSpec((1, HEAD_DIM), lambda i: (0, 0))],
        out_specs=[spec, spec, spec],
        out_shape=[out, out, out],
        compiler_params=_params(1),
        name="rope_tables",
    )(positions.reshape(s, 1), lane_freq)


TILES_PER_GROUP = O_WIDTH // TN


def _norm_rope_head(q, hg, cos, sin_lo, sin_hi):
    ms = jnp.mean(q * q, axis=-1, keepdims=True)
    qn = q * lax.rsqrt(ms + EPS) * hg
    hi_to_lo = pltpu.roll(qn, HEAD_DIM - ROT_DIM // 2, axis=1)
    lo_to_hi = pltpu.roll(qn, ROT_DIM // 2, axis=1)
    return qn * cos + hi_to_lo * sin_lo + lo_to_hi * sin_hi


SPLIT_STRIDE = 4


def _finish_tile(y_ref, out_ref, head_gain, use_rope, table_refs, y4_ref, *, r):
    rows = TM // r
    two_pass = r > SPLIT_STRIDE
    r_outer = r // SPLIT_STRIDE
    for head in range(TN // HEAD_DIM):
        cos, sin_lo, sin_hi = (t[...] for t in table_refs)
        y = y_ref[head]
        y_ref[head] = jnp.where(use_rope,
                                _norm_rope_head(y, head_gain, cos, sin_lo, sin_hi), y)
        if two_pass:
            for p in range(SPLIT_STRIDE):
                y4_ref[head % 2, p] = y_ref[head, pl.ds(p, TM // SPLIT_STRIDE,
                                                        stride=SPLIT_STRIDE), :]
        for res in range(r):
            if r == 1:
                y = y_ref[head]
            elif two_pass:
                p, q = res % SPLIT_STRIDE, res // SPLIT_STRIDE
                y = y4_ref[head % 2, p, pl.ds(q, rows, stride=r_outer), :]
            else:
                y = y_ref[head, pl.ds(res, rows, stride=r), :]
            out_ref[res, :, _lane_slab(head)] = y.astype(BF16)


N_KINDS = 3
K_KIND, V_KIND, Q_KIND = range(N_KINDS)
GROUP_TILES = N_KINDS * TILES_PER_GROUP
ROW_TILES = N_GROUPS * GROUP_TILES


def _qkv_kernel(x_ref, gkv_ref, shkv_ref, sckv_ref, gq_ref, shq_ref, scq_ref, w_ref,
                hg_ref, cos_ref, slo_ref, shi_ref, *refs):
    outs = refs[:N_GROUPS]
    h_ref, ya_ref, yb_ref, y4_ref = refs[N_GROUPS:]
    s = pl.program_id(0)
    t = TILES_PER_GROUP
    tile = jnp.minimum(s, pl.num_programs(0) - 2)
    col = tile % ROW_TILES
    done_col = jnp.maximum(s - 1, 0) % ROW_TILES
    done_kind = (done_col % GROUP_TILES) // t
    done_group = done_col // GROUP_TILES
    tables = (cos_ref, slo_ref, shi_ref)

    @pl.when(col == 0)
    def _():
        _modulate_rows(x_ref, [(gkv_ref, shkv_ref, sckv_ref, h_ref.at[0]),
                               (gq_ref, shq_ref, scq_ref, h_ref.at[1])], rows=TM)

    @pl.when(s == 0)
    def _():
        ya_ref[...] = jnp.zeros(ya_ref.shape, F32)

    h_sel = ((col % GROUP_TILES) // t == Q_KIND).astype(jnp.int32)
    gain = hg_ref[(done_kind == Q_KIND).astype(jnp.int32)]
    use_rope = done_kind != V_KIND
    for group, r in enumerate(GROUP_DILATIONS):
        @pl.when(done_group == group)
        def _(group=group, r=r):
            yb_ref[...] = ya_ref[...]
            y = jnp.dot(h_ref[h_sel], w_ref[...], preferred_element_type=F32)
            for head in range(TN // HEAD_DIM):
                ya_ref[head] = y[:, _lane_slab(head)]
            _finish_tile(yb_ref, outs[group], gain, use_rope, tables, y4_ref, r=r)


def _qkv(x, kv_norm_g, kv_shift, kv_scale, q_norm_g, q_shift, q_scale, w_all, head_gains,
         tables):
    s, d = x.shape
    t = TILES_PER_GROUP
    n_tiles = (s // TM) * ROW_TILES
    vec = lambda col: pl.BlockSpec((1, d), lambda j: (0, col))
    one = pl.BlockSpec((1, d), lambda j: (0, 0))

    def tile_of(j):
        return jnp.minimum(j, n_tiles - 1)

    def done_of(j):
        return jnp.maximum(j - 1, 0)

    def w_col(j):
        col = tile_of(j) % ROW_TILES
        group, rem = col // GROUP_TILES, col % GROUP_TILES
        return (rem // t) * (N_GROUPS * t) + group * t + rem % t

    def out_spec(group):
        r = GROUP_DILATIONS[group]
        return pl.BlockSpec(
            (r, TM // r, TN),
            lambda j: (0, done_of(j) // ROW_TILES,
                       jnp.clip(done_of(j) % ROW_TILES - group * GROUP_TILES,
                                0, GROUP_TILES - 1)))

    tab = pl.BlockSpec((TM, HEAD_DIM), lambda j: (done_of(j) // ROW_TILES, 0))
    return pl.pallas_call(
        _qkv_kernel,
        grid=(n_tiles + 1,),
        in_specs=[
            pl.BlockSpec((TM, d), lambda j: (tile_of(j) // ROW_TILES, 0)),
            one, vec(kv_shift[1]), vec(kv_scale[1]),
            one, vec(q_shift[1]), vec(q_scale[1]),
            pl.BlockSpec((d, TN), lambda j: (0, w_col(j))),
            pl.BlockSpec((2, 1, HEAD_DIM), lambda j: (0, 0, 0)),
            tab, tab, tab,
        ],
        out_specs=[out_spec(g) for g in range(N_GROUPS)],
        out_shape=[jax.ShapeDtypeStruct((r, s // r, N_KINDS * O_WIDTH), BF16)
                   for r in GROUP_DILATIONS],
        scratch_shapes=[
            pltpu.VMEM((2, TM, d), BF16),
            pltpu.VMEM((TN // V7X_LANES, TM, V7X_LANES), F32),
            pltpu.VMEM((TN // V7X_LANES, TM, V7X_LANES), F32),
            pltpu.VMEM((2, SPLIT_STRIDE, TM // SPLIT_STRIDE, V7X_LANES), F32),
        ],
        compiler_params=_params(1),
        name="qkv",
    )(x, kv_norm_g, kv_shift[0], kv_scale[0], q_norm_g, q_shift[0], q_scale[0],
      w_all, head_gains, *tables)


ATTN_Q_BLOCKS = 2


def _band_attn_kernel(q_ref, kp_ref, kc_ref, vp_ref, vc_ref, o_ref, lse_ref, *, span):
    n = pl.program_id(1)
    qi = lax.broadcasted_iota(jnp.int32, (BLK, 2 * BLK), 0)
    kj = lax.broadcasted_iota(jnp.int32, (BLK, 2 * BLK), 1)
    dist = qi + BLK - kj
    band = (dist >= 0) & (dist <= span)
    first_band = band & ((n > 0) | (kj >= BLK))
    lane = lax.broadcasted_iota(jnp.int32, (BLK, V7X_LANES), 1)
    scale = 1.0 / math.sqrt(HEAD_DIM)
    for blk in range(ATTN_Q_BLOCKS):
        rows = slice(blk * BLK, (blk + 1) * BLK)
        lse_tile = jnp.zeros((BLK, V7X_LANES), F32)
        for h in range(HEADS_PER_GROUP):
            cols = slice(h * HEAD_DIM, (h + 1) * HEAD_DIM)
            q = q_ref[rows, cols]
            if blk == 0:
                k = jnp.concatenate([kp_ref[:, cols], kc_ref[rows, cols]], axis=0)
                v = jnp.concatenate([vp_ref[:, cols], vc_ref[rows, cols]], axis=0)
                mask = first_band
            else:
                keys = slice((blk - 1) * BLK, (blk + 1) * BLK)
                k, v, mask = kc_ref[keys, cols], vc_ref[keys, cols], band
            sc = lax.dot_general(q, k, (((1,), (1,)), ((), ())),
                                 preferred_element_type=F32) * scale
            sc = jnp.where(mask, sc, NEG)
            m = jnp.max(sc, axis=-1, keepdims=True)
            e = jnp.exp(sc - m)
            l = jnp.sum(e, axis=-1, keepdims=True)
            p = (e * (1.0 / l)).astype(BF16)
            o_ref[rows, cols] = jnp.dot(p, v, preferred_element_type=F32)
            lse_tile = jnp.where(lane == h, m + jnp.log(l), lse_tile)
        lse_ref[rows, :] = lse_tile


def _band_attn(kvq, group):
    r, rows, _ = kvq.shape
    step_rows = ATTN_Q_BLOCKS * BLK

    def cur(col):
        return pl.BlockSpec((None, step_rows, O_WIDTH), lambda j, n: (j, n, col))

    def prev(col):
        return pl.BlockSpec((None, BLK, O_WIDTH),
                            lambda j, n: (j, jnp.maximum(n * ATTN_Q_BLOCKS - 1, 0), col))

    return pl.pallas_call(
        functools.partial(_band_attn_kernel, span=GROUP_SPANS[group]),
        grid=(r, rows // step_rows),
        in_specs=[cur(Q_KIND), prev(K_KIND), cur(K_KIND), prev(V_KIND), cur(V_KIND)],
        out_specs=[cur(0),
                   pl.BlockSpec((None, step_rows, V7X_LANES), lambda j, n: (j, n, 0))],
        out_shape=[jax.ShapeDtypeStruct((r, rows, O_WIDTH), F32),
                   jax.ShapeDtypeStruct((r, rows, V7X_LANES), F32)],
        compiler_params=_params(2),
        name=f"band_attn_r{r}",
    )(kvq, kvq, kvq, kvq, kvq)


TM_MIX = 512


def _rows_from_residues(src_ref, dst_ref, *, r, lanes=None):
    n = src_ref.shape[1]
    for res in range(r):
        rows = pl.ds(res, n, stride=r)
        if lanes is None:
            dst_ref[rows, :] = src_ref[res]
        else:
            for slab in range(lanes // V7X_LANES):
                dst_ref[slab, rows, :] = src_ref[res, :, _lane_slab(slab)]


def _mix_wo_kernel(o0_ref, o1_ref, o2_ref, l0_ref, l1_ref, l2_ref, w_ref, x_ref,
                   gate_ref, out_ref, hb_ref, on1_ref, on2_ref, ln1_ref, ln2_ref):
    @pl.when(pl.program_id(1) == 0)
    def _():
        r1, r2 = GROUP_DILATIONS[1], GROUP_DILATIONS[2]
        _rows_from_residues(l1_ref, ln1_ref, r=r1)
        _rows_from_residues(l2_ref, ln2_ref, r=r2)
        _rows_from_residues(o1_ref, on1_ref, r=r1, lanes=O_WIDTH)
        _rows_from_residues(o2_ref, on2_ref, r=r2, lanes=O_WIDTH)
        l0, l1, l2 = l0_ref[0], ln1_ref[...], ln2_ref[...]
        m = jnp.maximum(jnp.maximum(l0, l1), l2)
        e0, e1, e2 = jnp.exp(l0 - m), jnp.exp(l1 - m), jnp.exp(l2 - m)
        inv = 1.0 / (e0 + e1 + e2)
        a0, a1, a2 = e0 * inv, e1 * inv, e2 * inv
        for h in range(HEADS_PER_GROUP):
            cols = _lane_slab(h)
            o = (a0[:, h:h + 1] * o0_ref[0, :, cols] + a1[:, h:h + 1] * on1_ref[h]
                 + a2[:, h:h + 1] * on2_ref[h])
            hb_ref[:, cols] = o.astype(BF16)

    y = jnp.dot(hb_ref[...], w_ref[...], preferred_element_type=F32)
    out_ref[...] = x_ref[...] + gate_ref[...] * y


def _mix_wo(outs, lses, w_o, x, gate):
    s, d = x.shape
    gate_col = gate[1] * (d // TN)

    def planes(width, r):
        return pl.BlockSpec((r, TM_MIX // r, width), lambda i, j: (0, i, 0))

    slabs = pltpu.VMEM((O_WIDTH // V7X_LANES, TM_MIX, V7X_LANES), F32)
    rows = pltpu.VMEM((TM_MIX, V7X_LANES), F32)
    return pl.pallas_call(
        _mix_wo_kernel,
        grid=(s // TM_MIX, d // TN),
        in_specs=[planes(O_WIDTH, r) for r in GROUP_DILATIONS]
        + [planes(V7X_LANES, r) for r in GROUP_DILATIONS]
        + [pl.BlockSpec((O_WIDTH, TN), lambda i, j: (0, j)),
           pl.BlockSpec((TM_MIX, TN), lambda i, j: (i, j)),
           pl.BlockSpec((1, TN), lambda i, j: (0, gate_col + j))],
        out_specs=pl.BlockSpec((TM_MIX, TN), lambda i, j: (i, j)),
        out_shape=jax.ShapeDtypeStruct((s, d), F32),
        scratch_shapes=[pltpu.VMEM((TM_MIX, O_WIDTH), BF16), slabs, slabs, rows, rows],
        compiler_params=_params(2),
        name="mix_wo",
    )(*outs, *lses, w_o, x, gate[0])


def kernel(x, c, positions, mod_w, mod_b, norm_mix_g, norm_ffn_g, conv_pw1_w, conv_pw1_b,
           conv_dw_w, conv_dw_b, conv_ln_g, conv_ln_b, conv_pw2_w, conv_pw2_b, kv_mod_w,
           kv_mod_b, kv_norm_g, w_kv, k_norm_g, w_q, q_norm_g, w_o, ffn_up_w, ffn_dw_w,
           ffn_dw_b, ffn_down_w):
    batch, s, d = x.shape
    assert (batch, s, d) == (1, SEQ, D_MODEL)
    x = x[0]
    c_col = c.reshape(d, 1)
    row = lambda v: v.reshape(1, -1)

    mod = _mod_matvec(c_col, mod_w, mod_b[:, None, :])
    kv_mod = _mod_matvec(c_col, kv_mod_w[None], kv_mod_b[None, None, :])[0]

    mvec = lambda l, q: (mod[l], q)
    ffn_dw_b3 = ffn_dw_b[:, None, :]

    conv_pw1_w, conv_pw2_w, w_o, ffn_up_w, ffn_down_w = (
        w.astype(BF16) for w in (conv_pw1_w, conv_pw2_w, w_o, ffn_up_w, ffn_down_w))
    w_qkv = jnp.concatenate([w_kv.astype(BF16), w_q[0].astype(BF16)], axis=1)

    def ffn(x, l):
        return _conv_ffn(x, row(norm_ffn_g[l]), mvec(l, 3), mvec(l, 4), mvec(l, 5), l,
                         ffn_up_w, ffn_dw_w, ffn_dw_b3, ffn_down_w)

    glu = _pw1_glu(x, row(norm_mix_g[0]), mvec(0, 0), mvec(0, 1),
                   conv_pw1_w[0], row(conv_pw1_b[0]))
    x = _conv_pw2(glu, conv_dw_w[0], row(conv_dw_b[0]), row(conv_ln_g[0]),
                  row(conv_ln_b[0]), conv_pw2_w[0], row(conv_pw2_b[0]), x, mvec(0, 2))
    x = ffn(x, 0)

    tables = _rope_tables(positions[0])
    head_gains = jnp.stack([k_norm_g, q_norm_g[0]])[:, None, :]
    kvq = _qkv(x, row(kv_norm_g), (kv_mod, 0), (kv_mod, 1),
               row(norm_mix_g[1]), mvec(1, 0), mvec(1, 1), w_qkv, head_gains, tables)
    outs, lses = zip(*[_band_attn(kvq[g], g) for g in range(N_GROUPS)])
    x = _mix_wo(outs, lses, w_o[0], x, mvec(1, 2))
    x = ffn(x, 1)
    return x[None]
```

```python
import functools
import math

import jax
import jax.numpy as jnp
from jax import lax
from jax.experimental import pallas as pl
from jax.experimental.pallas import tpu as pltpu

D_MODEL = 2048
SEQ = 8192
CONV_K = 31
FFN_CONV_K = 3
D_FF = 5632
GROUP_DILATIONS = (1, 4, 16)
GROUP_SPANS = (128, 128, 128)
N_GROUPS = 3
HEADS_PER_GROUP = 8
HEAD_DIM = 128
Q_WIDTH = N_GROUPS * HEADS_PER_GROUP * HEAD_DIM
O_WIDTH = HEADS_PER_GROUP * HEAD_DIM
ROT_DIM = HEAD_DIM // 4
ROPE_THETA = 500000.0
BLK = 128
EPS = 1e-6
NEG = -1e30

V7X_LANES = 128
V7X_SUBLANES = 8
V7X_BF16_ROWS_PER_VREG = 16
V7X_VMEM_BYTES = 64 * 1024 * 1024
VMEM_LIMIT = 56 * 1024 * 1024

TM = 1024
TN = 512
TN_GLU = TN
ROW_CHUNK = 64

F32 = jnp.float32
BF16 = jnp.bfloat16


def _params(n_axes):
    return pltpu.CompilerParams(
        dimension_semantics=("arbitrary",) * n_axes,
        vmem_limit_bytes=VMEM_LIMIT)


def _sigmoid(x):
    return 1.0 / (1.0 + jnp.exp(-x))


def _silu(x):
    return x * _sigmoid(x)


def _lane_slab(slab):
    return slice(slab * V7X_LANES, (slab + 1) * V7X_LANES)


def _matvec_kernel(c_ref, w_ref, b_ref, o_ref, sb_ref, *, k_dim, tn):
    first = (pl.program_id(0) == 0) & (pl.program_id(1) == 0)

    @pl.when(first)
    def _():
        c = c_ref[...]
        sb_ref[...] = jnp.broadcast_to(_silu(c), (k_dim, V7X_LANES))

    n_groups = tn // V7X_LANES

    def body(t, accs):
        r0 = pl.multiple_of(t * ROW_CHUNK, ROW_CHUNK)
        s = sb_ref[pl.ds(r0, ROW_CHUNK), :]
        w = w_ref[pl.ds(r0, ROW_CHUNK), :]
        new = []
        for g in range(n_groups):
            p = w[:, _lane_slab(g)] * s
            a = accs[g]
            for u in range(ROW_CHUNK // V7X_SUBLANES):
                a = a + p[u * V7X_SUBLANES:(u + 1) * V7X_SUBLANES, :]
            new.append(a)
        return tuple(new)

    init = tuple(jnp.zeros((V7X_SUBLANES, V7X_LANES), F32) for _ in range(n_groups))
    accs = lax.fori_loop(0, k_dim // ROW_CHUNK, body, init)
    row = jnp.concatenate([jnp.sum(a, axis=0, keepdims=True) for a in accs], axis=1)
    o_ref[...] = row + b_ref[...]


def _mod_matvec(c_col, w, b):
    n_l, k_dim, n = w.shape
    tn = 1024
    return pl.pallas_call(
        functools.partial(_matvec_kernel, k_dim=k_dim, tn=tn),
        grid=(n_l, n // tn),
        in_specs=[
            pl.BlockSpec((k_dim, 1), lambda l, j: (0, 0)),
            pl.BlockSpec((None, k_dim, tn), lambda l, j: (l, 0, j)),
            pl.BlockSpec((None, 1, tn), lambda l, j: (l, 0, j)),
        ],
        out_specs=pl.BlockSpec((None, 1, tn), lambda l, j: (l, 0, j)),
        out_shape=jax.ShapeDtypeStruct((n_l, 1, n), F32),
        scratch_shapes=[pltpu.VMEM((k_dim, V7X_LANES), F32)],
        compiler_params=_params(2),
        name="mod_matvec",
    )(c_col, w, b)


MOD_ROWS = V7X_BF16_ROWS_PER_VREG
MOD_UNROLL = 4


def _modulate_rows(x_ref, targets, *, rows, h_row0=0):
    def body(t, carry):
        r0 = pl.multiple_of(t * MOD_ROWS, MOD_ROWS)
        x = x_ref[pl.ds(r0, MOD_ROWS), :]
        ms = jnp.mean(x * x, axis=-1, keepdims=True)
        y = x * lax.rsqrt(ms + EPS)
        h0 = pl.multiple_of(h_row0 + t * MOD_ROWS, MOD_ROWS)
        for g_ref, shift_ref, scale_ref, h_ref in targets:
            h = (y * g_ref[...]) * (1.0 + scale_ref[...]) + shift_ref[...]
            h_ref[pl.ds(h0, MOD_ROWS), :] = h.astype(BF16)
        return carry

    trips = rows // MOD_ROWS
    lax.fori_loop(0, trips, body, 0, unroll=min(MOD_UNROLL, trips))


def _pw1_glu_kernel(x_ref, g_ref, sh_ref, sc_ref, wa_ref, wg_ref, ba_ref, bg_ref,
                    o_ref, h_ref):
    @pl.when(pl.program_id(1) == 0)
    def _():
        _modulate_rows(x_ref, [(g_ref, sh_ref, sc_ref, h_ref)], rows=TM)

    w = jnp.concatenate([wa_ref[...], wg_ref[...]], axis=1)
    u = jnp.dot(h_ref[...], w, preferred_element_type=F32)
    a = u[:, :TN_GLU] + ba_ref[...]
    gt = u[:, TN_GLU:] + bg_ref[...]
    o_ref[...] = a * _sigmoid(gt)


def _pw1_glu(x, norm_g, shift, scale, w, b):
    s, d = x.shape
    nj = d // TN_GLU
    vec = lambda col: pl.BlockSpec((1, d), lambda i, j: (0, col))
    return pl.pallas_call(
        _pw1_glu_kernel,
        grid=(s // TM, nj),
        in_specs=[
            pl.BlockSpec((TM, d), lambda i, j: (i, 0)),
            pl.BlockSpec((1, d), lambda i, j: (0, 0)),
            vec(shift[1]), vec(scale[1]),
            pl.BlockSpec((d, TN_GLU), lambda i, j: (0, j)),
            pl.BlockSpec((d, TN_GLU), lambda i, j: (0, nj + j)),
            pl.BlockSpec((1, TN_GLU), lambda i, j: (0, j)),
            pl.BlockSpec((1, TN_GLU), lambda i, j: (0, nj + j)),
        ],
        out_specs=pl.BlockSpec((TM, TN_GLU), lambda i, j: (i, j)),
        out_shape=jax.ShapeDtypeStruct((s, d), F32),
        scratch_shapes=[pltpu.VMEM((TM, d), BF16)],
        compiler_params=_params(2),
        name="pw1_glu",
    )(x, norm_g, shift[0], scale[0], w, w, b, b)


CONV_HALO = 32
CONV_ROWS = 32
CONV_UNROLL = 2


def _dwconv_ln_rows(gbuf_ref, dw_ref, db_ref, lg_ref, lb_ref, tmp_ref, hb_ref, *, rows, d):
    off = CONV_HALO - (CONV_K - 1)

    def chunk(r0, tmp):
        for slab in range(d // V7X_LANES):
            lanes = _lane_slab(slab)
            acc = jnp.broadcast_to(db_ref[:, lanes], (CONV_ROWS, V7X_LANES))
            for k in range(CONV_K):
                tap = gbuf_ref[slab, pl.ds(r0 + off + k, CONV_ROWS, stride=1), :]
                acc = acc + tap * dw_ref[k:k + 1, lanes]
            tmp[:, lanes] = acc
        u = tmp[...]
        mu = jnp.mean(u, axis=-1, keepdims=True)
        uc = u - mu
        var = jnp.mean(uc * uc, axis=-1, keepdims=True)
        y = uc * lax.rsqrt(var + EPS) * lg_ref[...] + lb_ref[...]
        hb_ref[pl.ds(r0, CONV_ROWS), :] = _silu(y).astype(BF16)

    def body(t, carry):
        for u in range(CONV_UNROLL):
            r0 = pl.multiple_of((t * CONV_UNROLL + u) * CONV_ROWS, CONV_ROWS)
            chunk(r0, tmp_ref.at[u])
        return carry

    lax.fori_loop(0, rows // (CONV_ROWS * CONV_UNROLL), body, 0)


def _conv_pw2_kernel(g_ref, halo_ref, dw_ref, db_ref, lg_ref, lb_ref, w_ref, b_ref,
                     x_ref, gate_ref, o_ref, gbuf_ref, tmp_ref, hb_ref, *, d):
    i = pl.program_id(0)

    @pl.when(pl.program_id(1) == 0)
    def _():
        for slab in range(d // V7X_LANES):
            lanes = _lane_slab(slab)
            halo = halo_ref[:, lanes]
            gbuf_ref[slab, 0:CONV_HALO, :] = jnp.where(i == 0, jnp.zeros_like(halo), halo)
            gbuf_ref[slab, CONV_HALO:, :] = g_ref[:, lanes]
        _dwconv_ln_rows(gbuf_ref, dw_ref, db_ref, lg_ref, lb_ref, tmp_ref, hb_ref,
                        rows=TM, d=d)

    y = jnp.dot(hb_ref[...], w_ref[...], preferred_element_type=F32)
    o_ref[...] = x_ref[...] + gate_ref[...] * (y + b_ref[...])


def _conv_pw2(glu, dw_w, dw_b, ln_g, ln_b, w, b, x, gate):
    s, d = glu.shape
    halo_blocks_per_tile = TM // CONV_HALO
    gate_col = gate[1] * (d // TN)
    return pl.pallas_call(
        functools.partial(_conv_pw2_kernel, d=d),
        grid=(s // TM, d // TN),
        in_specs=[
            pl.BlockSpec((TM, d), lambda i, j: (i, 0)),
            pl.BlockSpec((CONV_HALO, d),
                         lambda i, j: (jnp.maximum(i * halo_blocks_per_tile - 1, 0), 0)),
            pl.BlockSpec((CONV_K, d), lambda i, j: (0, 0)),
            pl.BlockSpec((1, d), lambda i, j: (0, 0)),
            pl.BlockSpec((1, d), lambda i, j: (0, 0)),
            pl.BlockSpec((1, d), lambda i, j: (0, 0)),
            pl.BlockSpec((d, TN), lambda i, j: (0, j)),
            pl.BlockSpec((1, TN), lambda i, j: (0, j)),
            pl.BlockSpec((TM, TN), lambda i, j: (i, j)),
            pl.BlockSpec((1, TN), lambda i, j: (0, gate_col + j)),
        ],
        out_specs=pl.BlockSpec((TM, TN), lambda i, j: (i, j)),
        out_shape=jax.ShapeDtypeStruct((s, d), F32),
        scratch_shapes=[
            pltpu.VMEM((d // V7X_LANES, TM + CONV_HALO, V7X_LANES), F32),
            pltpu.VMEM((CONV_UNROLL, CONV_ROWS, d), F32),
            pltpu.VMEM((TM, d), BF16),
        ],
        compiler_params=_params(2),
        name="dwconv_pw2",
    )(glu, glu, dw_w, dw_b, ln_g, ln_b, w, b, x, gate[0])


FFN_HALO = V7X_BF16_ROWS_PER_VREG
TF = 512
FFN_OUT_LANES = 512


def _ffn_kernel(x_ref, xh_ref, g_ref, sh_ref, sc_ref, gate_ref, wg_ref, wv_ref,
                dw_ref, db_ref, wd_ref, o_ref, h_ref, u_ref):
    i = pl.program_id(0)

    @pl.when(pl.program_id(1) == 0)
    def _():
        target = [(g_ref, sh_ref, sc_ref, h_ref)]
        _modulate_rows(xh_ref, target, rows=FFN_HALO)
        _modulate_rows(x_ref, target, rows=TM, h_row0=FFN_HALO)
        o_ref[...] = x_ref[...]

    w_up = jnp.concatenate([wg_ref[...], wv_ref[...]], axis=1)
    u = jnp.dot(h_ref[...], w_up, preferred_element_type=F32)
    u_ref[...] = u
    keep_halo = jnp.where(i == 0, 0.0, 1.0)
    u_ref[0:FFN_HALO, 0:TF] = u[0:FFN_HALO, 0:TF] * keep_halo

    gt = db_ref[...] + dw_ref[2:3, :] * u_ref[pl.ds(FFN_HALO, TM), 0:TF]
    gt = gt + dw_ref[1:2, :] * u_ref[pl.ds(FFN_HALO - 1, TM), 0:TF]
    gt = gt + dw_ref[0:1, :] * u_ref[pl.ds(FFN_HALO - 2, TM), 0:TF]
    val = u_ref[pl.ds(FFN_HALO, TM), TF:2 * TF]
    act = (_silu(gt) * val).astype(BF16)
    for c0 in range(0, o_ref.shape[1], FFN_OUT_LANES):
        cols = slice(c0, c0 + FFN_OUT_LANES)
        y = jnp.dot(act, wd_ref[:, cols], preferred_element_type=F32)
        o_ref[:, cols] += gate_ref[:, cols] * y


def _conv_ffn(x, norm_g, shift, scale, gate, layer, up_w, dw_w, dw_b, down_w):
    s, d = x.shape
    f = down_w.shape[1]
    nf = f // TF
    halo_blocks_per_tile = TM // FFN_HALO
    vec = lambda col: pl.BlockSpec((1, d), lambda i, j: (0, col))
    return pl.pallas_call(
        _ffn_kernel,
        grid=(s // TM, nf),
        in_specs=[
            pl.BlockSpec((TM, d), lambda i, j: (i, 0), pipeline_mode=pl.Buffered(1)),
            pl.BlockSpec((FFN_HALO, d),
                         lambda i, j: (jnp.maximum(i * halo_blocks_per_tile - 1, 0), 0)),
            pl.BlockSpec((1, d), lambda i, j: (0, 0)),
            vec(shift[1]), vec(scale[1]), vec(gate[1]),
            pl.BlockSpec((None, d, TF), lambda i, j: (layer, 0, j)),
            pl.BlockSpec((None, d, TF), lambda i, j: (layer, 0, nf + j)),
            pl.BlockSpec((None, FFN_CONV_K, TF), lambda i, j: (layer, 0, j)),
            pl.BlockSpec((None, 1, TF), lambda i, j: (layer, 0, j)),
            pl.BlockSpec((None, TF, d), lambda i, j: (layer, j, 0)),
        ],
        out_specs=pl.BlockSpec((TM, d), lambda i, j: (i, 0)),
        out_shape=jax.ShapeDtypeStruct((s, d), F32),
        scratch_shapes=[
            pltpu.VMEM((TM + FFN_HALO, d), BF16),
            pltpu.VMEM((TM + FFN_HALO, 2 * TF), F32),
        ],
        compiler_params=_params(2),
        name="conv_ffn",
    )(x, x, norm_g, shift[0], scale[0], gate[0], up_w, up_w, dw_w, dw_b, down_w)


def _rope_table_kernel(pos_ref, freq_ref, cos_ref, sin_lo_ref, sin_hi_ref):
    pos = pos_ref[...].astype(F32)
    ang = pos * freq_ref[...]
    lane = lax.broadcasted_iota(jnp.int32, ang.shape, 1)
    c = jnp.cos(ang)
    sn = jnp.sin(ang)
    half = ROT_DIM // 2
    cos_ref[...] = jnp.where(lane < ROT_DIM, c, 1.0)
    sin_lo_ref[...] = jnp.where(lane < half, -sn, 0.0)
    sin_hi_ref[...] = jnp.where((lane >= half) & (lane < ROT_DIM), sn, 0.0)


def _rope_tables(positions):
    s = positions.shape[0]
    rows = 1024
    inv_freq = ROPE_THETA ** (-jnp.arange(0, ROT_DIM, 2, dtype=F32) / ROT_DIM)
    lane_freq = jnp.concatenate(
        [inv_freq, inv_freq, jnp.zeros((HEAD_DIM - ROT_DIM,), F32)])[None, :]
    out = jax.ShapeDtypeStruct((s, HEAD_DIM), F32)
    spec = pl.BlockSpec((rows, HEAD_DIM), lambda i: (i, 0))
    return pl.pallas_call(
        _rope_table_kernel,
        grid=(s // rows,),
        in_specs=[pl.BlockSpec((rows, 1), lambda i: (i, 0)),
                  pl.BlockSpec((1, HEAD_DIM), lambda i: (0, 0))],
        out_specs=[spec, spec, spec],
        out_shape=[out, out, out],
        compiler_params=_params(1),
        name="rope_tables",
    )(positions.reshape(s, 1), lane_freq)


TILES_PER_GROUP = O_WIDTH // TN


def _norm_rope_head(q, hg, cos, sin_lo, sin_hi):
    ms = jnp.mean(q * q, axis=-1, keepdims=True)
    qn = q * lax.rsqrt(ms + EPS) * hg
    hi_to_lo = pltpu.roll(qn, HEAD_DIM - ROT_DIM // 2, axis=1)
    lo_to_hi = pltpu.roll(qn, ROT_DIM // 2, axis=1)
    return qn * cos + hi_to_lo * sin_lo + lo_to_hi * sin_hi


SPLIT_STRIDE = 4


def _finish_tile(y_ref, out_ref, head_gain, use_rope, table_refs, y4_ref, *, r):
    rows = TM // r
    two_pass = r > SPLIT_STRIDE
    r_outer = r // SPLIT_STRIDE
    for head in range(TN // HEAD_DIM):
        cos, sin_lo, sin_hi = (t[...] for t in table_refs)
        y = y_ref[head]
        y_ref[head] = jnp.where(use_rope,
                                _norm_rope_head(y, head_gain, cos, sin_lo, sin_hi), y)
        if two_pass:
            for p in range(SPLIT_STRIDE):
                y4_ref[head % 2, p] = y_ref[head, pl.ds(p, TM // SPLIT_STRIDE,
                                                        stride=SPLIT_STRIDE), :]
        for res in range(r):
            if r == 1:
                y = y_ref[head]
            elif two_pass:
                p, q = res % SPLIT_STRIDE, res // SPLIT_STRIDE
                y = y4_ref[head % 2, p, pl.ds(q, rows, stride=r_outer), :]
            else:
                y = y_ref[head, pl.ds(res, rows, stride=r), :]
            out_ref[res, :, _lane_slab(head)] = y.astype(BF16)


N_KINDS = 3
K_KIND, V_KIND, Q_KIND = range(N_KINDS)
GROUP_TILES = N_KINDS * TILES_PER_GROUP
ROW_TILES = N_GROUPS * GROUP_TILES


def _qkv_kernel(x_ref, gkv_ref, shkv_ref, sckv_ref, gq_ref, shq_ref, scq_ref, w_ref,
                hg_ref, cos_ref, slo_ref, shi_ref, *refs):
    outs = refs[:N_GROUPS]
    h_ref, ya_ref, yb_ref, y4_ref = refs[N_GROUPS:]
    s = pl.program_id(0)
    t = TILES_PER_GROUP
    tile = jnp.minimum(s, pl.num_programs(0) - 2)
    col = tile % ROW_TILES
    done_col = jnp.maximum(s - 1, 0) % ROW_TILES
    done_kind = (done_col % GROUP_TILES) // t
    done_group = done_col // GROUP_TILES
    tables = (cos_ref, slo_ref, shi_ref)

    @pl.when(col == 0)
    def _():
        _modulate_rows(x_ref, [(gkv_ref, shkv_ref, sckv_ref, h_ref.at[0]),
                               (gq_ref, shq_ref, scq_ref, h_ref.at[1])], rows=TM)

    @pl.when(s == 0)
    def _():
        ya_ref[...] = jnp.zeros(ya_ref.shape, F32)

    h_sel = ((col % GROUP_TILES) // t == Q_KIND).astype(jnp.int32)
    gain = hg_ref[(done_kind == Q_KIND).astype(jnp.int32)]
    use_rope = done_kind != V_KIND
    for group, r in enumerate(GROUP_DILATIONS):
        @pl.when(done_group == group)
        def _(group=group, r=r):
            yb_ref[...] = ya_ref[...]
            y = jnp.dot(h_ref[h_sel], w_ref[...], preferred_element_type=F32)
            for head in range(TN // HEAD_DIM):
                ya_ref[head] = y[:, _lane_slab(head)]
            _finish_tile(yb_ref, outs[group], gain, use_rope, tables, y4_ref, r=r)


def _qkv(x, kv_norm_g, kv_shift, kv_scale, q_norm_g, q_shift, q_scale, w_all, head_gains,
         tables):
    s, d = x.shape
    t = TILES_PER_GROUP
    n_tiles = (s // TM) * ROW_TILES
    vec = lambda col: pl.BlockSpec((1, d), lambda j: (0, col))
    one = pl.BlockSpec((1, d), lambda j: (0, 0))

    def tile_of(j):
        return jnp.minimum(j, n_tiles - 1)

    def done_of(j):
        return jnp.maximum(j - 1, 0)

    def w_col(j):
        col = tile_of(j) % ROW_TILES
        group, rem = col // GROUP_TILES, col % GROUP_TILES
        return (rem // t) * (N_GROUPS * t) + group * t + rem % t

    def out_spec(group):
        r = GROUP_DILATIONS[group]
        return pl.BlockSpec(
            (r, TM // r, TN),
            lambda j: (0, done_of(j) // ROW_TILES,
                       jnp.clip(done_of(j) % ROW_TILES - group * GROUP_TILES,
                                0, GROUP_TILES - 1)))

    tab = pl.BlockSpec((TM, HEAD_DIM), lambda j: (done_of(j) // ROW_TILES, 0))
    return pl.pallas_call(
        _qkv_kernel,
        grid=(n_tiles + 1,),
        in_specs=[
            pl.BlockSpec((TM, d), lambda j: (tile_of(j) // ROW_TILES, 0)),
            one, vec(kv_shift[1]), vec(kv_scale[1]),
            one, vec(q_shift[1]), vec(q_scale[1]),
            pl.BlockSpec((d, TN), lambda j: (0, w_col(j))),
            pl.BlockSpec((2, 1, HEAD_DIM), lambda j: (0, 0, 0)),
            tab, tab, tab,
        ],
        out_specs=[out_spec(g) for g in range(N_GROUPS)],
        out_shape=[jax.ShapeDtypeStruct((r, s // r, N_KINDS * O_WIDTH), BF16)
                   for r in GROUP_DILATIONS],
        scratch_shapes=[
            pltpu.VMEM((2, TM, d), BF16),
            pltpu.VMEM((TN // V7X_LANES, TM, V7X_LANES), F32),
            pltpu.VMEM((TN // V7X_LANES, TM, V7X_LANES), F32),
            pltpu.VMEM((2, SPLIT_STRIDE, TM // SPLIT_STRIDE, V7X_LANES), F32),
        ],
        compiler_params=_params(1),
        name="qkv",
    )(x, kv_norm_g, kv_shift[0], kv_scale[0], q_norm_g, q_shift[0], q_scale[0],
      w_all, head_gains, *tables)


ATTN_Q_BLOCKS = 2


def _band_attn_kernel(q_ref, kp_ref, kc_ref, vp_ref, vc_ref, o_ref, lse_ref, *, span):
    n = pl.program_id(1)
    qi = lax.broadcasted_iota(jnp.int32, (BLK, 2 * BLK), 0)
    kj = lax.broadcasted_iota(jnp.int32, (BLK, 2 * BLK), 1)
    dist = qi + BLK - kj
    band = (dist >= 0) & (dist <= span)
    first_band = band & ((n > 0) | (kj >= BLK))
    lane = lax.broadcasted_iota(jnp.int32, (BLK, V7X_LANES), 1)
    scale = 1.0 / math.sqrt(HEAD_DIM)
    for blk in range(ATTN_Q_BLOCKS):
        rows = slice(blk * BLK, (blk + 1) * BLK)
        lse_tile = jnp.zeros((BLK, V7X_LANES), F32)
        for h in range(HEADS_PER_GROUP):
            cols = slice(h * HEAD_DIM, (h + 1) * HEAD_DIM)
            q = q_ref[rows, cols]
            if blk == 0:
                k = jnp.concatenate([kp_ref[:, cols], kc_ref[rows, cols]], axis=0)
                v = jnp.concatenate([vp_ref[:, cols], vc_ref[rows, cols]], axis=0)
                mask = first_band
            else:
                keys = slice((blk - 1) * BLK, (blk + 1) * BLK)
                k, v, mask = kc_ref[keys, cols], vc_ref[keys, cols], band
            sc = lax.dot_general(q, k, (((1,), (1,)), ((), ())),
                                 preferred_element_type=F32) * scale
            sc = jnp.where(mask, sc, NEG)
            m = jnp.max(sc, axis=-1, keepdims=True)
            e = jnp.exp(sc - m)
            l = jnp.sum(e, axis=-1, keepdims=True)
            p = (e * (1.0 / l)).astype(BF16)
            o_ref[rows, cols] = jnp.dot(p, v, preferred_element_type=F32)
            lse_tile = jnp.where(lane == h, m + jnp.log(l), lse_tile)
        lse_ref[rows, :] = lse_tile


def _band_attn(kvq, group):
    r, rows, _ = kvq.shape
    step_rows = ATTN_Q_BLOCKS * BLK

    def cur(col):
        return pl.BlockSpec((None, step_rows, O_WIDTH), lambda j, n: (j, n, col))

    def prev(col):
        return pl.BlockSpec((None, BLK, O_WIDTH),
                            lambda j, n: (j, jnp.maximum(n * ATTN_Q_BLOCKS - 1, 0), col))

    return pl.pallas_call(
        functools.partial(_band_attn_kernel, span=GROUP_SPANS[group]),
        grid=(r, rows // step_rows),
        in_specs=[cur(Q_KIND), prev(K_KIND), cur(K_KIND), prev(V_KIND), cur(V_KIND)],
        out_specs=[cur(0),
                   pl.BlockSpec((None, step_rows, V7X_LANES), lambda j, n: (j, n, 0))],
        out_shape=[jax.ShapeDtypeStruct((r, rows, O_WIDTH), F32),
                   jax.ShapeDtypeStruct((r, rows, V7X_LANES), F32)],
        compiler_params=_params(2),
        name=f"band_attn_r{r}",
    )(kvq, kvq, kvq, kvq, kvq)


TM_MIX = 1024


def _rows_from_residues(src_ref, dst_ref, *, r, lanes=None):
    n = src_ref.shape[1]
    for res in range(r):
        rows = pl.ds(res, n, stride=r)
        if lanes is None:
            dst_ref[rows, :] = src_ref[res]
        else:
            for slab in range(lanes // V7X_LANES):
                dst_ref[slab, rows, :] = src_ref[res, :, _lane_slab(slab)]


def _mix_wo_kernel(o0_ref, o1_ref, o2_ref, l0_ref, l1_ref, l2_ref, w_ref, x_ref,
                   gate_ref, out_ref, hb_ref, on1_ref, on2_ref, ln1_ref, ln2_ref):
    @pl.when(pl.program_id(1) == 0)
    def _():
        r1, r2 = GROUP_DILATIONS[1], GROUP_DILATIONS[2]
        _rows_from_residues(l1_ref, ln1_ref, r=r1)
        _rows_from_residues(l2_ref, ln2_ref, r=r2)
        _rows_from_residues(o1_ref, on1_ref, r=r1, lanes=O_WIDTH)
        _rows_from_residues(o2_ref, on2_ref, r=r2, lanes=O_WIDTH)
        l0, l1, l2 = l0_ref[0], ln1_ref[...], ln2_ref[...]
        m = jnp.maximum(jnp.maximum(l0, l1), l2)
        e0, e1, e2 = jnp.exp(l0 - m), jnp.exp(l1 - m), jnp.exp(l2 - m)
        inv = 1.0 / (e0 + e1 + e2)
        a0, a1, a2 = e0 * inv, e1 * inv, e2 * inv
        for h in range(HEADS_PER_GROUP):
            cols = _lane_slab(h)
            o = (a0[:, h:h + 1] * o0_ref[0, :, cols] + a1[:, h:h + 1] * on1_ref[h]
                 + a2[:, h:h + 1] * on2_ref[h])
            hb_ref[:, cols] = o.astype(BF16)

    y = jnp.dot(hb_ref[...], w_ref[...], preferred_element_type=F32)
    out_ref[...] = x_ref[...] + gate_ref[...] * y


def _mix_wo(outs, lses, w_o, x, gate):
    s, d = x.shape
    gate_col = gate[1] * (d // TN)

    def planes(width, r):
        return pl.BlockSpec((r, TM_MIX // r, width), lambda i, j: (0, i, 0))

    slabs = pltpu.VMEM((O_WIDTH // V7X_LANES, TM_MIX, V7X_LANES), F32)
    rows = pltpu.VMEM((TM_MIX, V7X_LANES), F32)
    return pl.pallas_call(
        _mix_wo_kernel,
        grid=(s // TM_MIX, d // TN),
        in_specs=[planes(O_WIDTH, r) for r in GROUP_DILATIONS]
        + [planes(V7X_LANES, r) for r in GROUP_DILATIONS]
        + [pl.BlockSpec((O_WIDTH, TN), lambda i, j: (0, j)),
           pl.BlockSpec((TM_MIX, TN), lambda i, j: (i, j)),
           pl.BlockSpec((1, TN), lambda i, j: (0, gate_col + j))],
        out_specs=pl.BlockSpec((TM_MIX, TN), lambda i, j: (i, j)),
        out_shape=jax.ShapeDtypeStruct((s, d), F32),
        scratch_shapes=[pltpu.VMEM((TM_MIX, O_WIDTH), BF16), slabs, slabs, rows, rows],
        compiler_params=_params(2),
        name="mix_wo",
    )(*outs, *lses, w_o, x, gate[0])


def kernel(x, c, positions, mod_w, mod_b, norm_mix_g, norm_ffn_g, conv_pw1_w, conv_pw1_b,
           conv_dw_w, conv_dw_b, conv_ln_g, conv_ln_b, conv_pw2_w, conv_pw2_b, kv_mod_w,
           kv_mod_b, kv_norm_g, w_kv, k_norm_g, w_q, q_norm_g, w_o, ffn_up_w, ffn_dw_w,
           ffn_dw_b, ffn_down_w):
    batch, s, d = x.shape
    assert (batch, s, d) == (1, SEQ, D_MODEL)
    x = x[0]
    c_col = c.reshape(d, 1)
    row = lambda v: v.reshape(1, -1)

    mod = _mod_matvec(c_col, mod_w, mod_b[:, None, :])
    kv_mod = _mod_matvec(c_col, kv_mod_w[None], kv_mod_b[None, None, :])[0]

    mvec = lambda l, q: (mod[l], q)
    ffn_dw_b3 = ffn_dw_b[:, None, :]

    conv_pw1_w, conv_pw2_w, w_o, ffn_up_w, ffn_down_w = (
        w.astype(BF16) for w in (conv_pw1_w, conv_pw2_w, w_o, ffn_up_w, ffn_down_w))
    w_qkv = jnp.concatenate([w_kv.astype(BF16), w_q[0].astype(BF16)], axis=1)

    def ffn(x, l):
        return _conv_ffn(x, row(norm_ffn_g[l]), mvec(l, 3), mvec(l, 4), mvec(l, 5), l,
                         ffn_up_w, ffn_dw_w, ffn_dw_b3, ffn_down_w)

    glu = _pw1_glu(x, row(norm_mix_g[0]), mvec(0, 0), mvec(0, 1),
                   conv_pw1_w[0], row(conv_pw1_b[0]))
    x = _conv_pw2(glu, conv_dw_w[0], row(conv_dw_b[0]), row(conv_ln_g[0]),
                  row(conv_ln_b[0]), conv_pw2_w[0], row(conv_pw2_b[0]), x, mvec(0, 2))
    x = ffn(x, 0)

    tables = _rope_tables(positions[0])
    head_gains = jnp.stack([k_norm_g, q_norm_g[0]])[:, None, :]
    kvq = _qkv(x, row(kv_norm_g), (kv_mod, 0), (kv_mod, 1),
               row(norm_mix_g[1]), mvec(1, 0), mvec(1, 1), w_qkv, head_gains, tables)
    outs, lses = zip(*[_band_attn(kvq[g], g) for g in range(N_GROUPS)])
    x = _mix_wo(outs, lses, w_o[0], x, mvec(1, 2))
    x = ffn(x, 1)
    return x[None]
```

```python
import functools
import math

import jax
import jax.numpy as jnp
from jax import lax
from jax.experimental import pallas as pl
from jax.experimental.pallas import tpu as pltpu

D_MODEL = 2048
SEQ = 8192
CONV_K = 31
FFN_CONV_K = 3
D_FF = 5632
GROUP_DILATIONS = (1, 4, 16)
GROUP_SPANS = (128, 128, 128)
N_GROUPS = 3
HEADS_PER_GROUP = 8
HEAD_DIM = 128
Q_WIDTH = N_GROUPS * HEADS_PER_GROUP * HEAD_DIM
O_WIDTH = HEADS_PER_GROUP * HEAD_DIM
ROT_DIM = HEAD_DIM // 4
ROPE_THETA = 500000.0
BLK = 128
EPS = 1e-6
NEG = -1e30

V7X_LANES = 128
V7X_SUBLANES = 8
V7X_BF16_ROWS_PER_VREG = 16
V7X_VMEM_BYTES = 64 * 1024 * 1024
VMEM_LIMIT = 56 * 1024 * 1024

TM = 1024
TN = 512
TN_GLU = TN
ROW_CHUNK = 64

F32 = jnp.float32
BF16 = jnp.bfloat16


def _params(n_axes, vmem_limit=VMEM_LIMIT):
    return pltpu.CompilerParams(
        dimension_semantics=("arbitrary",) * n_axes,
        vmem_limit_bytes=vmem_limit)


def _sigmoid(x):
    return 1.0 / (1.0 + jnp.exp(-x))


def _silu(x):
    return x * _sigmoid(x)


def _lane_slab(slab):
    return slice(slab * V7X_LANES, (slab + 1) * V7X_LANES)


def _matvec_kernel(c_ref, w_ref, b_ref, o_ref, sb_ref, *, k_dim, tn):
    first = (pl.program_id(0) == 0) & (pl.program_id(1) == 0)

    @pl.when(first)
    def _():
        c = c_ref[...]
        sb_ref[...] = jnp.broadcast_to(_silu(c), (k_dim, V7X_LANES))

    n_groups = tn // V7X_LANES

    def body(t, accs):
        r0 = pl.multiple_of(t * ROW_CHUNK, ROW_CHUNK)
        s = sb_ref[pl.ds(r0, ROW_CHUNK), :]
        w = w_ref[pl.ds(r0, ROW_CHUNK), :]
        new = []
        for g in range(n_groups):
            p = w[:, _lane_slab(g)] * s
            a = accs[g]
            for u in range(ROW_CHUNK // V7X_SUBLANES):
                a = a + p[u * V7X_SUBLANES:(u + 1) * V7X_SUBLANES, :]
            new.append(a)
        return tuple(new)

    init = tuple(jnp.zeros((V7X_SUBLANES, V7X_LANES), F32) for _ in range(n_groups))
    accs = lax.fori_loop(0, k_dim // ROW_CHUNK, body, init)
    row = jnp.concatenate([jnp.sum(a, axis=0, keepdims=True) for a in accs], axis=1)
    o_ref[...] = row + b_ref[...]


def _mod_matvec(c_col, w, b):
    n_l, k_dim, n = w.shape
    tn = 1024
    return pl.pallas_call(
        functools.partial(_matvec_kernel, k_dim=k_dim, tn=tn),
        grid=(n_l, n // tn),
        in_specs=[
            pl.BlockSpec((k_dim, 1), lambda l, j: (0, 0)),
            pl.BlockSpec((None, k_dim, tn), lambda l, j: (l, 0, j)),
            pl.BlockSpec((None, 1, tn), lambda l, j: (l, 0, j)),
        ],
        out_specs=pl.BlockSpec((None, 1, tn), lambda l, j: (l, 0, j)),
        out_shape=jax.ShapeDtypeStruct((n_l, 1, n), F32),
        scratch_shapes=[pltpu.VMEM((k_dim, V7X_LANES), F32)],
        compiler_params=_params(2),
        name="mod_matvec",
    )(c_col, w, b)


MOD_ROWS = V7X_BF16_ROWS_PER_VREG
MOD_UNROLL = 4


def _modulate_rows(x_ref, targets, *, rows, h_row0=0):
    def body(t, carry):
        r0 = pl.multiple_of(t * MOD_ROWS, MOD_ROWS)
        x = x_ref[pl.ds(r0, MOD_ROWS), :]
        ms = jnp.mean(x * x, axis=-1, keepdims=True)
        y = x * lax.rsqrt(ms + EPS)
        h0 = pl.multiple_of(h_row0 + t * MOD_ROWS, MOD_ROWS)
        for g_ref, shift_ref, scale_ref, h_ref in targets:
            h = (y * g_ref[...]) * (1.0 + scale_ref[...]) + shift_ref[...]
            h_ref[pl.ds(h0, MOD_ROWS), :] = h.astype(BF16)
        return carry

    trips = rows // MOD_ROWS
    lax.fori_loop(0, trips, body, 0, unroll=min(MOD_UNROLL, trips))


def _pw1_glu_kernel(x_ref, g_ref, sh_ref, sc_ref, wa_ref, wg_ref, ba_ref, bg_ref,
                    o_ref, h_ref):
    @pl.when(pl.program_id(1) == 0)
    def _():
        _modulate_rows(x_ref, [(g_ref, sh_ref, sc_ref, h_ref)], rows=TM)

    w = jnp.concatenate([wa_ref[...], wg_ref[...]], axis=1)
    u = jnp.dot(h_ref[...], w, preferred_element_type=F32)
    a = u[:, :TN_GLU] + ba_ref[...]
    gt = u[:, TN_GLU:] + bg_ref[...]
    o_ref[...] = a * _sigmoid(gt)


def _pw1_glu(x, norm_g, shift, scale, w, b):
    s, d = x.shape
    nj = d // TN_GLU
    vec = lambda col: pl.BlockSpec((1, d), lambda i, j: (0, col))
    return pl.pallas_call(
        _pw1_glu_kernel,
        grid=(s // TM, nj),
        in_specs=[
            pl.BlockSpec((TM, d), lambda i, j: (i, 0)),
            pl.BlockSpec((1, d), lambda i, j: (0, 0)),
            vec(shift[1]), vec(scale[1]),
            pl.BlockSpec((d, TN_GLU), lambda i, j: (0, j)),
            pl.BlockSpec((d, TN_GLU), lambda i, j: (0, nj + j)),
            pl.BlockSpec((1, TN_GLU), lambda i, j: (0, j)),
            pl.BlockSpec((1, TN_GLU), lambda i, j: (0, nj + j)),
        ],
        out_specs=pl.BlockSpec((TM, TN_GLU), lambda i, j: (i, j)),
        out_shape=jax.ShapeDtypeStruct((s, d), F32),
        scratch_shapes=[pltpu.VMEM((TM, d), BF16)],
        compiler_params=_params(2),
        name="pw1_glu",
    )(x, norm_g, shift[0], scale[0], w, w, b, b)


CONV_HALO = 32
CONV_ROWS = 32
CONV_UNROLL = 2


def _dwconv_ln_rows(gbuf_ref, dw_ref, db_ref, lg_ref, lb_ref, tmp_ref, hb_ref, *, rows, d):
    off = CONV_HALO - (CONV_K - 1)

    def chunk(r0, tmp):
        for slab in range(d // V7X_LANES):
            lanes = _lane_slab(slab)
            acc = jnp.broadcast_to(db_ref[:, lanes], (CONV_ROWS, V7X_LANES))
            for k in range(CONV_K):
                tap = gbuf_ref[slab, pl.ds(r0 + off + k, CONV_ROWS, stride=1), :]
                acc = acc + tap * dw_ref[k:k + 1, lanes]
            tmp[:, lanes] = acc
        u = tmp[...]
        mu = jnp.mean(u, axis=-1, keepdims=True)
        uc = u - mu
        var = jnp.mean(uc * uc, axis=-1, keepdims=True)
        y = uc * lax.rsqrt(var + EPS) * lg_ref[...] + lb_ref[...]
        hb_ref[pl.ds(r0, CONV_ROWS), :] = _silu(y).astype(BF16)

    def body(t, carry):
        for u in range(CONV_UNROLL):
            r0 = pl.multiple_of((t * CONV_UNROLL + u) * CONV_ROWS, CONV_ROWS)
            chunk(r0, tmp_ref.at[u])
        return carry

    lax.fori_loop(0, rows // (CONV_ROWS * CONV_UNROLL), body, 0)


def _conv_pw2_kernel(g_ref, halo_ref, dw_ref, db_ref, lg_ref, lb_ref, w_ref, b_ref,
                     x_ref, gate_ref, o_ref, gbuf_ref, tmp_ref, hb_ref, *, d):
    i = pl.program_id(0)

    @pl.when(pl.program_id(1) == 0)
    def _():
        for slab in range(d // V7X_LANES):
            lanes = _lane_slab(slab)
            halo = halo_ref[:, lanes]
            gbuf_ref[slab, 0:CONV_HALO, :] = jnp.where(i == 0, jnp.zeros_like(halo), halo)
            gbuf_ref[slab, CONV_HALO:, :] = g_ref[:, lanes]
        _dwconv_ln_rows(gbuf_ref, dw_ref, db_ref, lg_ref, lb_ref, tmp_ref, hb_ref,
                        rows=TM, d=d)

    y = jnp.dot(hb_ref[...], w_ref[...], preferred_element_type=F32)
    o_ref[...] = x_ref[...] + gate_ref[...] * (y + b_ref[...])


def _conv_pw2(glu, dw_w, dw_b, ln_g, ln_b, w, b, x, gate):
    s, d = glu.shape
    halo_blocks_per_tile = TM // CONV_HALO
    gate_col = gate[1] * (d // TN)
    return pl.pallas_call(
        functools.partial(_conv_pw2_kernel, d=d),
        grid=(s // TM, d // TN),
        in_specs=[
            pl.BlockSpec((TM, d), lambda i, j: (i, 0)),
            pl.BlockSpec((CONV_HALO, d),
                         lambda i, j: (jnp.maximum(i * halo_blocks_per_tile - 1, 0), 0)),
            pl.BlockSpec((CONV_K, d), lambda i, j: (0, 0)),
            pl.BlockSpec((1, d), lambda i, j: (0, 0)),
            pl.BlockSpec((1, d), lambda i, j: (0, 0)),
            pl.BlockSpec((1, d), lambda i, j: (0, 0)),
            pl.BlockSpec((d, TN), lambda i, j: (0, j)),
            pl.BlockSpec((1, TN), lambda i, j: (0, j)),
            pl.BlockSpec((TM, TN), lambda i, j: (i, j)),
            pl.BlockSpec((1, TN), lambda i, j: (0, gate_col + j)),
        ],
        out_specs=pl.BlockSpec((TM, TN), lambda i, j: (i, j)),
        out_shape=jax.ShapeDtypeStruct((s, d), F32),
        scratch_shapes=[
            pltpu.VMEM((d // V7X_LANES, TM + CONV_HALO, V7X_LANES), F32),
            pltpu.VMEM((CONV_UNROLL, CONV_ROWS, d), F32),
            pltpu.VMEM((TM, d), BF16),
        ],
        compiler_params=_params(2),
        name="dwconv_pw2",
    )(glu, glu, dw_w, dw_b, ln_g, ln_b, w, b, x, gate[0])


FFN_HALO = V7X_BF16_ROWS_PER_VREG
TF = 512
FFN_OUT_LANES = 512
FFN_VMEM_LIMIT = 60 * 1024 * 1024


def _ffn_kernel(x_ref, xh_ref, g_ref, sh_ref, sc_ref, gate_ref, wg_ref, wv_ref,
                dw_ref, db_ref, wd_ref, o_ref, h_ref, u_ref):
    i = pl.program_id(0)

    @pl.when(pl.program_id(1) == 0)
    def _():
        target = [(g_ref, sh_ref, sc_ref, h_ref)]
        _modulate_rows(xh_ref, target, rows=FFN_HALO)
        _modulate_rows(x_ref, target, rows=TM, h_row0=FFN_HALO)
        o_ref[...] = x_ref[...]

    w_up = jnp.concatenate([wg_ref[...], wv_ref[...]], axis=1)
    u = jnp.dot(h_ref[...], w_up, preferred_element_type=F32)
    u_ref[...] = u
    keep_halo = jnp.where(i == 0, 0.0, 1.0)
    u_ref[0:FFN_HALO, 0:TF] = u[0:FFN_HALO, 0:TF] * keep_halo

    gt = db_ref[...] + dw_ref[2:3, :] * u_ref[pl.ds(FFN_HALO, TM), 0:TF]
    gt = gt + dw_ref[1:2, :] * u_ref[pl.ds(FFN_HALO - 1, TM), 0:TF]
    gt = gt + dw_ref[0:1, :] * u_ref[pl.ds(FFN_HALO - 2, TM), 0:TF]
    val = u_ref[pl.ds(FFN_HALO, TM), TF:2 * TF]
    act = (_silu(gt) * val).astype(BF16)
    for c0 in range(0, o_ref.shape[1], FFN_OUT_LANES):
        cols = slice(c0, c0 + FFN_OUT_LANES)
        y = jnp.dot(act, wd_ref[:, cols], preferred_element_type=F32)
        o_ref[:, cols] += gate_ref[:, cols] * y


def _conv_ffn(x, norm_g, shift, scale, gate, layer, up_w, dw_w, dw_b, down_w):
    s, d = x.shape
    f = down_w.shape[1]
    nf = f // TF
    halo_blocks_per_tile = TM // FFN_HALO
    vec = lambda col: pl.BlockSpec((1, d), lambda i, j: (0, col))
    return pl.pallas_call(
        _ffn_kernel,
        grid=(s // TM, nf),
        in_specs=[
            pl.BlockSpec((TM, d), lambda i, j: (i, 0)),
            pl.BlockSpec((FFN_HALO, d),
                         lambda i, j: (jnp.maximum(i * halo_blocks_per_tile - 1, 0), 0)),
            pl.BlockSpec((1, d), lambda i, j: (0, 0)),
            vec(shift[1]), vec(scale[1]), vec(gate[1]),
            pl.BlockSpec((None, d, TF), lambda i, j: (layer, 0, j)),
            pl.BlockSpec((None, d, TF), lambda i, j: (layer, 0, nf + j)),
            pl.BlockSpec((None, FFN_CONV_K, TF), lambda i, j: (layer, 0, j)),
            pl.BlockSpec((None, 1, TF), lambda i, j: (layer, 0, j)),
            pl.BlockSpec((None, TF, d), lambda i, j: (layer, j, 0)),
        ],
        out_specs=pl.BlockSpec((TM, d), lambda i, j: (i, 0)),
        out_shape=jax.ShapeDtypeStruct((s, d), F32),
        scratch_shapes=[
            pltpu.VMEM((TM + FFN_HALO, d), BF16),
            pltpu.VMEM((TM + FFN_HALO, 2 * TF), F32),
        ],
        compiler_params=_params(2, FFN_VMEM_LIMIT),
        name="conv_ffn",
    )(x, x, norm_g, shift[0], scale[0], gate[0], up_w, up_w, dw_w, dw_b, down_w)


def _rope_table_kernel(pos_ref, freq_ref, cos_ref, sin_lo_ref, sin_hi_ref):
    pos = pos_ref[...].astype(F32)
    ang = pos * freq_ref[...]
    lane = lax.broadcasted_iota(jnp.int32, ang.shape, 1)
    c = jnp.cos(ang)
    sn = jnp.sin(ang)
    half = ROT_DIM // 2
    cos_ref[...] = jnp.where(lane < ROT_DIM, c, 1.0)
    sin_lo_ref[...] = jnp.where(lane < half, -sn, 0.0)
    sin_hi_ref[...] = jnp.where((lane >= half) & (lane < ROT_DIM), sn, 0.0)


def _rope_tables(positions):
    s = positions.shape[0]
    rows = 1024
    inv_freq = ROPE_THETA ** (-jnp.arange(0, ROT_DIM, 2, dtype=F32) / ROT_DIM)
    lane_freq = jnp.concatenate(
        [inv_freq, inv_freq, jnp.zeros((HEAD_DIM - ROT_DIM,), F32)])[None, :]
    out = jax.ShapeDtypeStruct((s, HEAD_DIM), F32)
    spec = pl.BlockSpec((rows, HEAD_DIM), lambda i: (i, 0))
    return pl.pallas_call(
        _rope_table_kernel,
        grid=(s // rows,),
        in_specs=[pl.BlockSpec((rows, 1), lambda i: (i, 0)),
                  pl.BlockSpec((1, HEAD_DIM), lambda i: (0, 0))],
        out_specs=[spec, spec, spec],
        out_shape=[out, out, out],
        compiler_params=_params(1),
        name="rope_tables",
    )(positions.reshape(s, 1), lane_freq)


TILES_PER_GROUP = O_WIDTH // TN


def _norm_rope_head(q, hg, cos, sin_lo, sin_hi):
    ms = jnp.mean(q * q, axis=-1, keepdims=True)
    qn = q * lax.rsqrt(ms + EPS) * hg
    hi_to_lo = pltpu.roll(qn, HEAD_DIM - ROT_DIM // 2, axis=1)
    lo_to_hi = pltpu.roll(qn, ROT_DIM // 2, axis=1)
    return qn * cos + hi_to_lo * sin_lo + lo_to_hi * sin_hi


SPLIT_STRIDE = 4


def _finish_tile(y_ref, out_ref, head_gain, use_rope, table_refs, y4_ref, *, r):
    rows = TM // r
    two_pass = r > SPLIT_STRIDE
    r_outer = r // SPLIT_STRIDE
    for head in range(TN // HEAD_DIM):
        cos, sin_lo, sin_hi = (t[...] for t in table_refs)
        y = y_ref[head]
        y_ref[head] = jnp.where(use_rope,
                                _norm_rope_head(y, head_gain, cos, sin_lo, sin_hi), y)
        if two_pass:
            for p in range(SPLIT_STRIDE):
                y4_ref[head % 2, p] = y_ref[head, pl.ds(p, TM // SPLIT_STRIDE,
                                                        stride=SPLIT_STRIDE), :]
        for res in range(r):
            if r == 1:
                y = y_ref[head]
            elif two_pass:
                p, q = res % SPLIT_STRIDE, res // SPLIT_STRIDE
                y = y4_ref[head % 2, p, pl.ds(q, rows, stride=r_outer), :]
            else:
                y = y_ref[head, pl.ds(res, rows, stride=r), :]
            out_ref[res, :, _lane_slab(head)] = y.astype(BF16)


N_KINDS = 3
K_KIND, V_KIND, Q_KIND = range(N_KINDS)
GROUP_TILES = N_KINDS * TILES_PER_GROUP
ROW_TILES = N_GROUPS * GROUP_TILES


def _qkv_kernel(x_ref, gkv_ref, shkv_ref, sckv_ref, gq_ref, shq_ref, scq_ref, w_ref,
                hg_ref, cos_ref, slo_ref, shi_ref, *refs):
    outs = refs[:N_GROUPS]
    h_ref, ya_ref, yb_ref, y4_ref = refs[N_GROUPS:]
    s = pl.program_id(0)
    t = TILES_PER_GROUP
    tile = jnp.minimum(s, pl.num_programs(0) - 2)
    col = tile % ROW_TILES
    done_col = jnp.maximum(s - 1, 0) % ROW_TILES
    done_kind = (done_col % GROUP_TILES) // t
    done_group = done_col // GROUP_TILES
    tables = (cos_ref, slo_ref, shi_ref)

    @pl.when(col == 0)
    def _():
        _modulate_rows(x_ref, [(gkv_ref, shkv_ref, sckv_ref, h_ref.at[0]),
                               (gq_ref, shq_ref, scq_ref, h_ref.at[1])], rows=TM)

    @pl.when(s == 0)
    def _():
        ya_ref[...] = jnp.zeros(ya_ref.shape, F32)

    h_sel = ((col % GROUP_TILES) // t == Q_KIND).astype(jnp.int32)
    gain = hg_ref[(done_kind == Q_KIND).astype(jnp.int32)]
    use_rope = done_kind != V_KIND
    for group, r in enumerate(GROUP_DILATIONS):
        @pl.when(done_group == group)
        def _(group=group, r=r):
            yb_ref[...] = ya_ref[...]
            y = jnp.dot(h_ref[h_sel], w_ref[...], preferred_element_type=F32)
            for head in range(TN // HEAD_DIM):
                ya_ref[head] = y[:, _lane_slab(head)]
            _finish_tile(yb_ref, outs[group], gain, use_rope, tables, y4_ref, r=r)


def _qkv(x, kv_norm_g, kv_shift, kv_scale, q_norm_g, q_shift, q_scale, w_all, head_gains,
         tables):
    s, d = x.shape
    t = TILES_PER_GROUP
    n_tiles = (s // TM) * ROW_TILES
    vec = lambda col: pl.BlockSpec((1, d), lambda j: (0, col))
    one = pl.BlockSpec((1, d), lambda j: (0, 0))

    def tile_of(j):
        return jnp.minimum(j, n_tiles - 1)

    def done_of(j):
        return jnp.maximum(j - 1, 0)

    def w_col(j):
        col = tile_of(j) % ROW_TILES
        group, rem = col // GROUP_TILES, col % GROUP_TILES
        return (rem // t) * (N_GROUPS * t) + group * t + rem % t

    def out_spec(group):
        r = GROUP_DILATIONS[group]
        return pl.BlockSpec(
            (r, TM // r, TN),
            lambda j: (0, done_of(j) // ROW_TILES,
                       jnp.clip(done_of(j) % ROW_TILES - group * GROUP_TILES,
                                0, GROUP_TILES - 1)))

    tab = pl.BlockSpec((TM, HEAD_DIM), lambda j: (done_of(j) // ROW_TILES, 0))
    return pl.pallas_call(
        _qkv_kernel,
        grid=(n_tiles + 1,),
        in_specs=[
            pl.BlockSpec((TM, d), lambda j: (tile_of(j) // ROW_TILES, 0)),
            one, vec(kv_shift[1]), vec(kv_scale[1]),
            one, vec(q_shift[1]), vec(q_scale[1]),
            pl.BlockSpec((d, TN), lambda j: (0, w_col(j))),
            pl.BlockSpec((2, 1, HEAD_DIM), lambda j: (0, 0, 0)),
            tab, tab, tab,
        ],
        out_specs=[out_spec(g) for g in range(N_GROUPS)],
        out_shape=[jax.ShapeDtypeStruct((r, s // r, N_KINDS * O_WIDTH), BF16)
                   for r in GROUP_DILATIONS],
        scratch_shapes=[
            pltpu.VMEM((2, TM, d), BF16),
            pltpu.VMEM((TN // V7X_LANES, TM, V7X_LANES), F32),
            pltpu.VMEM((TN // V7X_LANES, TM, V7X_LANES), F32),
            pltpu.VMEM((2, SPLIT_STRIDE, TM // SPLIT_STRIDE, V7X_LANES), F32),
        ],
        compiler_params=_params(1),
        name="qkv",
    )(x, kv_norm_g, kv_shift[0], kv_scale[0], q_norm_g, q_shift[0], q_scale[0],
      w_all, head_gains, *tables)


ATTN_Q_BLOCKS = 4


def _band_attn_kernel(q_ref, kp_ref, kc_ref, vp_ref, vc_ref, o_ref, lse_ref, *, span):
    n = pl.program_id(1)
    qi = lax.broadcasted_iota(jnp.int32, (BLK, 2 * BLK), 0)
    kj = lax.broadcasted_iota(jnp.int32, (BLK, 2 * BLK), 1)
    dist = qi + BLK - kj
    band = (dist >= 0) & (dist <= span)
    first_band = band & ((n > 0) | (kj >= BLK))
    lane = lax.broadcasted_iota(jnp.int32, (BLK, V7X_LANES), 1)
    scale = 1.0 / math.sqrt(HEAD_DIM)
    for blk in range(ATTN_Q_BLOCKS):
        rows = slice(blk * BLK, (blk + 1) * BLK)
        lse_tile = jnp.zeros((BLK, V7X_LANES), F32)
        for h in range(HEADS_PER_GROUP):
            cols = slice(h * HEAD_DIM, (h + 1) * HEAD_DIM)
            q = q_ref[rows, cols]
            if blk == 0:
                k = jnp.concatenate([kp_ref[:, cols], kc_ref[rows, cols]], axis=0)
                v = jnp.concatenate([vp_ref[:, cols], vc_ref[rows, cols]], axis=0)
                mask = first_band
            else:
                keys = slice((blk - 1) * BLK, (blk + 1) * BLK)
                k, v, mask = kc_ref[keys, cols], vc_ref[keys, cols], band
            sc = lax.dot_general(q, k, (((1,), (1,)), ((), ())),
                                 preferred_element_type=F32) * scale
            sc = jnp.where(mask, sc, NEG)
            m = jnp.max(sc, axis=-1, keepdims=True)
            e = jnp.exp(sc - m)
            l = jnp.sum(e, axis=-1, keepdims=True)
            p = (e * (1.0 / l)).astype(BF16)
            o_ref[rows, cols] = jnp.dot(p, v, preferred_element_type=F32)
            lse_tile = jnp.where(lane == h, m + jnp.log(l), lse_tile)
        lse_ref[rows, :] = lse_tile


def _band_attn(kvq, group):
    r, rows, _ = kvq.shape
    step_rows = ATTN_Q_BLOCKS * BLK

    def cur(col):
        return pl.BlockSpec((None, step_rows, O_WIDTH), lambda j, n: (j, n, col))

    def prev(col):
        return pl.BlockSpec((None, BLK, O_WIDTH),
                            lambda j, n: (j, jnp.maximum(n * ATTN_Q_BLOCKS - 1, 0), col))

    return pl.pallas_call(
        functools.partial(_band_attn_kernel, span=GROUP_SPANS[group]),
        grid=(r, rows // step_rows),
        in_specs=[cur(Q_KIND), prev(K_KIND), cur(K_KIND), prev(V_KIND), cur(V_KIND)],
        out_specs=[cur(0),
                   pl.BlockSpec((None, step_rows, V7X_LANES), lambda j, n: (j, n, 0))],
        out_shape=[jax.ShapeDtypeStruct((r, rows, O_WIDTH), F32),
                   jax.ShapeDtypeStruct((r, rows, V7X_LANES), F32)],
        compiler_params=_params(2),
        name=f"band_attn_r{r}",
    )(kvq, kvq, kvq, kvq, kvq)


TM_MIX = 1024


def _rows_from_residues(src_ref, dst_ref, *, r, lanes=None):
    n = src_ref.shape[1]
    for res in range(r):
        rows = pl.ds(res, n, stride=r)
        if lanes is None:
            dst_ref[rows, :] = src_ref[res]
        else:
            for slab in range(lanes // V7X_LANES):
                dst_ref[slab, rows, :] = src_ref[res, :, _lane_slab(slab)]


def _mix_wo_kernel(o0_ref, o1_ref, o2_ref, l0_ref, l1_ref, l2_ref, w_ref, x_ref,
                   gate_ref, out_ref, hb_ref, on1_ref, on2_ref, ln1_ref, ln2_ref):
    @pl.when(pl.program_id(1) == 0)
    def _():
        r1, r2 = GROUP_DILATIONS[1], GROUP_DILATIONS[2]
        _rows_from_residues(l1_ref, ln1_ref, r=r1)
        _rows_from_residues(l2_ref, ln2_ref, r=r2)
        _rows_from_residues(o1_ref, on1_ref, r=r1, lanes=O_WIDTH)
        _rows_from_residues(o2_ref, on2_ref, r=r2, lanes=O_WIDTH)
        l0, l1, l2 = l0_ref[0], ln1_ref[...], ln2_ref[...]
        m = jnp.maximum(jnp.maximum(l0, l1), l2)
        e0, e1, e2 = jnp.exp(l0 - m), jnp.exp(l1 - m), jnp.exp(l2 - m)
        inv = 1.0 / (e0 + e1 + e2)
        a0, a1, a2 = e0 * inv, e1 * inv, e2 * inv
        for h in range(HEADS_PER_GROUP):
            cols = _lane_slab(h)
            o = (a0[:, h:h + 1] * o0_ref[0, :, cols] + a1[:, h:h + 1] * on1_ref[h]
                 + a2[:, h:h + 1] * on2_ref[h])
            hb_ref[:, cols] = o.astype(BF16)

    y = jnp.dot(hb_ref[...], w_ref[...], preferred_element_type=F32)
    out_ref[...] = x_ref[...] + gate_ref[...] * y


def _mix_wo(outs, lses, w_o, x, gate):
    s, d = x.shape
    gate_col = gate[1] * (d // TN)

    def planes(width, r):
        return pl.BlockSpec((r, TM_MIX // r, width), lambda i, j: (0, i, 0))

    slabs = pltpu.VMEM((O_WIDTH // V7X_LANES, TM_MIX, V7X_LANES), F32)
    rows = pltpu.VMEM((TM_MIX, V7X_LANES), F32)
    return pl.pallas_call(
        _mix_wo_kernel,
        grid=(s // TM_MIX, d // TN),
        in_specs=[planes(O_WIDTH, r) for r in GROUP_DILATIONS]
        + [planes(V7X_LANES, r) for r in GROUP_DILATIONS]
        + [pl.BlockSpec((O_WIDTH, TN), lambda i, j: (0, j)),
           pl.BlockSpec((TM_MIX, TN), lambda i, j: (i, j)),
           pl.BlockSpec((1, TN), lambda i, j: (0, gate_col + j))],
        out_specs=pl.BlockSpec((TM_MIX, TN), lambda i, j: (i, j)),
        out_shape=jax.ShapeDtypeStruct((s, d), F32),
        scratch_shapes=[pltpu.VMEM((TM_MIX, O_WIDTH), BF16), slabs, slabs, rows, rows],
        compiler_params=_params(2),
        name="mix_wo",
    )(*outs, *lses, w_o, x, gate[0])


def kernel(x, c, positions, mod_w, mod_b, norm_mix_g, norm_ffn_g, conv_pw1_w, conv_pw1_b,
           conv_dw_w, conv_dw_b, conv_ln_g, conv_ln_b, conv_pw2_w, conv_pw2_b, kv_mod_w,
           kv_mod_b, kv_norm_g, w_kv, k_norm_g, w_q, q_norm_g, w_o, ffn_up_w, ffn_dw_w,
           ffn_dw_b, ffn_down_w):
    batch, s, d = x.shape
    assert (batch, s, d) == (1, SEQ, D_MODEL)
    x = x[0]
    c_col = c.reshape(d, 1)
    row = lambda v: v.reshape(1, -1)

    mod = _mod_matvec(c_col, mod_w, mod_b[:, None, :])
    kv_mod = _mod_matvec(c_col, kv_mod_w[None], kv_mod_b[None, None, :])[0]

    mvec = lambda l, q: (mod[l], q)
    ffn_dw_b3 = ffn_dw_b[:, None, :]

    conv_pw1_w, conv_pw2_w, w_o, ffn_up_w, ffn_down_w = (
        w.astype(BF16) for w in (conv_pw1_w, conv_pw2_w, w_o, ffn_up_w, ffn_down_w))
    w_qkv = jnp.concatenate([w_kv.astype(BF16), w_q[0].astype(BF16)], axis=1)

    def ffn(x, l):
        return _conv_ffn(x, row(norm_ffn_g[l]), mvec(l, 3), mvec(l, 4), mvec(l, 5), l,
                         ffn_up_w, ffn_dw_w, ffn_dw_b3, ffn_down_w)

    glu = _pw1_glu(x, row(norm_mix_g[0]), mvec(0, 0), mvec(0, 1),
                   conv_pw1_w[0], row(conv_pw1_b[0]))
    x = _conv_pw2(glu, conv_dw_w[0], row(conv_dw_b[0]), row(conv_ln_g[0]),
                  row(conv_ln_b[0]), conv_pw2_w[0], row(conv_pw2_b[0]), x, mvec(0, 2))
    x = ffn(x, 0)

    tables = _rope_tables(positions[0])
    head_gains = jnp.stack([k_norm_g, q_norm_g[0]])[:, None, :]
    kvq = _qkv(x, row(kv_norm_g), (kv_mod, 0), (kv_mod, 1),
               row(norm_mix_g[1]), mvec(1, 0), mvec(1, 1), w_qkv, head_gains, tables)
    outs, lses = zip(*[_band_attn(kvq[g], g) for g in range(N_GROUPS)])
    x = _mix_wo(outs, lses, w_o[0], x, mvec(1, 2))
    x = ffn(x, 1)
    return x[None]
```

```python
import functools
import math

import jax
import jax.numpy as jnp
from jax import lax
from jax.experimental import pallas as pl
from jax.experimental.pallas import tpu as pltpu

D_MODEL = 2048
SEQ = 8192
CONV_K = 31
FFN_CONV_K = 3
D_FF = 5632
GROUP_DILATIONS = (1, 4, 16)
GROUP_SPANS = (128, 128, 128)
N_GROUPS = 3
HEADS_PER_GROUP = 8
HEAD_DIM = 128
Q_WIDTH = N_GROUPS * HEADS_PER_GROUP * HEAD_DIM
O_WIDTH = HEADS_PER_GROUP * HEAD_DIM
ROT_DIM = HEAD_DIM // 4
ROPE_THETA = 500000.0
BLK = 128
EPS = 1e-6
NEG = -1e30

V7X_LANES = 128
V7X_SUBLANES = 8
V7X_BF16_ROWS_PER_VREG = 16
V7X_VMEM_BYTES = 64 * 1024 * 1024
VMEM_LIMIT = 56 * 1024 * 1024

TM = 1024
TN = 512
TN_GLU = TN
ROW_CHUNK = 64

F32 = jnp.float32
BF16 = jnp.bfloat16


def _params(n_axes, vmem_limit=VMEM_LIMIT):
    return pltpu.CompilerParams(
        dimension_semantics=("arbitrary",) * n_axes,
        vmem_limit_bytes=vmem_limit)


def _sigmoid(x):
    return 1.0 / (1.0 + jnp.exp(-x))


def _silu(x):
    return x * _sigmoid(x)


def _lane_slab(slab):
    return slice(slab * V7X_LANES, (slab + 1) * V7X_LANES)


def _matvec_kernel(c_ref, w_ref, b_ref, o_ref, sb_ref, *, k_dim, tn):
    first = (pl.program_id(0) == 0) & (pl.program_id(1) == 0)

    @pl.when(first)
    def _():
        c = c_ref[...]
        sb_ref[...] = jnp.broadcast_to(_silu(c), (k_dim, V7X_LANES))

    n_groups = tn // V7X_LANES

    def body(t, accs):
        r0 = pl.multiple_of(t * ROW_CHUNK, ROW_CHUNK)
        s = sb_ref[pl.ds(r0, ROW_CHUNK), :]
        w = w_ref[pl.ds(r0, ROW_CHUNK), :]
        new = []
        for g in range(n_groups):
            p = w[:, _lane_slab(g)] * s
            a = accs[g]
            for u in range(ROW_CHUNK // V7X_SUBLANES):
                a = a + p[u * V7X_SUBLANES:(u + 1) * V7X_SUBLANES, :]
            new.append(a)
        return tuple(new)

    init = tuple(jnp.zeros((V7X_SUBLANES, V7X_LANES), F32) for _ in range(n_groups))
    accs = lax.fori_loop(0, k_dim // ROW_CHUNK, body, init)
    row = jnp.concatenate([jnp.sum(a, axis=0, keepdims=True) for a in accs], axis=1)
    o_ref[...] = row + b_ref[...]


def _mod_matvec(c_col, w, b):
    n_l, k_dim, n = w.shape
    tn = 1024
    return pl.pallas_call(
        functools.partial(_matvec_kernel, k_dim=k_dim, tn=tn),
        grid=(n_l, n // tn),
        in_specs=[
            pl.BlockSpec((k_dim, 1), lambda l, j: (0, 0)),
            pl.BlockSpec((None, k_dim, tn), lambda l, j: (l, 0, j)),
            pl.BlockSpec((None, 1, tn), lambda l, j: (l, 0, j)),
        ],
        out_specs=pl.BlockSpec((None, 1, tn), lambda l, j: (l, 0, j)),
        out_shape=jax.ShapeDtypeStruct((n_l, 1, n), F32),
        scratch_shapes=[pltpu.VMEM((k_dim, V7X_LANES), F32)],
        compiler_params=_params(2),
        name="mod_matvec",
    )(c_col, w, b)


MOD_ROWS = V7X_BF16_ROWS_PER_VREG
MOD_UNROLL = 4


def _modulate_rows(x_ref, targets, *, rows, h_row0=0):
    def body(t, carry):
        r0 = pl.multiple_of(t * MOD_ROWS, MOD_ROWS)
        x = x_ref[pl.ds(r0, MOD_ROWS), :]
        ms = jnp.mean(x * x, axis=-1, keepdims=True)
        y = x * lax.rsqrt(ms + EPS)
        h0 = pl.multiple_of(h_row0 + t * MOD_ROWS, MOD_ROWS)
        for g_ref, shift_ref, scale_ref, h_ref in targets:
            h = (y * g_ref[...]) * (1.0 + scale_ref[...]) + shift_ref[...]
            h_ref[pl.ds(h0, MOD_ROWS), :] = h.astype(BF16)
        return carry

    trips = rows // MOD_ROWS
    lax.fori_loop(0, trips, body, 0, unroll=min(MOD_UNROLL, trips))


def _pw1_glu_kernel(x_ref, g_ref, sh_ref, sc_ref, wa_ref, wg_ref, ba_ref, bg_ref,
                    o_ref, h_ref):
    @pl.when(pl.program_id(1) == 0)
    def _():
        _modulate_rows(x_ref, [(g_ref, sh_ref, sc_ref, h_ref)], rows=TM)

    w = jnp.concatenate([wa_ref[...], wg_ref[...]], axis=1)
    u = jnp.dot(h_ref[...], w, preferred_element_type=F32)
    a = u[:, :TN_GLU] + ba_ref[...]
    gt = u[:, TN_GLU:] + bg_ref[...]
    o_ref[...] = a * _sigmoid(gt)


def _pw1_glu(x, norm_g, shift, scale, w, b):
    s, d = x.shape
    nj = d // TN_GLU
    vec = lambda col: pl.BlockSpec((1, d), lambda i, j: (0, col))
    return pl.pallas_call(
        _pw1_glu_kernel,
        grid=(s // TM, nj),
        in_specs=[
            pl.BlockSpec((TM, d), lambda i, j: (i, 0)),
            pl.BlockSpec((1, d), lambda i, j: (0, 0)),
            vec(shift[1]), vec(scale[1]),
            pl.BlockSpec((d, TN_GLU), lambda i, j: (0, j)),
            pl.BlockSpec((d, TN_GLU), lambda i, j: (0, nj + j)),
            pl.BlockSpec((1, TN_GLU), lambda i, j: (0, j)),
            pl.BlockSpec((1, TN_GLU), lambda i, j: (0, nj + j)),
        ],
        out_specs=pl.BlockSpec((TM, TN_GLU), lambda i, j: (i, j)),
        out_shape=jax.ShapeDtypeStruct((s, d), F32),
        scratch_shapes=[pltpu.VMEM((TM, d), BF16)],
        compiler_params=_params(2),
        name="pw1_glu",
    )(x, norm_g, shift[0], scale[0], w, w, b, b)


CONV_HALO = 32
CONV_ROWS = 32
CONV_TILE = 256


def _dwconv_ln_chunk(r0, gbuf_ref, dw_ref, db_ref, lg_ref, lb_ref, tmp_ref, h_ref, *, d):
    off = CONV_HALO - (CONV_K - 1)
    for slab in range(d // V7X_LANES):
        lanes = _lane_slab(slab)
        acc = jnp.broadcast_to(db_ref[:, lanes], (CONV_ROWS, V7X_LANES))
        for k in range(CONV_K):
            tap = gbuf_ref[slab, pl.ds(r0 + off + k, CONV_ROWS, stride=1), :]
            acc = acc + tap * dw_ref[k:k + 1, lanes]
        tmp_ref[:, lanes] = acc
    u = tmp_ref[...]
    mu = jnp.mean(u, axis=-1, keepdims=True)
    uc = u - mu
    var = jnp.mean(uc * uc, axis=-1, keepdims=True)
    y = uc * lax.rsqrt(var + EPS) * lg_ref[...] + lb_ref[...]
    h_ref[pl.ds(r0, CONV_ROWS), :] = _silu(y).astype(BF16)


def _conv_pw2_kernel(g_ref, halo_ref, dw_ref, db_ref, lg_ref, lb_ref, w_ref, b_ref,
                     x_ref, gate_ref, o_ref, gbuf_ref, tmp_ref, hnew_ref, hprev_ref, *, d):
    s = pl.program_id(0)

    @pl.when(s == 0)
    def _():
        hnew_ref[...] = jnp.zeros(hnew_ref.shape, BF16)

    hprev_ref[...] = hnew_ref[...]
    y = jnp.dot(hprev_ref[...], w_ref[...], preferred_element_type=F32)
    o_ref[...] = x_ref[...] + gate_ref[...] * (y + b_ref[...])

    for slab in range(d // V7X_LANES):
        lanes = _lane_slab(slab)
        halo = halo_ref[:, lanes]
        gbuf_ref[slab, 0:CONV_HALO, :] = jnp.where(s == 0, jnp.zeros_like(halo), halo)
        gbuf_ref[slab, CONV_HALO:, :] = g_ref[:, lanes]
    for c in range(CONV_TILE // CONV_ROWS):
        _dwconv_ln_chunk(c * CONV_ROWS, gbuf_ref, dw_ref, db_ref, lg_ref, lb_ref,
                         tmp_ref.at[c % 2], hnew_ref, d=d)


def _conv_pw2(glu, dw_w, dw_b, ln_g, ln_b, w, b, x, gate):
    s, d = glu.shape
    n_tiles = s // CONV_TILE
    halo_blocks_per_tile = CONV_TILE // CONV_HALO
    conv_tile = lambda j: jnp.minimum(j, n_tiles - 1)
    done_tile = lambda j: jnp.maximum(j - 1, 0)
    row_vec = pl.BlockSpec((1, d), lambda j: (0, 0))
    return pl.pallas_call(
        functools.partial(_conv_pw2_kernel, d=d),
        grid=(n_tiles + 1,),
        in_specs=[
            pl.BlockSpec((CONV_TILE, d), lambda j: (conv_tile(j), 0)),
            pl.BlockSpec((CONV_HALO, d),
                         lambda j: (jnp.maximum(conv_tile(j) * halo_blocks_per_tile - 1, 0), 0)),
            pl.BlockSpec((CONV_K, d), lambda j: (0, 0)),
            row_vec, row_vec, row_vec,
            pl.BlockSpec((d, d), lambda j: (0, 0), pipeline_mode=pl.Buffered(1)),
            row_vec,
            pl.BlockSpec((CONV_TILE, d), lambda j: (done_tile(j), 0)),
            pl.BlockSpec((1, d), lambda j: (0, gate[1])),
        ],
        out_specs=pl.BlockSpec((CONV_TILE, d), lambda j: (done_tile(j), 0)),
        out_shape=jax.ShapeDtypeStruct((s, d), F32),
        scratch_shapes=[
            pltpu.VMEM((d // V7X_LANES, CONV_TILE + CONV_HALO, V7X_LANES), F32),
            pltpu.VMEM((2, CONV_ROWS, d), F32),
            pltpu.VMEM((CONV_TILE, d), BF16),
            pltpu.VMEM((CONV_TILE, d), BF16),
        ],
        compiler_params=_params(1),
        name="dwconv_pw2",
    )(glu, glu, dw_w, dw_b, ln_g, ln_b, w, b, x, gate[0])


FFN_HALO = V7X_BF16_ROWS_PER_VREG
TF = 512
FFN_OUT_LANES = 512
FFN_VMEM_LIMIT = 60 * 1024 * 1024


def _ffn_kernel(x_ref, xh_ref, g_ref, sh_ref, sc_ref, gate_ref, wg_ref, wv_ref,
                dw_ref, db_ref, wd_ref, o_ref, h_ref, u_ref):
    i = pl.program_id(0)

    @pl.when(pl.program_id(1) == 0)
    def _():
        target = [(g_ref, sh_ref, sc_ref, h_ref)]
        _modulate_rows(xh_ref, target, rows=FFN_HALO)
        _modulate_rows(x_ref, target, rows=TM, h_row0=FFN_HALO)
        o_ref[...] = x_ref[...]

    w_up = jnp.concatenate([wg_ref[...], wv_ref[...]], axis=1)
    u = jnp.dot(h_ref[...], w_up, preferred_element_type=F32)
    u_ref[...] = u
    keep_halo = jnp.where(i == 0, 0.0, 1.0)
    u_ref[0:FFN_HALO, 0:TF] = u[0:FFN_HALO, 0:TF] * keep_halo

    gt = db_ref[...] + dw_ref[2:3, :] * u_ref[pl.ds(FFN_HALO, TM), 0:TF]
    gt = gt + dw_ref[1:2, :] * u_ref[pl.ds(FFN_HALO - 1, TM), 0:TF]
    gt = gt + dw_ref[0:1, :] * u_ref[pl.ds(FFN_HALO - 2, TM), 0:TF]
    val = u_ref[pl.ds(FFN_HALO, TM), TF:2 * TF]
    act = (_silu(gt) * val).astype(BF16)
    for c0 in range(0, o_ref.shape[1], FFN_OUT_LANES):
        cols = slice(c0, c0 + FFN_OUT_LANES)
        y = jnp.dot(act, wd_ref[:, cols], preferred_element_type=F32)
        o_ref[:, cols] += gate_ref[:, cols] * y


def _conv_ffn(x, norm_g, shift, scale, gate, layer, up_w, dw_w, dw_b, down_w):
    s, d = x.shape
    f = down_w.shape[1]
    nf = f // TF
    halo_blocks_per_tile = TM // FFN_HALO
    vec = lambda col: pl.BlockSpec((1, d), lambda i, j: (0, col))
    return pl.pallas_call(
        _ffn_kernel,
        grid=(s // TM, nf),
        in_specs=[
            pl.BlockSpec((TM, d), lambda i, j: (i, 0)),
            pl.BlockSpec((FFN_HALO, d),
                         lambda i, j: (jnp.maximum(i * halo_blocks_per_tile - 1, 0), 0)),
            pl.BlockSpec((1, d), lambda i, j: (0, 0)),
            vec(shift[1]), vec(scale[1]), vec(gate[1]),
            pl.BlockSpec((None, d, TF), lambda i, j: (layer, 0, j)),
            pl.BlockSpec((None, d, TF), lambda i, j: (layer, 0, nf + j)),
            pl.BlockSpec((None, FFN_CONV_K, TF), lambda i, j: (layer, 0, j)),
            pl.BlockSpec((None, 1, TF), lambda i, j: (layer, 0, j)),
            pl.BlockSpec((None, TF, d), lambda i, j: (layer, j, 0)),
        ],
        out_specs=pl.BlockSpec((TM, d), lambda i, j: (i, 0)),
        out_shape=jax.ShapeDtypeStruct((s, d), F32),
        scratch_shapes=[
            pltpu.VMEM((TM + FFN_HALO, d), BF16),
            pltpu.VMEM((TM + FFN_HALO, 2 * TF), F32),
        ],
        compiler_params=_params(2, FFN_VMEM_LIMIT),
        name="conv_ffn",
    )(x, x, norm_g, shift[0], scale[0], gate[0], up_w, up_w, dw_w, dw_b, down_w)


def _rope_table_kernel(pos_ref, freq_ref, cos_ref, sin_lo_ref, sin_hi_ref):
    pos = pos_ref[...].astype(F32)
    ang = pos * freq_ref[...]
    lane = lax.broadcasted_iota(jnp.int32, ang.shape, 1)
    c = jnp.cos(ang)
    sn = jnp.sin(ang)
    half = ROT_DIM // 2
    cos_ref[...] = jnp.where(lane < ROT_DIM, c, 1.0)
    sin_lo_ref[...] = jnp.where(lane < half, -sn, 0.0)
    sin_hi_ref[...] = jnp.where((lane >= half) & (lane < ROT_DIM), sn, 0.0)


def _rope_tables(positions):
    s = positions.shape[0]
    rows = 1024
    inv_freq = ROPE_THETA ** (-jnp.arange(0, ROT_DIM, 2, dtype=F32) / ROT_DIM)
    lane_freq = jnp.concatenate(
        [inv_freq, inv_freq, jnp.zeros((HEAD_DIM - ROT_DIM,), F32)])[None, :]
    out = jax.ShapeDtypeStruct((s, HEAD_DIM), F32)
    spec = pl.BlockSpec((rows, HEAD_DIM), lambda i: (i, 0))
    return pl.pallas_call(
        _rope_table_kernel,
        grid=(s // rows,),
        in_specs=[pl.BlockSpec((rows, 1), lambda i: (i, 0)),
                  pl.BlockSpec((1, HEAD_DIM), lambda i: (0, 0))],
        out_specs=[spec, spec, spec],
        out_shape=[out, out, out],
        compiler_params=_params(1),
        name="rope_tables",
    )(positions.reshape(s, 1), lane_freq)


TILES_PER_GROUP = O_WIDTH // TN


def _norm_rope_head(q, hg, cos, sin_lo, sin_hi):
    ms = jnp.mean(q * q, axis=-1, keepdims=True)
    qn = q * lax.rsqrt(ms + EPS) * hg
    hi_to_lo = pltpu.roll(qn, HEAD_DIM - ROT_DIM // 2, axis=1)
    lo_to_hi = pltpu.roll(qn, ROT_DIM // 2, axis=1)
    return qn * cos + hi_to_lo * sin_lo + lo_to_hi * sin_hi


SPLIT_STRIDE = 4


def _finish_tile(y_ref, out_ref, head_gain, use_rope, table_refs, y4_ref, *, r):
    rows = TM // r
    two_pass = r > SPLIT_STRIDE
    r_outer = r // SPLIT_STRIDE
    for head in range(TN // HEAD_DIM):
        cos, sin_lo, sin_hi = (t[...] for t in table_refs)
        y = y_ref[head]
        y_ref[head] = jnp.where(use_rope,
                                _norm_rope_head(y, head_gain, cos, sin_lo, sin_hi), y)
        if two_pass:
            for p in range(SPLIT_STRIDE):
                y4_ref[head % 2, p] = y_ref[head, pl.ds(p, TM // SPLIT_STRIDE,
                                                        stride=SPLIT_STRIDE), :]
        for res in range(r):
            if r == 1:
                y = y_ref[head]
            elif two_pass:
                p, q = res % SPLIT_STRIDE, res // SPLIT_STRIDE
                y = y4_ref[head % 2, p, pl.ds(q, rows, stride=r_outer), :]
            else:
                y = y_ref[head, pl.ds(res, rows, stride=r), :]
            out_ref[res, :, _lane_slab(head)] = y.astype(BF16)


N_KINDS = 3
K_KIND, V_KIND, Q_KIND = range(N_KINDS)
GROUP_TILES = N_KINDS * TILES_PER_GROUP
ROW_TILES = N_GROUPS * GROUP_TILES


def _qkv_kernel(x_ref, gkv_ref, shkv_ref, sckv_ref, gq_ref, shq_ref, scq_ref, w_ref,
                hg_ref, cos_ref, slo_ref, shi_ref, *refs):
    outs = refs[:N_GROUPS]
    h_ref, ya_ref, yb_ref, y4_ref = refs[N_GROUPS:]
    s = pl.program_id(0)
    t = TILES_PER_GROUP
    tile = jnp.minimum(s, pl.num_programs(0) - 2)
    col = tile % ROW_TILES
    done_col = jnp.maximum(s - 1, 0) % ROW_TILES
    done_kind = (done_col % GROUP_TILES) // t
    done_group = done_col // GROUP_TILES
    tables = (cos_ref, slo_ref, shi_ref)

    @pl.when(col == 0)
    def _():
        _modulate_rows(x_ref, [(gkv_ref, shkv_ref, sckv_ref, h_ref.at[0]),
                               (gq_ref, shq_ref, scq_ref, h_ref.at[1])], rows=TM)

    @pl.when(s == 0)
    def _():
        ya_ref[...] = jnp.zeros(ya_ref.shape, F32)

    h_sel = ((col % GROUP_TILES) // t == Q_KIND).astype(jnp.int32)
    gain = hg_ref[(done_kind == Q_KIND).astype(jnp.int32)]
    use_rope = done_kind != V_KIND
    for group, r in enumerate(GROUP_DILATIONS):
        @pl.when(done_group == group)
        def _(group=group, r=r):
            yb_ref[...] = ya_ref[...]
            y = jnp.dot(h_ref[h_sel], w_ref[...], preferred_element_type=F32)
            for head in range(TN // HEAD_DIM):
                ya_ref[head] = y[:, _lane_slab(head)]
            _finish_tile(yb_ref, outs[group], gain, use_rope, tables, y4_ref, r=r)


def _qkv(x, kv_norm_g, kv_shift, kv_scale, q_norm_g, q_shift, q_scale, w_all, head_gains,
         tables):
    s, d = x.shape
    t = TILES_PER_GROUP
    n_tiles = (s // TM) * ROW_TILES
    vec = lambda col: pl.BlockSpec((1, d), lambda j: (0, col))
    one = pl.BlockSpec((1, d), lambda j: (0, 0))

    def tile_of(j):
        return jnp.minimum(j, n_tiles - 1)

    def done_of(j):
        return jnp.maximum(j - 1, 0)

    def w_col(j):
        col = tile_of(j) % ROW_TILES
        group, rem = col // GROUP_TILES, col % GROUP_TILES
        return (rem // t) * (N_GROUPS * t) + group * t + rem % t

    def out_spec(group):
        r = GROUP_DILATIONS[group]
        return pl.BlockSpec(
            (r, TM // r, TN),
            lambda j: (0, done_of(j) // ROW_TILES,
                       jnp.clip(done_of(j) % ROW_TILES - group * GROUP_TILES,
                                0, GROUP_TILES - 1)))

    tab = pl.BlockSpec((TM, HEAD_DIM), lambda j: (done_of(j) // ROW_TILES, 0))
    return pl.pallas_call(
        _qkv_kernel,
        grid=(n_tiles + 1,),
        in_specs=[
            pl.BlockSpec((TM, d), lambda j: (tile_of(j) // ROW_TILES, 0)),
            one, vec(kv_shift[1]), vec(kv_scale[1]),
            one, vec(q_shift[1]), vec(q_scale[1]),
            pl.BlockSpec((d, TN), lambda j: (0, w_col(j))),
            pl.BlockSpec((2, 1, HEAD_DIM), lambda j: (0, 0, 0)),
            tab, tab, tab,
        ],
        out_specs=[out_spec(g) for g in range(N_GROUPS)],
        out_shape=[jax.ShapeDtypeStruct((r, s // r, N_KINDS * O_WIDTH), BF16)
                   for r in GROUP_DILATIONS],
        scratch_shapes=[
            pltpu.VMEM((2, TM, d), BF16),
            pltpu.VMEM((TN // V7X_LANES, TM, V7X_LANES), F32),
            pltpu.VMEM((TN // V7X_LANES, TM, V7X_LANES), F32),
            pltpu.VMEM((2, SPLIT_STRIDE, TM // SPLIT_STRIDE, V7X_LANES), F32),
        ],
        compiler_params=_params(1),
        name="qkv",
    )(x, kv_norm_g, kv_shift[0], kv_scale[0], q_norm_g, q_shift[0], q_scale[0],
      w_all, head_gains, *tables)


ATTN_Q_BLOCKS = 4


def _band_attn_kernel(q_ref, kp_ref, kc_ref, vp_ref, vc_ref, o_ref, lse_ref, *, span):
    n = pl.program_id(1)
    qi = lax.broadcasted_iota(jnp.int32, (BLK, 2 * BLK), 0)
    kj = lax.broadcasted_iota(jnp.int32, (BLK, 2 * BLK), 1)
    dist = qi + BLK - kj
    band = (dist >= 0) & (dist <= span)
    first_band = band & ((n > 0) | (kj >= BLK))
    lane = lax.broadcasted_iota(jnp.int32, (BLK, V7X_LANES), 1)
    scale = 1.0 / math.sqrt(HEAD_DIM)
    for blk in range(ATTN_Q_BLOCKS):
        rows = slice(blk * BLK, (blk + 1) * BLK)
        lse_tile = jnp.zeros((BLK, V7X_LANES), F32)
        for h in range(HEADS_PER_GROUP):
            cols = slice(h * HEAD_DIM, (h + 1) * HEAD_DIM)
            q = q_ref[rows, cols]
            if blk == 0:
                k = jnp.concatenate([kp_ref[:, cols], kc_ref[rows, cols]], axis=0)
                v = jnp.concatenate([vp_ref[:, cols], vc_ref[rows, cols]], axis=0)
                mask = first_band
            else:
                keys = slice((blk - 1) * BLK, (blk + 1) * BLK)
                k, v, mask = kc_ref[keys, cols], vc_ref[keys, cols], band
            sc = lax.dot_general(q, k, (((1,), (1,)), ((), ())),
                                 preferred_element_type=F32) * scale
            sc = jnp.where(mask, sc, NEG)
            m = jnp.max(sc, axis=-1, keepdims=True)
            e = jnp.exp(sc - m)
            l = jnp.sum(e, axis=-1, keepdims=True)
            p = (e * (1.0 / l)).astype(BF16)
            o_ref[rows, cols] = jnp.dot(p, v, preferred_element_type=F32)
            lse_tile = jnp.where(lane == h, m + jnp.log(l), lse_tile)
        lse_ref[rows, :] = lse_tile


def _band_attn(kvq, group):
    r, rows, _ = kvq.shape
    step_rows = ATTN_Q_BLOCKS * BLK

    def cur(col):
        return pl.BlockSpec((None, step_rows, O_WIDTH), lambda j, n: (j, n, col))

    def prev(col):
        return pl.BlockSpec((None, BLK, O_WIDTH),
                            lambda j, n: (j, jnp.maximum(n * ATTN_Q_BLOCKS - 1, 0), col))

    return pl.pallas_call(
        functools.partial(_band_attn_kernel, span=GROUP_SPANS[group]),
        grid=(r, rows // step_rows),
        in_specs=[cur(Q_KIND), prev(K_KIND), cur(K_KIND), prev(V_KIND), cur(V_KIND)],
        out_specs=[cur(0),
                   pl.BlockSpec((None, step_rows, V7X_LANES), lambda j, n: (j, n, 0))],
        out_shape=[jax.ShapeDtypeStruct((r, rows, O_WIDTH), F32),
                   jax.ShapeDtypeStruct((r, rows, V7X_LANES), F32)],
        compiler_params=_params(2),
        name=f"band_attn_r{r}",
    )(kvq, kvq, kvq, kvq, kvq)


TM_MIX = 1024


def _rows_from_residues(src_ref, dst_ref, *, r, lanes=None):
    n = src_ref.shape[1]
    for res in range(r):
        rows = pl.ds(res, n, stride=r)
        if lanes is None:
            dst_ref[rows, :] = src_ref[res]
        else:
            for slab in range(lanes // V7X_LANES):
                dst_ref[slab, rows, :] = src_ref[res, :, _lane_slab(slab)]


def _mix_wo_kernel(o0_ref, o1_ref, o2_ref, l0_ref, l1_ref, l2_ref, w_ref, x_ref,
                   gate_ref, out_ref, hb_ref, on1_ref, on2_ref, ln1_ref, ln2_ref):
    @pl.when(pl.program_id(1) == 0)
    def _():
        r1, r2 = GROUP_DILATIONS[1], GROUP_DILATIONS[2]
        _rows_from_residues(l1_ref, ln1_ref, r=r1)
        _rows_from_residues(l2_ref, ln2_ref, r=r2)
        _rows_from_residues(o1_ref, on1_ref, r=r1, lanes=O_WIDTH)
        _rows_from_residues(o2_ref, on2_ref, r=r2, lanes=O_WIDTH)
        l0, l1, l2 = l0_ref[0], ln1_ref[...], ln2_ref[...]
        m = jnp.maximum(jnp.maximum(l0, l1), l2)
        e0, e1, e2 = jnp.exp(l0 - m), jnp.exp(l1 - m), jnp.exp(l2 - m)
        inv = 1.0 / (e0 + e1 + e2)
        a0, a1, a2 = e0 * inv, e1 * inv, e2 * inv
        for h in range(HEADS_PER_GROUP):
            cols = _lane_slab(h)
            o = (a0[:, h:h + 1] * o0_ref[0, :, cols] + a1[:, h:h + 1] * on1_ref[h]
                 + a2[:, h:h + 1] * on2_ref[h])
            hb_ref[:, cols] = o.astype(BF16)

    y = jnp.dot(hb_ref[...], w_ref[...], preferred_element_type=F32)
    out_ref[...] = x_ref[...] + gate_ref[...] * y


def _mix_wo(outs, lses, w_o, x, gate):
    s, d = x.shape
    gate_col = gate[1] * (d // TN)

    def planes(width, r):
        return pl.BlockSpec((r, TM_MIX // r, width), lambda i, j: (0, i, 0))

    slabs = pltpu.VMEM((O_WIDTH // V7X_LANES, TM_MIX, V7X_LANES), F32)
    rows = pltpu.VMEM((TM_MIX, V7X_LANES), F32)
    return pl.pallas_call(
        _mix_wo_kernel,
        grid=(s // TM_MIX, d // TN),
        in_specs=[planes(O_WIDTH, r) for r in GROUP_DILATIONS]
        + [planes(V7X_LANES, r) for r in GROUP_DILATIONS]
        + [pl.BlockSpec((O_WIDTH, TN), lambda i, j: (0, j)),
           pl.BlockSpec((TM_MIX, TN), lambda i, j: (i, j)),
           pl.BlockSpec((1, TN), lambda i, j: (0, gate_col + j))],
        out_specs=pl.BlockSpec((TM_MIX, TN), lambda i, j: (i, j)),
        out_shape=jax.ShapeDtypeStruct((s, d), F32),
        scratch_shapes=[pltpu.VMEM((TM_MIX, O_WIDTH), BF16), slabs, slabs, rows, rows],
        compiler_params=_params(2),
        name="mix_wo",
    )(*outs, *lses, w_o, x, gate[0])


def kernel(x, c, positions, mod_w, mod_b, norm_mix_g, norm_ffn_g, conv_pw1_w, conv_pw1_b,
           conv_dw_w, conv_dw_b, conv_ln_g, conv_ln_b, conv_pw2_w, conv_pw2_b, kv_mod_w,
           kv_mod_b, kv_norm_g, w_kv, k_norm_g, w_q, q_norm_g, w_o, ffn_up_w, ffn_dw_w,
           ffn_dw_b, ffn_down_w):
    batch, s, d = x.shape
    assert (batch, s, d) == (1, SEQ, D_MODEL)
    x = x[0]
    c_col = c.reshape(d, 1)
    row = lambda v: v.reshape(1, -1)

    mod = _mod_matvec(c_col, mod_w, mod_b[:, None, :])
    kv_mod = _mod_matvec(c_col, kv_mod_w[None], kv_mod_b[None, None, :])[0]

    mvec = lambda l, q: (mod[l], q)
    ffn_dw_b3 = ffn_dw_b[:, None, :]

    conv_pw1_w, conv_pw2_w, w_o, ffn_up_w, ffn_down_w = (
        w.astype(BF16) for w in (conv_pw1_w, conv_pw2_w, w_o, ffn_up_w, ffn_down_w))
    w_qkv = jnp.concatenate([w_kv.astype(BF16), w_q[0].astype(BF16)], axis=1)

    def ffn(x, l):
        return _conv_ffn(x, row(norm_ffn_g[l]), mvec(l, 3), mvec(l, 4), mvec(l, 5), l,
                         ffn_up_w, ffn_dw_w, ffn_dw_b3, ffn_down_w)

    glu = _pw1_glu(x, row(norm_mix_g[0]), mvec(0, 0), mvec(0, 1),
                   conv_pw1_w[0], row(conv_pw1_b[0]))
    x = _conv_pw2(glu, conv_dw_w[0], row(conv_dw_b[0]), row(conv_ln_g[0]),
                  row(conv_ln_b[0]), conv_pw2_w[0], row(conv_pw2_b[0]), x, mvec(0, 2))
    x = ffn(x, 0)

    tables = _rope_tables(positions[0])
    head_gains = jnp.stack([k_norm_g, q_norm_g[0]])[:, None, :]
    kvq = _qkv(x, row(kv_norm_g), (kv_mod, 0), (kv_mod, 1),
               row(norm_mix_g[1]), mvec(1, 0), mvec(1, 1), w_qkv, head_gains, tables)
    outs, lses = zip(*[_band_attn(kvq[g], g) for g in range(N_GROUPS)])
    x = _mix_wo(outs, lses, w_o[0], x, mvec(1, 2))
    x = ffn(x, 1)
    return x[None]
```

```python
import functools
import math

import jax
import jax.numpy as jnp
from jax import lax
from jax.experimental import pallas as pl
from jax.experimental.pallas import tpu as pltpu

D_MODEL = 2048
SEQ = 8192
CONV_K = 31
FFN_CONV_K = 3
D_FF = 5632
GROUP_DILATIONS = (1, 4, 16)
GROUP_SPANS = (128, 128, 128)
N_GROUPS = 3
HEADS_PER_GROUP = 8
HEAD_DIM = 128
Q_WIDTH = N_GROUPS * HEADS_PER_GROUP * HEAD_DIM
O_WIDTH = HEADS_PER_GROUP * HEAD_DIM
ROT_DIM = HEAD_DIM // 4
ROPE_THETA = 500000.0
BLK = 128
EPS = 1e-6
NEG = -1e30

V7X_LANES = 128
V7X_SUBLANES = 8
V7X_BF16_ROWS_PER_VREG = 16
V7X_VMEM_BYTES = 64 * 1024 * 1024
VMEM_LIMIT = 56 * 1024 * 1024

TM = 1024
TN = 512
TN_GLU = TN
ROW_CHUNK = 64

F32 = jnp.float32
BF16 = jnp.bfloat16


def _params(n_axes, vmem_limit=VMEM_LIMIT):
    return pltpu.CompilerParams(
        dimension_semantics=("arbitrary",) * n_axes,
        vmem_limit_bytes=vmem_limit)


def _sigmoid(x):
    return 1.0 / (1.0 + jnp.exp(-x))


def _silu(x):
    return x * _sigmoid(x)


def _lane_slab(slab):
    return slice(slab * V7X_LANES, (slab + 1) * V7X_LANES)


def _matvec_kernel(c_ref, w_ref, b_ref, o_ref, sb_ref, *, k_dim, tn):
    first = (pl.program_id(0) == 0) & (pl.program_id(1) == 0)

    @pl.when(first)
    def _():
        c = c_ref[...]
        sb_ref[...] = jnp.broadcast_to(_silu(c), (k_dim, V7X_LANES))

    n_groups = tn // V7X_LANES

    def body(t, accs):
        r0 = pl.multiple_of(t * ROW_CHUNK, ROW_CHUNK)
        s = sb_ref[pl.ds(r0, ROW_CHUNK), :]
        w = w_ref[pl.ds(r0, ROW_CHUNK), :]
        new = []
        for g in range(n_groups):
            p = w[:, _lane_slab(g)] * s
            a = accs[g]
            for u in range(ROW_CHUNK // V7X_SUBLANES):
                a = a + p[u * V7X_SUBLANES:(u + 1) * V7X_SUBLANES, :]
            new.append(a)
        return tuple(new)

    init = tuple(jnp.zeros((V7X_SUBLANES, V7X_LANES), F32) for _ in range(n_groups))
    accs = lax.fori_loop(0, k_dim // ROW_CHUNK, body, init)
    row = jnp.concatenate([jnp.sum(a, axis=0, keepdims=True) for a in accs], axis=1)
    o_ref[...] = row + b_ref[...]


def _mod_matvec(c_col, w, b):
    n_l, k_dim, n = w.shape
    tn = 1024
    return pl.pallas_call(
        functools.partial(_matvec_kernel, k_dim=k_dim, tn=tn),
        grid=(n_l, n // tn),
        in_specs=[
            pl.BlockSpec((k_dim, 1), lambda l, j: (0, 0)),
            pl.BlockSpec((None, k_dim, tn), lambda l, j: (l, 0, j)),
            pl.BlockSpec((None, 1, tn), lambda l, j: (l, 0, j)),
        ],
        out_specs=pl.BlockSpec((None, 1, tn), lambda l, j: (l, 0, j)),
        out_shape=jax.ShapeDtypeStruct((n_l, 1, n), F32),
        scratch_shapes=[pltpu.VMEM((k_dim, V7X_LANES), F32)],
        compiler_params=_params(2),
        name="mod_matvec",
    )(c_col, w, b)


MOD_ROWS = V7X_BF16_ROWS_PER_VREG
MOD_UNROLL = 4


def _modulate_rows(x_ref, targets, *, rows, h_row0=0, inline=False):
    def chunk(r0, h0):
        x = x_ref[pl.ds(r0, MOD_ROWS), :]
        ms = jnp.mean(x * x, axis=-1, keepdims=True)
        y = x * lax.rsqrt(ms + EPS)
        for g_ref, shift_ref, scale_ref, h_ref in targets:
            h = (y * g_ref[...]) * (1.0 + scale_ref[...]) + shift_ref[...]
            h_ref[pl.ds(h0, MOD_ROWS), :] = h.astype(BF16)

    trips = rows // MOD_ROWS
    if inline:
        for t in range(trips):
            chunk(t * MOD_ROWS, h_row0 + t * MOD_ROWS)
        return

    def body(t, carry):
        chunk(pl.multiple_of(t * MOD_ROWS, MOD_ROWS),
              pl.multiple_of(h_row0 + t * MOD_ROWS, MOD_ROWS))
        return carry

    lax.fori_loop(0, trips, body, 0, unroll=min(MOD_UNROLL, trips))


PW1_TILE = 256


def _pw1_glu_kernel(x_ref, g_ref, sh_ref, sc_ref, w_ref, b_ref, o_ref, hnew_ref, hprev_ref):
    s = pl.program_id(0)
    d = o_ref.shape[1]

    @pl.when(s == 0)
    def _():
        hnew_ref[...] = jnp.zeros(hnew_ref.shape, BF16)

    hprev_ref[...] = hnew_ref[...]
    h = hprev_ref[...]
    for c0 in range(0, d, TN_GLU):
        val, gate = slice(c0, c0 + TN_GLU), slice(d + c0, d + c0 + TN_GLU)
        a = jnp.dot(h, w_ref[:, val], preferred_element_type=F32) + b_ref[:, val]
        gt = jnp.dot(h, w_ref[:, gate], preferred_element_type=F32) + b_ref[:, gate]
        o_ref[:, val] = a * _sigmoid(gt)
    _modulate_rows(x_ref, [(g_ref, sh_ref, sc_ref, hnew_ref)], rows=PW1_TILE, inline=True)


def _pw1_glu(x, norm_g, shift, scale, w, b):
    s, d = x.shape
    n_tiles = s // PW1_TILE
    vec = lambda col: pl.BlockSpec((1, d), lambda j: (0, col))
    return pl.pallas_call(
        _pw1_glu_kernel,
        grid=(n_tiles + 1,),
        in_specs=[
            pl.BlockSpec((PW1_TILE, d), lambda j: (jnp.minimum(j, n_tiles - 1), 0)),
            pl.BlockSpec((1, d), lambda j: (0, 0)),
            vec(shift[1]), vec(scale[1]),
            pl.BlockSpec((d, 2 * d), lambda j: (0, 0), pipeline_mode=pl.Buffered(1)),
            pl.BlockSpec((1, 2 * d), lambda j: (0, 0)),
        ],
        out_specs=pl.BlockSpec((PW1_TILE, d), lambda j: (jnp.maximum(j - 1, 0), 0)),
        out_shape=jax.ShapeDtypeStruct((s, d), F32),
        scratch_shapes=[pltpu.VMEM((PW1_TILE, d), BF16), pltpu.VMEM((PW1_TILE, d), BF16)],
        compiler_params=_params(1),
        name="pw1_glu",
    )(x, norm_g, shift[0], scale[0], w, b)


CONV_HALO = 32
CONV_ROWS = 32
CONV_TILE = 256


def _dwconv_ln_chunk(r0, gbuf_ref, dw_ref, db_ref, lg_ref, lb_ref, tmp_ref, h_ref, *, d):
    off = CONV_HALO - (CONV_K - 1)
    for slab in range(d // V7X_LANES):
        lanes = _lane_slab(slab)
        acc = jnp.broadcast_to(db_ref[:, lanes], (CONV_ROWS, V7X_LANES))
        for k in range(CONV_K):
            tap = gbuf_ref[slab, pl.ds(r0 + off + k, CONV_ROWS, stride=1), :]
            acc = acc + tap * dw_ref[k:k + 1, lanes]
        tmp_ref[:, lanes] = acc
    u = tmp_ref[...]
    mu = jnp.mean(u, axis=-1, keepdims=True)
    uc = u - mu
    var = jnp.mean(uc * uc, axis=-1, keepdims=True)
    y = uc * lax.rsqrt(var + EPS) * lg_ref[...] + lb_ref[...]
    h_ref[pl.ds(r0, CONV_ROWS), :] = _silu(y).astype(BF16)


def _conv_pw2_kernel(g_ref, halo_ref, dw_ref, db_ref, lg_ref, lb_ref, w_ref, b_ref,
                     x_ref, gate_ref, o_ref, gbuf_ref, tmp_ref, hnew_ref, hprev_ref, *, d):
    s = pl.program_id(0)

    @pl.when(s == 0)
    def _():
        hnew_ref[...] = jnp.zeros(hnew_ref.shape, BF16)

    hprev_ref[...] = hnew_ref[...]
    y = jnp.dot(hprev_ref[...], w_ref[...], preferred_element_type=F32)
    o_ref[...] = x_ref[...] + gate_ref[...] * (y + b_ref[...])

    for slab in range(d // V7X_LANES):
        lanes = _lane_slab(slab)
        halo = halo_ref[:, lanes]
        gbuf_ref[slab, 0:CONV_HALO, :] = jnp.where(s == 0, jnp.zeros_like(halo), halo)
        gbuf_ref[slab, CONV_HALO:, :] = g_ref[:, lanes]
    for c in range(CONV_TILE // CONV_ROWS):
        _dwconv_ln_chunk(c * CONV_ROWS, gbuf_ref, dw_ref, db_ref, lg_ref, lb_ref,
                         tmp_ref.at[c % 2], hnew_ref, d=d)


def _conv_pw2(glu, dw_w, dw_b, ln_g, ln_b, w, b, x, gate):
    s, d = glu.shape
    n_tiles = s // CONV_TILE
    halo_blocks_per_tile = CONV_TILE // CONV_HALO
    conv_tile = lambda j: jnp.minimum(j, n_tiles - 1)
    done_tile = lambda j: jnp.maximum(j - 1, 0)
    row_vec = pl.BlockSpec((1, d), lambda j: (0, 0))
    return pl.pallas_call(
        functools.partial(_conv_pw2_kernel, d=d),
        grid=(n_tiles + 1,),
        in_specs=[
            pl.BlockSpec((CONV_TILE, d), lambda j: (conv_tile(j), 0)),
            pl.BlockSpec((CONV_HALO, d),
                         lambda j: (jnp.maximum(conv_tile(j) * halo_blocks_per_tile - 1, 0), 0)),
            pl.BlockSpec((CONV_K, d), lambda j: (0, 0)),
            row_vec, row_vec, row_vec,
            pl.BlockSpec((d, d), lambda j: (0, 0), pipeline_mode=pl.Buffered(1)),
            row_vec,
            pl.BlockSpec((CONV_TILE, d), lambda j: (done_tile(j), 0)),
            pl.BlockSpec((1, d), lambda j: (0, gate[1])),
        ],
        out_specs=pl.BlockSpec((CONV_TILE, d), lambda j: (done_tile(j), 0)),
        out_shape=jax.ShapeDtypeStruct((s, d), F32),
        scratch_shapes=[
            pltpu.VMEM((d // V7X_LANES, CONV_TILE + CONV_HALO, V7X_LANES), F32),
            pltpu.VMEM((2, CONV_ROWS, d), F32),
            pltpu.VMEM((CONV_TILE, d), BF16),
            pltpu.VMEM((CONV_TILE, d), BF16),
        ],
        compiler_params=_params(1),
        name="dwconv_pw2",
    )(glu, glu, dw_w, dw_b, ln_g, ln_b, w, b, x, gate[0])


FFN_HALO = V7X_BF16_ROWS_PER_VREG
TF = 512
FFN_OUT_LANES = 512
FFN_VMEM_LIMIT = 60 * 1024 * 1024


def _ffn_kernel(x_ref, xh_ref, g_ref, sh_ref, sc_ref, gate_ref, wg_ref, wv_ref,
                dw_ref, db_ref, wd_ref, o_ref, h_ref, u_ref):
    i = pl.program_id(0)

    @pl.when(pl.program_id(1) == 0)
    def _():
        target = [(g_ref, sh_ref, sc_ref, h_ref)]
        _modulate_rows(xh_ref, target, rows=FFN_HALO)
        _modulate_rows(x_ref, target, rows=TM, h_row0=FFN_HALO)
        o_ref[...] = x_ref[...]

    w_up = jnp.concatenate([wg_ref[...], wv_ref[...]], axis=1)
    u = jnp.dot(h_ref[...], w_up, preferred_element_type=F32)
    u_ref[...] = u
    keep_halo = jnp.where(i == 0, 0.0, 1.0)
    u_ref[0:FFN_HALO, 0:TF] = u[0:FFN_HALO, 0:TF] * keep_halo

    gt = db_ref[...] + dw_ref[2:3, :] * u_ref[pl.ds(FFN_HALO, TM), 0:TF]
    gt = gt + dw_ref[1:2, :] * u_ref[pl.ds(FFN_HALO - 1, TM), 0:TF]
    gt = gt + dw_ref[0:1, :] * u_ref[pl.ds(FFN_HALO - 2, TM), 0:TF]
    val = u_ref[pl.ds(FFN_HALO, TM), TF:2 * TF]
    act = (_silu(gt) * val).astype(BF16)
    for c0 in range(0, o_ref.shape[1], FFN_OUT_LANES):
        cols = slice(c0, c0 + FFN_OUT_LANES)
        y = jnp.dot(act, wd_ref[:, cols], preferred_element_type=F32)
        o_ref[:, cols] += gate_ref[:, cols] * y


def _conv_ffn(x, norm_g, shift, scale, gate, layer, up_w, dw_w, dw_b, down_w):
    s, d = x.shape
    f = down_w.shape[1]
    nf = f // TF
    halo_blocks_per_tile = TM // FFN_HALO
    vec = lambda col: pl.BlockSpec((1, d), lambda i, j: (0, col))
    return pl.pallas_call(
        _ffn_kernel,
        grid=(s // TM, nf),
        in_specs=[
            pl.BlockSpec((TM, d), lambda i, j: (i, 0)),
            pl.BlockSpec((FFN_HALO, d),
                         lambda i, j: (jnp.maximum(i * halo_blocks_per_tile - 1, 0), 0)),
            pl.BlockSpec((1, d), lambda i, j: (0, 0)),
            vec(shift[1]), vec(scale[1]), vec(gate[1]),
            pl.BlockSpec((None, d, TF), lambda i, j: (layer, 0, j)),
            pl.BlockSpec((None, d, TF), lambda i, j: (layer, 0, nf + j)),
            pl.BlockSpec((None, FFN_CONV_K, TF), lambda i, j: (layer, 0, j)),
            pl.BlockSpec((None, 1, TF), lambda i, j: (layer, 0, j)),
            pl.BlockSpec((None, TF, d), lambda i, j: (layer, j, 0)),
        ],
        out_specs=pl.BlockSpec((TM, d), lambda i, j: (i, 0)),
        out_shape=jax.ShapeDtypeStruct((s, d), F32),
        scratch_shapes=[
            pltpu.VMEM((TM + FFN_HALO, d), BF16),
            pltpu.VMEM((TM + FFN_HALO, 2 * TF), F32),
        ],
        compiler_params=_params(2, FFN_VMEM_LIMIT),
        name="conv_ffn",
    )(x, x, norm_g, shift[0], scale[0], gate[0], up_w, up_w, dw_w, dw_b, down_w)


def _rope_table_kernel(pos_ref, freq_ref, cos_ref, sin_lo_ref, sin_hi_ref):
    pos = pos_ref[...].astype(F32)
    ang = pos * freq_ref[...]
    lane = lax.broadcasted_iota(jnp.int32, ang.shape, 1)
    c = jnp.cos(ang)
    sn = jnp.sin(ang)
    half = ROT_DIM // 2
    cos_ref[...] = jnp.where(lane < ROT_DIM, c, 1.0)
    sin_lo_ref[...] = jnp.where(lane < half, -sn, 0.0)
    sin_hi_ref[...] = jnp.where((lane >= half) & (lane < ROT_DIM), sn, 0.0)


def _rope_tables(positions):
    s = positions.shape[0]
    rows = 1024
    inv_freq = ROPE_THETA ** (-jnp.arange(0, ROT_DIM, 2, dtype=F32) / ROT_DIM)
    lane_freq = jnp.concatenate(
        [inv_freq, inv_freq, jnp.zeros((HEAD_DIM - ROT_DIM,), F32)])[None, :]
    out = jax.ShapeDtypeStruct((s, HEAD_DIM), F32)
    spec = pl.BlockSpec((rows, HEAD_DIM), lambda i: (i, 0))
    return pl.pallas_call(
        _rope_table_kernel,
        grid=(s // rows,),
        in_specs=[pl.BlockSpec((rows, 1), lambda i: (i, 0)),
                  pl.BlockSpec((1, HEAD_DIM), lambda i: (0, 0))],
        out_specs=[spec, spec, spec],
        out_shape=[out, out, out],
        compiler_params=_params(1),
        name="rope_tables",
    )(positions.reshape(s, 1), lane_freq)


TILES_PER_GROUP = O_WIDTH // TN


def _norm_rope_head(q, hg, cos, sin_lo, sin_hi):
    ms = jnp.mean(q * q, axis=-1, keepdims=True)
    qn = q * lax.rsqrt(ms + EPS) * hg
    hi_to_lo = pltpu.roll(qn, HEAD_DIM - ROT_DIM // 2, axis=1)
    lo_to_hi = pltpu.roll(qn, ROT_DIM // 2, axis=1)
    return qn * cos + hi_to_lo * sin_lo + lo_to_hi * sin_hi


SPLIT_STRIDE = 4


def _finish_tile(y_ref, out_ref, head_gain, use_rope, table_refs, y4_ref, *, r):
    rows = TM // r
    two_pass = r > SPLIT_STRIDE
    r_outer = r // SPLIT_STRIDE
    for head in range(TN // HEAD_DIM):
        cos, sin_lo, sin_hi = (t[...] for t in table_refs)
        y = y_ref[head]
        y_ref[head] = jnp.where(use_rope,
                                _norm_rope_head(y, head_gain, cos, sin_lo, sin_hi), y)
        if two_pass:
            for p in range(SPLIT_STRIDE):
                y4_ref[head % 2, p] = y_ref[head, pl.ds(p, TM // SPLIT_STRIDE,
                                                        stride=SPLIT_STRIDE), :]
        for res in range(r):
            if r == 1:
                y = y_ref[head]
            elif two_pass:
                p, q = res % SPLIT_STRIDE, res // SPLIT_STRIDE
                y = y4_ref[head % 2, p, pl.ds(q, rows, stride=r_outer), :]
            else:
                y = y_ref[head, pl.ds(res, rows, stride=r), :]
            out_ref[res, :, _lane_slab(head)] = y.astype(BF16)


N_KINDS = 3
K_KIND, V_KIND, Q_KIND = range(N_KINDS)
GROUP_TILES = N_KINDS * TILES_PER_GROUP
ROW_TILES = N_GROUPS * GROUP_TILES


def _qkv_kernel(x_ref, gkv_ref, shkv_ref, sckv_ref, gq_ref, shq_ref, scq_ref, w_ref,
                hg_ref, cos_ref, slo_ref, shi_ref, *refs):
    outs = refs[:N_GROUPS]
    h_ref, ya_ref, yb_ref, y4_ref = refs[N_GROUPS:]
    s = pl.program_id(0)
    t = TILES_PER_GROUP
    tile = jnp.minimum(s, pl.num_programs(0) - 2)
    col = tile % ROW_TILES
    done_col = jnp.maximum(s - 1, 0) % ROW_TILES
    done_kind = (done_col % GROUP_TILES) // t
    done_group = done_col // GROUP_TILES
    tables = (cos_ref, slo_ref, shi_ref)

    @pl.when(col == 0)
    def _():
        _modulate_rows(x_ref, [(gkv_ref, shkv_ref, sckv_ref, h_ref.at[0]),
                               (gq_ref, shq_ref, scq_ref, h_ref.at[1])], rows=TM)

    @pl.when(s == 0)
    def _():
        ya_ref[...] = jnp.zeros(ya_ref.shape, F32)

    h_sel = ((col % GROUP_TILES) // t == Q_KIND).astype(jnp.int32)
    gain = hg_ref[(done_kind == Q_KIND).astype(jnp.int32)]
    use_rope = done_kind != V_KIND
    for group, r in enumerate(GROUP_DILATIONS):
        @pl.when(done_group == group)
        def _(group=group, r=r):
            yb_ref[...] = ya_ref[...]
            y = jnp.dot(h_ref[h_sel], w_ref[...], preferred_element_type=F32)
            for head in range(TN // HEAD_DIM):
                ya_ref[head] = y[:, _lane_slab(head)]
            _finish_tile(yb_ref, outs[group], gain, use_rope, tables, y4_ref, r=r)


def _qkv(x, kv_norm_g, kv_shift, kv_scale, q_norm_g, q_shift, q_scale, w_all, head_gains,
         tables):
    s, d = x.shape
    t = TILES_PER_GROUP
    n_tiles = (s // TM) * ROW_TILES
    vec = lambda col: pl.BlockSpec((1, d), lambda j: (0, col))
    one = pl.BlockSpec((1, d), lambda j: (0, 0))

    def tile_of(j):
        return jnp.minimum(j, n_tiles - 1)

    def done_of(j):
        return jnp.maximum(j - 1, 0)

    def w_col(j):
        col = tile_of(j) % ROW_TILES
        group, rem = col // GROUP_TILES, col % GROUP_TILES
        return (rem // t) * (N_GROUPS * t) + group * t + rem % t

    def out_spec(group):
        r = GROUP_DILATIONS[group]
        return pl.BlockSpec(
            (r, TM // r, TN),
            lambda j: (0, done_of(j) // ROW_TILES,
                       jnp.clip(done_of(j) % ROW_TILES - group * GROUP_TILES,
                                0, GROUP_TILES - 1)))

    tab = pl.BlockSpec((TM, HEAD_DIM), lambda j: (done_of(j) // ROW_TILES, 0))
    return pl.pallas_call(
        _qkv_kernel,
        grid=(n_tiles + 1,),
        in_specs=[
            pl.BlockSpec((TM, d), lambda j: (tile_of(j) // ROW_TILES, 0)),
            one, vec(kv_shift[1]), vec(kv_scale[1]),
            one, vec(q_shift[1]), vec(q_scale[1]),
            pl.BlockSpec((d, TN), lambda j: (0, w_col(j))),
            pl.BlockSpec((2, 1, HEAD_DIM), lambda j: (0, 0, 0)),
            tab, tab, tab,
        ],
        out_specs=[out_spec(g) for g in range(N_GROUPS)],
        out_shape=[jax.ShapeDtypeStruct((r, s // r, N_KINDS * O_WIDTH), BF16)
                   for r in GROUP_DILATIONS],
        scratch_shapes=[
            pltpu.VMEM((2, TM, d), BF16),
            pltpu.VMEM((TN // V7X_LANES, TM, V7X_LANES), F32),
            pltpu.VMEM((TN // V7X_LANES, TM, V7X_LANES), F32),
            pltpu.VMEM((2, SPLIT_STRIDE, TM // SPLIT_STRIDE, V7X_LANES), F32),
        ],
        compiler_params=_params(1),
        name="qkv",
    )(x, kv_norm_g, kv_shift[0], kv_scale[0], q_norm_g, q_shift[0], q_scale[0],
      w_all, head_gains, *tables)


ATTN_Q_BLOCKS = 4


def _band_attn_kernel(q_ref, kp_ref, kc_ref, vp_ref, vc_ref, o_ref, lse_ref, *, span):
    n = pl.program_id(1)
    qi = lax.broadcasted_iota(jnp.int32, (BLK, 2 * BLK), 0)
    kj = lax.broadcasted_iota(jnp.int32, (BLK, 2 * BLK), 1)
    dist = qi + BLK - kj
    band = (dist >= 0) & (dist <= span)
    first_band = band & ((n > 0) | (kj >= BLK))
    lane = lax.broadcasted_iota(jnp.int32, (BLK, V7X_LANES), 1)
    scale = 1.0 / math.sqrt(HEAD_DIM)
    for blk in range(ATTN_Q_BLOCKS):
        rows = slice(blk * BLK, (blk + 1) * BLK)
        lse_tile = jnp.zeros((BLK, V7X_LANES), F32)
        for h in range(HEADS_PER_GROUP):
            cols = slice(h * HEAD_DIM, (h + 1) * HEAD_DIM)
            q = q_ref[rows, cols]
            if blk == 0:
                k = jnp.concatenate([kp_ref[:, cols], kc_ref[rows, cols]], axis=0)
                v = jnp.concatenate([vp_ref[:, cols], vc_ref[rows, cols]], axis=0)
                mask = first_band
            else:
                keys = slice((blk - 1) * BLK, (blk + 1) * BLK)
                k, v, mask = kc_ref[keys, cols], vc_ref[keys, cols], band
            sc = lax.dot_general(q, k, (((1,), (1,)), ((), ())),
                                 preferred_element_type=F32) * scale
            sc = jnp.where(mask, sc, NEG)
            m = jnp.max(sc, axis=-1, keepdims=True)
            e = jnp.exp(sc - m)
            l = jnp.sum(e, axis=-1, keepdims=True)
            p = (e * (1.0 / l)).astype(BF16)
            o_ref[rows, cols] = jnp.dot(p, v, preferred_element_type=F32)
            lse_tile = jnp.where(lane == h, m + jnp.log(l), lse_tile)
        lse_ref[rows, :] = lse_tile


def _band_attn(kvq, group):
    r, rows, _ = kvq.shape
    step_rows = ATTN_Q_BLOCKS * BLK

    def cur(col):
        return pl.BlockSpec((None, step_rows, O_WIDTH), lambda j, n: (j, n, col))

    def prev(col):
        return pl.BlockSpec((None, BLK, O_WIDTH),
                            lambda j, n: (j, jnp.maximum(n * ATTN_Q_BLOCKS - 1, 0), col))

    return pl.pallas_call(
        functools.partial(_band_attn_kernel, span=GROUP_SPANS[group]),
        grid=(r, rows // step_rows),
        in_specs=[cur(Q_KIND), prev(K_KIND), cur(K_KIND), prev(V_KIND), cur(V_KIND)],
        out_specs=[cur(0),
                   pl.BlockSpec((None, step_rows, V7X_LANES), lambda j, n: (j, n, 0))],
        out_shape=[jax.ShapeDtypeStruct((r, rows, O_WIDTH), F32),
                   jax.ShapeDtypeStruct((r, rows, V7X_LANES), F32)],
        compiler_params=_params(2),
        name=f"band_attn_r{r}",
    )(kvq, kvq, kvq, kvq, kvq)


TM_MIX = 256


def _rows_from_residues(src_ref, dst_ref, *, r, lanes=None):
    n = src_ref.shape[1]
    for res in range(r):
        rows = pl.ds(res, n, stride=r)
        if lanes is None:
            dst_ref[rows, :] = src_ref[res]
        else:
            for slab in range(lanes // V7X_LANES):
                dst_ref[slab, rows, :] = src_ref[res, :, _lane_slab(slab)]


def _mix_wo_kernel(o0_ref, o1_ref, o2_ref, l0_ref, l1_ref, l2_ref, w_ref, x_ref,
                   gate_ref, out_ref, hnew_ref, hprev_ref, on1_ref, on2_ref, ln1_ref,
                   ln2_ref):
    s = pl.program_id(0)

    @pl.when(s == 0)
    def _():
        hnew_ref[...] = jnp.zeros(hnew_ref.shape, BF16)

    hprev_ref[...] = hnew_ref[...]
    y = jnp.dot(hprev_ref[...], w_ref[...], preferred_element_type=F32)
    out_ref[...] = x_ref[...] + gate_ref[...] * y

    r1, r2 = GROUP_DILATIONS[1], GROUP_DILATIONS[2]
    _rows_from_residues(l1_ref, ln1_ref, r=r1)
    _rows_from_residues(l2_ref, ln2_ref, r=r2)
    _rows_from_residues(o1_ref, on1_ref, r=r1, lanes=O_WIDTH)
    _rows_from_residues(o2_ref, on2_ref, r=r2, lanes=O_WIDTH)
    l0, l1, l2 = l0_ref[0], ln1_ref[...], ln2_ref[...]
    m = jnp.maximum(jnp.maximum(l0, l1), l2)
    e0, e1, e2 = jnp.exp(l0 - m), jnp.exp(l1 - m), jnp.exp(l2 - m)
    inv = 1.0 / (e0 + e1 + e2)
    a0, a1, a2 = e0 * inv, e1 * inv, e2 * inv
    for h in range(HEADS_PER_GROUP):
        cols = _lane_slab(h)
        o = (a0[:, h:h + 1] * o0_ref[0, :, cols] + a1[:, h:h + 1] * on1_ref[h]
             + a2[:, h:h + 1] * on2_ref[h])
        hnew_ref[:, cols] = o.astype(BF16)


def _mix_wo(outs, lses, w_o, x, gate):
    s, d = x.shape
    n_tiles = s // TM_MIX
    mix_tile = lambda j: jnp.minimum(j, n_tiles - 1)
    done_tile = lambda j: jnp.maximum(j - 1, 0)

    def planes(width, r):
        return pl.BlockSpec((r, TM_MIX // r, width), lambda j: (0, mix_tile(j), 0))

    slabs = pltpu.VMEM((O_WIDTH // V7X_LANES, TM_MIX, V7X_LANES), F32)
    rows = pltpu.VMEM((TM_MIX, V7X_LANES), F32)
    h_tile = pltpu.VMEM((TM_MIX, O_WIDTH), BF16)
    return pl.pallas_call(
        _mix_wo_kernel,
        grid=(n_tiles + 1,),
        in_specs=[planes(O_WIDTH, r) for r in GROUP_DILATIONS]
        + [planes(V7X_LANES, r) for r in GROUP_DILATIONS]
        + [pl.BlockSpec((O_WIDTH, d), lambda j: (0, 0), pipeline_mode=pl.Buffered(1)),
           pl.BlockSpec((TM_MIX, d), lambda j: (done_tile(j), 0)),
           pl.BlockSpec((1, d), lambda j: (0, gate[1]))],
        out_specs=pl.BlockSpec((TM_MIX, d), lambda j: (done_tile(j), 0)),
        out_shape=jax.ShapeDtypeStruct((s, d), F32),
        scratch_shapes=[h_tile, h_tile, slabs, slabs, rows, rows],
        compiler_params=_params(1),
        name="mix_wo",
    )(*outs, *lses, w_o, x, gate[0])


def kernel(x, c, positions, mod_w, mod_b, norm_mix_g, norm_ffn_g, conv_pw1_w, conv_pw1_b,
           conv_dw_w, conv_dw_b, conv_ln_g, conv_ln_b, conv_pw2_w, conv_pw2_b, kv_mod_w,
           kv_mod_b, kv_norm_g, w_kv, k_norm_g, w_q, q_norm_g, w_o, ffn_up_w, ffn_dw_w,
           ffn_dw_b, ffn_down_w):
    batch, s, d = x.shape
    assert (batch, s, d) == (1, SEQ, D_MODEL)
    x = x[0]
    c_col = c.reshape(d, 1)
    row = lambda v: v.reshape(1, -1)

    mod = _mod_matvec(c_col, mod_w, mod_b[:, None, :])
    kv_mod = _mod_matvec(c_col, kv_mod_w[None], kv_mod_b[None, None, :])[0]

    mvec = lambda l, q: (mod[l], q)
    ffn_dw_b3 = ffn_dw_b[:, None, :]

    conv_pw1_w, conv_pw2_w, w_o, ffn_up_w, ffn_down_w = (
        w.astype(BF16) for w in (conv_pw1_w, conv_pw2_w, w_o, ffn_up_w, ffn_down_w))
    w_qkv = jnp.concatenate([w_kv.astype(BF16), w_q[0].astype(BF16)], axis=1)

    def ffn(x, l):
        return _conv_ffn(x, row(norm_ffn_g[l]), mvec(l, 3), mvec(l, 4), mvec(l, 5), l,
                         ffn_up_w, ffn_dw_w, ffn_dw_b3, ffn_down_w)

    glu = _pw1_glu(x, row(norm_mix_g[0]), mvec(0, 0), mvec(0, 1),
                   conv_pw1_w[0], row(conv_pw1_b[0]))
    x = _conv_pw2(glu, conv_dw_w[0], row(conv_dw_b[0]), row(conv_ln_g[0]),
                  row(conv_ln_b[0]), conv_pw2_w[0], row(conv_pw2_b[0]), x, mvec(0, 2))
    x = ffn(x, 0)

    tables = _rope_tables(positions[0])
    head_gains = jnp.stack([k_norm_g, q_norm_g[0]])[:, None, :]
    kvq = _qkv(x, row(kv_norm_g), (kv_mod, 0), (kv_mod, 1),
               row(norm_mix_g[1]), mvec(1, 0), mvec(1, 1), w_qkv, head_gains, tables)
    outs, lses = zip(*[_band_attn(kvq[g], g) for g in range(N_GROUPS)])
    x = _mix_wo(outs, lses, w_o[0], x, mvec(1, 2))
    x = ffn(x, 1)
    return x[None]
```

```python
import functools
import math

import jax
import jax.numpy as jnp
from jax import lax
from jax.experimental import pallas as pl
from jax.experimental.pallas import tpu as pltpu

D_MODEL = 2048
SEQ = 8192
CONV_K = 31
FFN_CONV_K = 3
D_FF = 5632
GROUP_DILATIONS = (1, 4, 16)
GROUP_SPANS = (128, 128, 128)
N_GROUPS = 3
HEADS_PER_GROUP = 8
HEAD_DIM = 128
Q_WIDTH = N_GROUPS * HEADS_PER_GROUP * HEAD_DIM
O_WIDTH = HEADS_PER_GROUP * HEAD_DIM
ROT_DIM = HEAD_DIM // 4
ROPE_THETA = 500000.0
BLK = 128
EPS = 1e-6
NEG = -1e30

V7X_LANES = 128
V7X_SUBLANES = 8
V7X_BF16_ROWS_PER_VREG = 16
V7X_VMEM_BYTES = 64 * 1024 * 1024
VMEM_LIMIT = 56 * 1024 * 1024

TM = 1024
TN = 512
TN_GLU = TN
ROW_CHUNK = 64

F32 = jnp.float32
BF16 = jnp.bfloat16


def _params(n_axes, vmem_limit=VMEM_LIMIT):
    return pltpu.CompilerParams(
        dimension_semantics=("arbitrary",) * n_axes,
        vmem_limit_bytes=vmem_limit)


def _sigmoid(x):
    return 1.0 / (1.0 + jnp.exp(-x))


def _silu(x):
    return x * _sigmoid(x)


def _lane_slab(slab):
    return slice(slab * V7X_LANES, (slab + 1) * V7X_LANES)


def _matvec_kernel(c_ref, w_ref, b_ref, o_ref, sb_ref, *, k_dim, tn):
    first = (pl.program_id(0) == 0) & (pl.program_id(1) == 0)

    @pl.when(first)
    def _():
        c = c_ref[...]
        sb_ref[...] = jnp.broadcast_to(_silu(c), (k_dim, V7X_LANES))

    n_groups = tn // V7X_LANES

    def body(t, accs):
        r0 = pl.multiple_of(t * ROW_CHUNK, ROW_CHUNK)
        s = sb_ref[pl.ds(r0, ROW_CHUNK), :]
        w = w_ref[pl.ds(r0, ROW_CHUNK), :]
        new = []
        for g in range(n_groups):
            p = w[:, _lane_slab(g)] * s
            a = accs[g]
            for u in range(ROW_CHUNK // V7X_SUBLANES):
                a = a + p[u * V7X_SUBLANES:(u + 1) * V7X_SUBLANES, :]
            new.append(a)
        return tuple(new)

    init = tuple(jnp.zeros((V7X_SUBLANES, V7X_LANES), F32) for _ in range(n_groups))
    accs = lax.fori_loop(0, k_dim // ROW_CHUNK, body, init)
    row = jnp.concatenate([jnp.sum(a, axis=0, keepdims=True) for a in accs], axis=1)
    o_ref[...] = row + b_ref[...]


def _mod_matvec(c_col, w, b):
    n_l, k_dim, n = w.shape
    tn = 1024
    return pl.pallas_call(
        functools.partial(_matvec_kernel, k_dim=k_dim, tn=tn),
        grid=(n_l, n // tn),
        in_specs=[
            pl.BlockSpec((k_dim, 1), lambda l, j: (0, 0)),
            pl.BlockSpec((None, k_dim, tn), lambda l, j: (l, 0, j)),
            pl.BlockSpec((None, 1, tn), lambda l, j: (l, 0, j)),
        ],
        out_specs=pl.BlockSpec((None, 1, tn), lambda l, j: (l, 0, j)),
        out_shape=jax.ShapeDtypeStruct((n_l, 1, n), F32),
        scratch_shapes=[pltpu.VMEM((k_dim, V7X_LANES), F32)],
        compiler_params=_params(2),
        name="mod_matvec",
    )(c_col, w, b)


MOD_ROWS = V7X_BF16_ROWS_PER_VREG
MOD_UNROLL = 4


def _modulate_rows(x_ref, targets, *, rows, h_row0=0, inline=False):
    def chunk(r0, h0):
        x = x_ref[pl.ds(r0, MOD_ROWS), :]
        ms = jnp.mean(x * x, axis=-1, keepdims=True)
        y = x * lax.rsqrt(ms + EPS)
        for g_ref, shift_ref, scale_ref, h_ref in targets:
            h = (y * g_ref[...]) * (1.0 + scale_ref[...]) + shift_ref[...]
            h_ref[pl.ds(h0, MOD_ROWS), :] = h.astype(BF16)

    trips = rows // MOD_ROWS
    if inline:
        for t in range(trips):
            chunk(t * MOD_ROWS, h_row0 + t * MOD_ROWS)
        return

    def body(t, carry):
        chunk(pl.multiple_of(t * MOD_ROWS, MOD_ROWS),
              pl.multiple_of(h_row0 + t * MOD_ROWS, MOD_ROWS))
        return carry

    lax.fori_loop(0, trips, body, 0, unroll=min(MOD_UNROLL, trips))


PW1_TILE = 512


def _pw1_glu_kernel(x_ref, g_ref, sh_ref, sc_ref, w_ref, b_ref, o_ref, hnew_ref, hprev_ref):
    s = pl.program_id(0)
    d = o_ref.shape[1]

    @pl.when(s == 0)
    def _():
        hnew_ref[...] = jnp.zeros(hnew_ref.shape, BF16)

    hprev_ref[...] = hnew_ref[...]
    h = hprev_ref[...]
    for c0 in range(0, d, TN_GLU):
        val, gate = slice(c0, c0 + TN_GLU), slice(d + c0, d + c0 + TN_GLU)
        a = jnp.dot(h, w_ref[:, val], preferred_element_type=F32) + b_ref[:, val]
        gt = jnp.dot(h, w_ref[:, gate], preferred_element_type=F32) + b_ref[:, gate]
        o_ref[:, val] = a * _sigmoid(gt)
    _modulate_rows(x_ref, [(g_ref, sh_ref, sc_ref, hnew_ref)], rows=PW1_TILE, inline=True)


def _pw1_glu(x, norm_g, shift, scale, w, b):
    s, d = x.shape
    n_tiles = s // PW1_TILE
    vec = lambda col: pl.BlockSpec((1, d), lambda j: (0, col))
    return pl.pallas_call(
        _pw1_glu_kernel,
        grid=(n_tiles + 1,),
        in_specs=[
            pl.BlockSpec((PW1_TILE, d), lambda j: (jnp.minimum(j, n_tiles - 1), 0)),
            pl.BlockSpec((1, d), lambda j: (0, 0)),
            vec(shift[1]), vec(scale[1]),
            pl.BlockSpec((d, 2 * d), lambda j: (0, 0), pipeline_mode=pl.Buffered(1)),
            pl.BlockSpec((1, 2 * d), lambda j: (0, 0)),
        ],
        out_specs=pl.BlockSpec((PW1_TILE, d), lambda j: (jnp.maximum(j - 1, 0), 0)),
        out_shape=jax.ShapeDtypeStruct((s, d), F32),
        scratch_shapes=[pltpu.VMEM((PW1_TILE, d), BF16), pltpu.VMEM((PW1_TILE, d), BF16)],
        compiler_params=_params(1),
        name="pw1_glu",
    )(x, norm_g, shift[0], scale[0], w, b)


CONV_HALO = 32
CONV_ROWS = 32
CONV_TILE = 256


def _dwconv_ln_chunk(r0, gbuf_ref, dw_ref, db_ref, lg_ref, lb_ref, tmp_ref, h_ref, *, d):
    off = CONV_HALO - (CONV_K - 1)
    for slab in range(d // V7X_LANES):
        lanes = _lane_slab(slab)
        acc = jnp.broadcast_to(db_ref[:, lanes], (CONV_ROWS, V7X_LANES))
        for k in range(CONV_K):
            tap = gbuf_ref[slab, pl.ds(r0 + off + k, CONV_ROWS, stride=1), :]
            acc = acc + tap * dw_ref[k:k + 1, lanes]
        tmp_ref[:, lanes] = acc
    u = tmp_ref[...]
    mu = jnp.mean(u, axis=-1, keepdims=True)
    uc = u - mu
    var = jnp.mean(uc * uc, axis=-1, keepdims=True)
    y = uc * lax.rsqrt(var + EPS) * lg_ref[...] + lb_ref[...]
    h_ref[pl.ds(r0, CONV_ROWS), :] = _silu(y).astype(BF16)


def _conv_pw2_kernel(g_ref, halo_ref, dw_ref, db_ref, lg_ref, lb_ref, w_ref, b_ref,
                     x_ref, gate_ref, o_ref, gbuf_ref, tmp_ref, hnew_ref, hprev_ref, *, d):
    s = pl.program_id(0)

    @pl.when(s == 0)
    def _():
        hnew_ref[...] = jnp.zeros(hnew_ref.shape, BF16)

    hprev_ref[...] = hnew_ref[...]
    y = jnp.dot(hprev_ref[...], w_ref[...], preferred_element_type=F32)
    o_ref[...] = x_ref[...] + gate_ref[...] * (y + b_ref[...])

    for slab in range(d // V7X_LANES):
        lanes = _lane_slab(slab)
        halo = halo_ref[:, lanes]
        gbuf_ref[slab, 0:CONV_HALO, :] = jnp.where(s == 0, jnp.zeros_like(halo), halo)
        gbuf_ref[slab, CONV_HALO:, :] = g_ref[:, lanes]
    for c in range(CONV_TILE // CONV_ROWS):
        _dwconv_ln_chunk(c * CONV_ROWS, gbuf_ref, dw_ref, db_ref, lg_ref, lb_ref,
                         tmp_ref.at[c % 2], hnew_ref, d=d)


def _conv_pw2(glu, dw_w, dw_b, ln_g, ln_b, w, b, x, gate):
    s, d = glu.shape
    n_tiles = s // CONV_TILE
    halo_blocks_per_tile = CONV_TILE // CONV_HALO
    conv_tile = lambda j: jnp.minimum(j, n_tiles - 1)
    done_tile = lambda j: jnp.maximum(j - 1, 0)
    row_vec = pl.BlockSpec((1, d), lambda j: (0, 0))
    return pl.pallas_call(
        functools.partial(_conv_pw2_kernel, d=d),
        grid=(n_tiles + 1,),
        in_specs=[
            pl.BlockSpec((CONV_TILE, d), lambda j: (conv_tile(j), 0)),
            pl.BlockSpec((CONV_HALO, d),
                         lambda j: (jnp.maximum(conv_tile(j) * halo_blocks_per_tile - 1, 0), 0)),
            pl.BlockSpec((CONV_K, d), lambda j: (0, 0)),
            row_vec, row_vec, row_vec,
            pl.BlockSpec((d, d), lambda j: (0, 0), pipeline_mode=pl.Buffered(1)),
            row_vec,
            pl.BlockSpec((CONV_TILE, d), lambda j: (done_tile(j), 0)),
            pl.BlockSpec((1, d), lambda j: (0, gate[1])),
        ],
        out_specs=pl.BlockSpec((CONV_TILE, d), lambda j: (done_tile(j), 0)),
        out_shape=jax.ShapeDtypeStruct((s, d), F32),
        scratch_shapes=[
            pltpu.VMEM((d // V7X_LANES, CONV_TILE + CONV_HALO, V7X_LANES), F32),
            pltpu.VMEM((2, CONV_ROWS, d), F32),
            pltpu.VMEM((CONV_TILE, d), BF16),
            pltpu.VMEM((CONV_TILE, d), BF16),
        ],
        compiler_params=_params(1),
        name="dwconv_pw2",
    )(glu, glu, dw_w, dw_b, ln_g, ln_b, w, b, x, gate[0])


FFN_HALO = V7X_BF16_ROWS_PER_VREG
TF = 512
FFN_OUT_LANES = 512
FFN_VMEM_LIMIT = 60 * 1024 * 1024


def _ffn_kernel(x_ref, xh_ref, g_ref, sh_ref, sc_ref, gate_ref, wg_ref, wv_ref,
                dw_ref, db_ref, wd_ref, o_ref, h_ref, u_ref):
    i = pl.program_id(0)

    @pl.when(pl.program_id(1) == 0)
    def _():
        target = [(g_ref, sh_ref, sc_ref, h_ref)]
        _modulate_rows(xh_ref, target, rows=FFN_HALO)
        _modulate_rows(x_ref, target, rows=TM, h_row0=FFN_HALO)
        o_ref[...] = x_ref[...]

    w_up = jnp.concatenate([wg_ref[...], wv_ref[...]], axis=1)
    u = jnp.dot(h_ref[...], w_up, preferred_element_type=F32)
    u_ref[...] = u
    keep_halo = jnp.where(i == 0, 0.0, 1.0)
    u_ref[0:FFN_HALO, 0:TF] = u[0:FFN_HALO, 0:TF] * keep_halo

    gt = db_ref[...] + dw_ref[2:3, :] * u_ref[pl.ds(FFN_HALO, TM), 0:TF]
    gt = gt + dw_ref[1:2, :] * u_ref[pl.ds(FFN_HALO - 1, TM), 0:TF]
    gt = gt + dw_ref[0:1, :] * u_ref[pl.ds(FFN_HALO - 2, TM), 0:TF]
    val = u_ref[pl.ds(FFN_HALO, TM), TF:2 * TF]
    act = (_silu(gt) * val).astype(BF16)
    for c0 in range(0, o_ref.shape[1], FFN_OUT_LANES):
        cols = slice(c0, c0 + FFN_OUT_LANES)
        y = jnp.dot(act, wd_ref[:, cols], preferred_element_type=F32)
        o_ref[:, cols] += gate_ref[:, cols] * y


def _conv_ffn(x, norm_g, shift, scale, gate, layer, up_w, dw_w, dw_b, down_w):
    s, d = x.shape
    f = down_w.shape[1]
    nf = f // TF
    halo_blocks_per_tile = TM // FFN_HALO
    vec = lambda col: pl.BlockSpec((1, d), lambda i, j: (0, col))
    return pl.pallas_call(
        _ffn_kernel,
        grid=(s // TM, nf),
        in_specs=[
            pl.BlockSpec((TM, d), lambda i, j: (i, 0)),
            pl.BlockSpec((FFN_HALO, d),
                         lambda i, j: (jnp.maximum(i * halo_blocks_per_tile - 1, 0), 0)),
            pl.BlockSpec((1, d), lambda i, j: (0, 0)),
            vec(shift[1]), vec(scale[1]), vec(gate[1]),
            pl.BlockSpec((None, d, TF), lambda i, j: (layer, 0, j)),
            pl.BlockSpec((None, d, TF), lambda i, j: (layer, 0, nf + j)),
            pl.BlockSpec((None, FFN_CONV_K, TF), lambda i, j: (layer, 0, j)),
            pl.BlockSpec((None, 1, TF), lambda i, j: (layer, 0, j)),
            pl.BlockSpec((None, TF, d), lambda i, j: (layer, j, 0)),
        ],
        out_specs=pl.BlockSpec((TM, d), lambda i, j: (i, 0)),
        out_shape=jax.ShapeDtypeStruct((s, d), F32),
        scratch_shapes=[
            pltpu.VMEM((TM + FFN_HALO, d), BF16),
            pltpu.VMEM((TM + FFN_HALO, 2 * TF), F32),
        ],
        compiler_params=_params(2, FFN_VMEM_LIMIT),
        name="conv_ffn",
    )(x, x, norm_g, shift[0], scale[0], gate[0], up_w, up_w, dw_w, dw_b, down_w)


def _rope_table_kernel(pos_ref, freq_ref, cos_ref, sin_lo_ref, sin_hi_ref):
    pos = pos_ref[...].astype(F32)
    ang = pos * freq_ref[...]
    lane = lax.broadcasted_iota(jnp.int32, ang.shape, 1)
    c = jnp.cos(ang)
    sn = jnp.sin(ang)
    half = ROT_DIM // 2
    cos_ref[...] = jnp.where(lane < ROT_DIM, c, 1.0)
    sin_lo_ref[...] = jnp.where(lane < half, -sn, 0.0)
    sin_hi_ref[...] = jnp.where((lane >= half) & (lane < ROT_DIM), sn, 0.0)


def _rope_tables(positions):
    s = positions.shape[0]
    rows = 1024
    inv_freq = ROPE_THETA ** (-jnp.arange(0, ROT_DIM, 2, dtype=F32) / ROT_DIM)
    lane_freq = jnp.concatenate(
        [inv_freq, inv_freq, jnp.zeros((HEAD_DIM - ROT_DIM,), F32)])[None, :]
    out = jax.ShapeDtypeStruct((s, HEAD_DIM), F32)
    spec = pl.BlockSpec((rows, HEAD_DIM), lambda i: (i, 0))
    return pl.pallas_call(
        _rope_table_kernel,
        grid=(s // rows,),
        in_specs=[pl.BlockSpec((rows, 1), lambda i: (i, 0)),
                  pl.BlockSpec((1, HEAD_DIM), lambda i: (0, 0))],
        out_specs=[spec, spec, spec],
        out_shape=[out, out, out],
        compiler_params=_params(1),
        name="rope_tables",
    )(positions.reshape(s, 1), lane_freq)


TILES_PER_GROUP = O_WIDTH // TN


def _norm_rope_head(q, hg, cos, sin_lo, sin_hi):
    ms = jnp.mean(q * q, axis=-1, keepdims=True)
    qn = q * lax.rsqrt(ms + EPS) * hg
    hi_to_lo = pltpu.roll(qn, HEAD_DIM - ROT_DIM // 2, axis=1)
    lo_to_hi = pltpu.roll(qn, ROT_DIM // 2, axis=1)
    return qn * cos + hi_to_lo * sin_lo + lo_to_hi * sin_hi


SPLIT_STRIDE = 4


def _finish_tile(y_ref, out_ref, head_gain, use_rope, table_refs, y4_ref, *, r):
    rows = TM // r
    two_pass = r > SPLIT_STRIDE
    r_outer = r // SPLIT_STRIDE
    for head in range(TN // HEAD_DIM):
        cos, sin_lo, sin_hi = (t[...] for t in table_refs)
        y = y_ref[head]
        y_ref[head] = jnp.where(use_rope,
                                _norm_rope_head(y, head_gain, cos, sin_lo, sin_hi), y)
        if two_pass:
            for p in range(SPLIT_STRIDE):
                y4_ref[head % 2, p] = y_ref[head, pl.ds(p, TM // SPLIT_STRIDE,
                                                        stride=SPLIT_STRIDE), :]
        for res in range(r):
            if r == 1:
                y = y_ref[head]
            elif two_pass:
                p, q = res % SPLIT_STRIDE, res // SPLIT_STRIDE
                y = y4_ref[head % 2, p, pl.ds(q, rows, stride=r_outer), :]
            else:
                y = y_ref[head, pl.ds(res, rows, stride=r), :]
            out_ref[res, :, _lane_slab(head)] = y.astype(BF16)


N_KINDS = 3
K_KIND, V_KIND, Q_KIND = range(N_KINDS)
GROUP_TILES = N_KINDS * TILES_PER_GROUP
ROW_TILES = N_GROUPS * GROUP_TILES


def _qkv_kernel(x_ref, gkv_ref, shkv_ref, sckv_ref, gq_ref, shq_ref, scq_ref, w_ref,
                hg_ref, cos_ref, slo_ref, shi_ref, *refs):
    outs = refs[:N_GROUPS]
    h_ref, ya_ref, yb_ref, y4_ref = refs[N_GROUPS:]
    s = pl.program_id(0)
    t = TILES_PER_GROUP
    tile = jnp.minimum(s, pl.num_programs(0) - 2)
    col = tile % ROW_TILES
    done_col = jnp.maximum(s - 1, 0) % ROW_TILES
    done_kind = (done_col % GROUP_TILES) // t
    done_group = done_col // GROUP_TILES
    tables = (cos_ref, slo_ref, shi_ref)

    @pl.when(col == 0)
    def _():
        _modulate_rows(x_ref, [(gkv_ref, shkv_ref, sckv_ref, h_ref.at[0]),
                               (gq_ref, shq_ref, scq_ref, h_ref.at[1])], rows=TM)

    @pl.when(s == 0)
    def _():
        ya_ref[...] = jnp.zeros(ya_ref.shape, F32)

    h_sel = ((col % GROUP_TILES) // t == Q_KIND).astype(jnp.int32)
    gain = hg_ref[(done_kind == Q_KIND).astype(jnp.int32)]
    use_rope = done_kind != V_KIND
    for group, r in enumerate(GROUP_DILATIONS):
        @pl.when(done_group == group)
        def _(group=group, r=r):
            yb_ref[...] = ya_ref[...]
            y = jnp.dot(h_ref[h_sel], w_ref[...], preferred_element_type=F32)
            for head in range(TN // HEAD_DIM):
                ya_ref[head] = y[:, _lane_slab(head)]
            _finish_tile(yb_ref, outs[group], gain, use_rope, tables, y4_ref, r=r)


def _qkv(x, kv_norm_g, kv_shift, kv_scale, q_norm_g, q_shift, q_scale, w_all, head_gains,
         tables):
    s, d = x.shape
    t = TILES_PER_GROUP
    n_tiles = (s // TM) * ROW_TILES
    vec = lambda col: pl.BlockSpec((1, d), lambda j: (0, col))
    one = pl.BlockSpec((1, d), lambda j: (0, 0))

    def tile_of(j):
        return jnp.minimum(j, n_tiles - 1)

    def done_of(j):
        return jnp.maximum(j - 1, 0)

    def w_col(j):
        col = tile_of(j) % ROW_TILES
        group, rem = col // GROUP_TILES, col % GROUP_TILES
        return (rem // t) * (N_GROUPS * t) + group * t + rem % t

    def out_spec(group):
        r = GROUP_DILATIONS[group]
        return pl.BlockSpec(
            (r, TM // r, TN),
            lambda j: (0, done_of(j) // ROW_TILES,
                       jnp.clip(done_of(j) % ROW_TILES - group * GROUP_TILES,
                                0, GROUP_TILES - 1)))

    tab = pl.BlockSpec((TM, HEAD_DIM), lambda j: (done_of(j) // ROW_TILES, 0))
    return pl.pallas_call(
        _qkv_kernel,
        grid=(n_tiles + 1,),
        in_specs=[
            pl.BlockSpec((TM, d), lambda j: (tile_of(j) // ROW_TILES, 0)),
            one, vec(kv_shift[1]), vec(kv_scale[1]),
            one, vec(q_shift[1]), vec(q_scale[1]),
            pl.BlockSpec((d, TN), lambda j: (0, w_col(j))),
            pl.BlockSpec((2, 1, HEAD_DIM), lambda j: (0, 0, 0)),
            tab, tab, tab,
        ],
        out_specs=[out_spec(g) for g in range(N_GROUPS)],
        out_shape=[jax.ShapeDtypeStruct((r, s // r, N_KINDS * O_WIDTH), BF16)
                   for r in GROUP_DILATIONS],
        scratch_shapes=[
            pltpu.VMEM((2, TM, d), BF16),
            pltpu.VMEM((TN // V7X_LANES, TM, V7X_LANES), F32),
            pltpu.VMEM((TN // V7X_LANES, TM, V7X_LANES), F32),
            pltpu.VMEM((2, SPLIT_STRIDE, TM // SPLIT_STRIDE, V7X_LANES), F32),
        ],
        compiler_params=_params(1),
        name="qkv",
    )(x, kv_norm_g, kv_shift[0], kv_scale[0], q_norm_g, q_shift[0], q_scale[0],
      w_all, head_gains, *tables)


ATTN_Q_BLOCKS = 4


def _band_attn_kernel(q_ref, kp_ref, kc_ref, vp_ref, vc_ref, o_ref, lse_ref, *, span):
    n = pl.program_id(1)
    qi = lax.broadcasted_iota(jnp.int32, (BLK, 2 * BLK), 0)
    kj = lax.broadcasted_iota(jnp.int32, (BLK, 2 * BLK), 1)
    dist = qi + BLK - kj
    band = (dist >= 0) & (dist <= span)
    first_band = band & ((n > 0) | (kj >= BLK))
    lane = lax.broadcasted_iota(jnp.int32, (BLK, V7X_LANES), 1)
    scale = 1.0 / math.sqrt(HEAD_DIM)
    for blk in range(ATTN_Q_BLOCKS):
        rows = slice(blk * BLK, (blk + 1) * BLK)
        lse_tile = jnp.zeros((BLK, V7X_LANES), F32)
        for h in range(HEADS_PER_GROUP):
            cols = slice(h * HEAD_DIM, (h + 1) * HEAD_DIM)
            q = q_ref[rows, cols]
            if blk == 0:
                k = jnp.concatenate([kp_ref[:, cols], kc_ref[rows, cols]], axis=0)
                v = jnp.concatenate([vp_ref[:, cols], vc_ref[rows, cols]], axis=0)
                mask = first_band
            else:
                keys = slice((blk - 1) * BLK, (blk + 1) * BLK)
                k, v, mask = kc_ref[keys, cols], vc_ref[keys, cols], band
            sc = lax.dot_general(q, k, (((1,), (1,)), ((), ())),
                                 preferred_element_type=F32) * scale
            sc = jnp.where(mask, sc, NEG)
            m = jnp.max(sc, axis=-1, keepdims=True)
            e = jnp.exp(sc - m)
            l = jnp.sum(e, axis=-1, keepdims=True)
            p = (e * (1.0 / l)).astype(BF16)
            o_ref[rows, cols] = jnp.dot(p, v, preferred_element_type=F32)
            lse_tile = jnp.where(lane == h, m + jnp.log(l), lse_tile)
        lse_ref[rows, :] = lse_tile


def _band_attn(kvq, group):
    r, rows, _ = kvq.shape
    step_rows = ATTN_Q_BLOCKS * BLK

    def cur(col):
        return pl.BlockSpec((None, step_rows, O_WIDTH), lambda j, n: (j, n, col))

    def prev(col):
        return pl.BlockSpec((None, BLK, O_WIDTH),
                            lambda j, n: (j, jnp.maximum(n * ATTN_Q_BLOCKS - 1, 0), col))

    return pl.pallas_call(
        functools.partial(_band_attn_kernel, span=GROUP_SPANS[group]),
        grid=(r, rows // step_rows),
        in_specs=[cur(Q_KIND), prev(K_KIND), cur(K_KIND), prev(V_KIND), cur(V_KIND)],
        out_specs=[cur(0),
                   pl.BlockSpec((None, step_rows, V7X_LANES), lambda j, n: (j, n, 0))],
        out_shape=[jax.ShapeDtypeStruct((r, rows, O_WIDTH), F32),
                   jax.ShapeDtypeStruct((r, rows, V7X_LANES), F32)],
        compiler_params=_params(2),
        name=f"band_attn_r{r}",
    )(kvq, kvq, kvq, kvq, kvq)


TM_MIX = 256


def _rows_from_residues(src_ref, dst_ref, *, r, lanes=None):
    n = src_ref.shape[1]
    for res in range(r):
        rows = pl.ds(res, n, stride=r)
        if lanes is None:
            dst_ref[rows, :] = src_ref[res]
        else:
            for slab in range(lanes // V7X_LANES):
                dst_ref[slab, rows, :] = src_ref[res, :, _lane_slab(slab)]


def _mix_wo_kernel(o0_ref, o1_ref, o2_ref, l0_ref, l1_ref, l2_ref, w_ref, x_ref,
                   gate_ref, out_ref, hnew_ref, hprev_ref, on1_ref, on2_ref, ln1_ref,
                   ln2_ref):
    s = pl.program_id(0)

    @pl.when(s == 0)
    def _():
        hnew_ref[...] = jnp.zeros(hnew_ref.shape, BF16)

    hprev_ref[...] = hnew_ref[...]
    y = jnp.dot(hprev_ref[...], w_ref[...], preferred_element_type=F32)
    out_ref[...] = x_ref[...] + gate_ref[...] * y

    r1, r2 = GROUP_DILATIONS[1], GROUP_DILATIONS[2]
    _rows_from_residues(l1_ref, ln1_ref, r=r1)
    _rows_from_residues(l2_ref, ln2_ref, r=r2)
    _rows_from_residues(o1_ref, on1_ref, r=r1, lanes=O_WIDTH)
    _rows_from_residues(o2_ref, on2_ref, r=r2, lanes=O_WIDTH)
    l0, l1, l2 = l0_ref[0], ln1_ref[...], ln2_ref[...]
    m = jnp.maximum(jnp.maximum(l0, l1), l2)
    e0, e1, e2 = jnp.exp(l0 - m), jnp.exp(l1 - m), jnp.exp(l2 - m)
    inv = 1.0 / (e0 + e1 + e2)
    a0, a1, a2 = e0 * inv, e1 * inv, e2 * inv
    for h in range(HEADS_PER_GROUP):
        cols = _lane_slab(h)
        o = (a0[:, h:h + 1] * o0_ref[0, :, cols] + a1[:, h:h + 1] * on1_ref[h]
             + a2[:, h:h + 1] * on2_ref[h])
        hnew_ref[:, cols] = o.astype(BF16)


def _mix_wo(outs, lses, w_o, x, gate):
    s, d = x.shape
    n_tiles = s // TM_MIX
    mix_tile = lambda j: jnp.minimum(j, n_tiles - 1)
    done_tile = lambda j: jnp.maximum(j - 1, 0)

    def planes(width, r):
        return pl.BlockSpec((r, TM_MIX // r, width), lambda j: (0, mix_tile(j), 0))

    slabs = pltpu.VMEM((O_WIDTH // V7X_LANES, TM_MIX, V7X_LANES), F32)
    rows = pltpu.VMEM((TM_MIX, V7X_LANES), F32)
    h_tile = pltpu.VMEM((TM_MIX, O_WIDTH), BF16)
    return pl.pallas_call(
        _mix_wo_kernel,
        grid=(n_tiles + 1,),
        in_specs=[planes(O_WIDTH, r) for r in GROUP_DILATIONS]
        + [planes(V7X_LANES, r) for r in GROUP_DILATIONS]
        + [pl.BlockSpec((O_WIDTH, d), lambda j: (0, 0), pipeline_mode=pl.Buffered(1)),
           pl.BlockSpec((TM_MIX, d), lambda j: (done_tile(j), 0)),
           pl.BlockSpec((1, d), lambda j: (0, gate[1]))],
        out_specs=pl.BlockSpec((TM_MIX, d), lambda j: (done_tile(j), 0)),
        out_shape=jax.ShapeDtypeStruct((s, d), F32),
        scratch_shapes=[h_tile, h_tile, slabs, slabs, rows, rows],
        compiler_params=_params(1),
        name="mix_wo",
    )(*outs, *lses, w_o, x, gate[0])


def kernel(x, c, positions, mod_w, mod_b, norm_mix_g, norm_ffn_g, conv_pw1_w, conv_pw1_b,
           conv_dw_w, conv_dw_b, conv_ln_g, conv_ln_b, conv_pw2_w, conv_pw2_b, kv_mod_w,
           kv_mod_b, kv_norm_g, w_kv, k_norm_g, w_q, q_norm_g, w_o, ffn_up_w, ffn_dw_w,
           ffn_dw_b, ffn_down_w):
    batch, s, d = x.shape
    assert (batch, s, d) == (1, SEQ, D_MODEL)
    x = x[0]
    c_col = c.reshape(d, 1)
    row = lambda v: v.reshape(1, -1)

    mod = _mod_matvec(c_col, mod_w, mod_b[:, None, :])
    kv_mod = _mod_matvec(c_col, kv_mod_w[None], kv_mod_b[None, None, :])[0]

    mvec = lambda l, q: (mod[l], q)
    ffn_dw_b3 = ffn_dw_b[:, None, :]

    conv_pw1_w, conv_pw2_w, w_o, ffn_up_w, ffn_down_w = (
        w.astype(BF16) for w in (conv_pw1_w, conv_pw2_w, w_o, ffn_up_w, ffn_down_w))
    w_qkv = jnp.concatenate([w_kv.astype(BF16), w_q[0].astype(BF16)], axis=1)

    def ffn(x, l):
        return _conv_ffn(x, row(norm_ffn_g[l]), mvec(l, 3), mvec(l, 4), mvec(l, 5), l,
                         ffn_up_w, ffn_dw_w, ffn_dw_b3, ffn_down_w)

    glu = _pw1_glu(x, row(norm_mix_g[0]), mvec(0, 0), mvec(0, 1),
                   conv_pw1_w[0], row(conv_pw1_b[0]))
    x = _conv_pw2(glu, conv_dw_w[0], row(conv_dw_b[0]), row(conv_ln_g[0]),
                  row(conv_ln_b[0]), conv_pw2_w[0], row(conv_pw2_b[0]), x, mvec(0, 2))
    x = ffn(x, 0)

    tables = _rope_tables(positions[0])
    head_gains = jnp.stack([k_norm_g, q_norm_g[0]])[:, None, :]
    kvq = _qkv(x, row(kv_norm_g), (kv_mod, 0), (kv_mod, 1),
               row(norm_mix_g[1]), mvec(1, 0), mvec(1, 1), w_qkv, head_gains, tables)
    outs, lses = zip(*[_band_attn(kvq[g], g) for g in range(N_GROUPS)])
    x = _mix_wo(outs, lses, w_o[0], x, mvec(1, 2))
    x = ffn(x, 1)
    return x[None]
```

```python
import functools
import math

import jax
import jax.numpy as jnp
from jax import lax
from jax.experimental import pallas as pl
from jax.experimental.pallas import tpu as pltpu

D_MODEL = 2048
SEQ = 8192
CONV_K = 31
FFN_CONV_K = 3
GROUP_DILATIONS = (1, 4, 16)
GROUP_SPANS = (128, 128, 128)
N_GROUPS = 3
HEADS_PER_GROUP = 8
HEAD_DIM = 128
Q_WIDTH = N_GROUPS * HEADS_PER_GROUP * HEAD_DIM
O_WIDTH = HEADS_PER_GROUP * HEAD_DIM
ROT_DIM = HEAD_DIM // 4
ROPE_THETA = 500000.0
BLK = 128
EPS = 1e-6
NEG = -1e30

V7X_LANES = 128
V7X_SUBLANES = 8
V7X_BF16_ROWS_PER_VREG = 16
V7X_VMEM_BYTES = 64 * 1024 * 1024
MIB = 1024 * 1024
VMEM_LIMIT = V7X_VMEM_BYTES - 8 * MIB

TM = 1024
TN = 512
TN_GLU = TN
ROW_CHUNK = 64

F32 = jnp.float32
BF16 = jnp.bfloat16


def _params(n_axes, vmem_limit=VMEM_LIMIT):
    return pltpu.CompilerParams(
        dimension_semantics=("arbitrary",) * n_axes,
        vmem_limit_bytes=vmem_limit)


def _sigmoid(x):
    return 1.0 / (1.0 + jnp.exp(-x))


def _silu(x):
    return x * _sigmoid(x)


def _lane_slab(slab):
    return slice(slab * V7X_LANES, (slab + 1) * V7X_LANES)


def _matvec_kernel(c_ref, w_ref, b_ref, o_ref, sb_ref, *, k_dim, tn):
    first = (pl.program_id(0) == 0) & (pl.program_id(1) == 0)

    @pl.when(first)
    def _():
        c = c_ref[...]
        sb_ref[...] = jnp.broadcast_to(_silu(c), (k_dim, V7X_LANES))

    n_groups = tn // V7X_LANES

    def body(t, accs):
        r0 = pl.multiple_of(t * ROW_CHUNK, ROW_CHUNK)
        s = sb_ref[pl.ds(r0, ROW_CHUNK), :]
        w = w_ref[pl.ds(r0, ROW_CHUNK), :]
        new = []
        for g in range(n_groups):
            p = w[:, _lane_slab(g)] * s
            a = accs[g]
            for u in range(ROW_CHUNK // V7X_SUBLANES):
                a = a + p[u * V7X_SUBLANES:(u + 1) * V7X_SUBLANES, :]
            new.append(a)
        return tuple(new)

    init = tuple(jnp.zeros((V7X_SUBLANES, V7X_LANES), F32) for _ in range(n_groups))
    accs = lax.fori_loop(0, k_dim // ROW_CHUNK, body, init)
    row = jnp.concatenate([jnp.sum(a, axis=0, keepdims=True) for a in accs], axis=1)
    o_ref[...] = row + b_ref[...]


def _mod_matvec(c_col, w, b):
    n_l, k_dim, n = w.shape
    tn = 1024
    return pl.pallas_call(
        functools.partial(_matvec_kernel, k_dim=k_dim, tn=tn),
        grid=(n_l, n // tn),
        in_specs=[
            pl.BlockSpec((k_dim, 1), lambda l, j: (0, 0)),
            pl.BlockSpec((None, k_dim, tn), lambda l, j: (l, 0, j)),
            pl.BlockSpec((None, 1, tn), lambda l, j: (l, 0, j)),
        ],
        out_specs=pl.BlockSpec((None, 1, tn), lambda l, j: (l, 0, j)),
        out_shape=jax.ShapeDtypeStruct((n_l, 1, n), F32),
        scratch_shapes=[pltpu.VMEM((k_dim, V7X_LANES), F32)],
        compiler_params=_params(2),
        name="mod_matvec",
    )(c_col, w, b)


MOD_ROWS = V7X_BF16_ROWS_PER_VREG
MOD_UNROLL = 4


def _modulate_rows(x_ref, targets, *, rows, h_row0=0, inline=False):
    def chunk(r0, h0):
        x = x_ref[pl.ds(r0, MOD_ROWS), :]
        ms = jnp.mean(x * x, axis=-1, keepdims=True)
        y = x * lax.rsqrt(ms + EPS)
        for g_ref, shift_ref, scale_ref, h_ref in targets:
            h = (y * g_ref[...]) * (1.0 + scale_ref[...]) + shift_ref[...]
            h_ref[pl.ds(h0, MOD_ROWS), :] = h.astype(BF16)

    trips = rows // MOD_ROWS
    if inline:
        for t in range(trips):
            chunk(t * MOD_ROWS, h_row0 + t * MOD_ROWS)
        return

    def body(t, carry):
        chunk(pl.multiple_of(t * MOD_ROWS, MOD_ROWS),
              pl.multiple_of(h_row0 + t * MOD_ROWS, MOD_ROWS))
        return carry

    lax.fori_loop(0, trips, body, 0, unroll=min(MOD_UNROLL, trips))


PW1_TILE = 512


def _pw1_glu_kernel(x_ref, g_ref, sh_ref, sc_ref, w_ref, b_ref, o_ref, hnew_ref, hprev_ref):
    s = pl.program_id(0)
    d = o_ref.shape[1]

    @pl.when(s == 0)
    def _():
        hnew_ref[...] = jnp.zeros(hnew_ref.shape, BF16)

    hprev_ref[...] = hnew_ref[...]
    h = hprev_ref[...]
    for c0 in range(0, d, TN_GLU):
        val, gate = slice(c0, c0 + TN_GLU), slice(d + c0, d + c0 + TN_GLU)
        a = jnp.dot(h, w_ref[:, val], preferred_element_type=F32) + b_ref[:, val]
        gt = jnp.dot(h, w_ref[:, gate], preferred_element_type=F32) + b_ref[:, gate]
        o_ref[:, val] = a * _sigmoid(gt)
    _modulate_rows(x_ref, [(g_ref, sh_ref, sc_ref, hnew_ref)], rows=PW1_TILE, inline=True)


def _pw1_glu(x, norm_g, shift, scale, w, b):
    s, d = x.shape
    n_tiles = s // PW1_TILE
    vec = lambda col: pl.BlockSpec((1, d), lambda j: (0, col))
    return pl.pallas_call(
        _pw1_glu_kernel,
        grid=(n_tiles + 1,),
        in_specs=[
            pl.BlockSpec((PW1_TILE, d), lambda j: (jnp.minimum(j, n_tiles - 1), 0)),
            pl.BlockSpec((1, d), lambda j: (0, 0)),
            vec(shift[1]), vec(scale[1]),
            pl.BlockSpec((d, 2 * d), lambda j: (0, 0), pipeline_mode=pl.Buffered(1)),
            pl.BlockSpec((1, 2 * d), lambda j: (0, 0)),
        ],
        out_specs=pl.BlockSpec((PW1_TILE, d), lambda j: (jnp.maximum(j - 1, 0), 0)),
        out_shape=jax.ShapeDtypeStruct((s, d), F32),
        scratch_shapes=[pltpu.VMEM((PW1_TILE, d), BF16), pltpu.VMEM((PW1_TILE, d), BF16)],
        compiler_params=_params(1),
        name="pw1_glu",
    )(x, norm_g, shift[0], scale[0], w, b)


CONV_HALO = 32
CONV_ROWS = 32
CONV_TILE = 256


def _dwconv_ln_chunk(r0, gbuf_ref, dw_ref, db_ref, lg_ref, lb_ref, tmp_ref, h_ref, *, d):
    off = CONV_HALO - (CONV_K - 1)
    for slab in range(d // V7X_LANES):
        lanes = _lane_slab(slab)
        acc = jnp.broadcast_to(db_ref[:, lanes], (CONV_ROWS, V7X_LANES))
        for k in range(CONV_K):
            tap = gbuf_ref[slab, pl.ds(r0 + off + k, CONV_ROWS, stride=1), :]
            acc = acc + tap * dw_ref[k:k + 1, lanes]
        tmp_ref[:, lanes] = acc
    u = tmp_ref[...]
    mu = jnp.mean(u, axis=-1, keepdims=True)
    uc = u - mu
    var = jnp.mean(uc * uc, axis=-1, keepdims=True)
    y = uc * lax.rsqrt(var + EPS) * lg_ref[...] + lb_ref[...]
    h_ref[pl.ds(r0, CONV_ROWS), :] = _silu(y).astype(BF16)


def _conv_pw2_kernel(g_ref, halo_ref, dw_ref, db_ref, lg_ref, lb_ref, w_ref, b_ref,
                     x_ref, gate_ref, o_ref, gbuf_ref, tmp_ref, hnew_ref, hprev_ref, *, d):
    s = pl.program_id(0)

    @pl.when(s == 0)
    def _():
        hnew_ref[...] = jnp.zeros(hnew_ref.shape, BF16)

    hprev_ref[...] = hnew_ref[...]
    y = jnp.dot(hprev_ref[...], w_ref[...], preferred_element_type=F32)
    o_ref[...] = x_ref[...] + gate_ref[...] * (y + b_ref[...])

    for slab in range(d // V7X_LANES):
        lanes = _lane_slab(slab)
        halo = halo_ref[:, lanes]
        gbuf_ref[slab, 0:CONV_HALO, :] = jnp.where(s == 0, jnp.zeros_like(halo), halo)
        gbuf_ref[slab, CONV_HALO:, :] = g_ref[:, lanes]
    for c in range(CONV_TILE // CONV_ROWS):
        _dwconv_ln_chunk(c * CONV_ROWS, gbuf_ref, dw_ref, db_ref, lg_ref, lb_ref,
                         tmp_ref.at[c % 2], hnew_ref, d=d)


def _conv_pw2(glu, dw_w, dw_b, ln_g, ln_b, w, b, x, gate):
    s, d = glu.shape
    n_tiles = s // CONV_TILE
    halo_blocks_per_tile = CONV_TILE // CONV_HALO
    conv_tile = lambda j: jnp.minimum(j, n_tiles - 1)
    done_tile = lambda j: jnp.maximum(j - 1, 0)
    row_vec = pl.BlockSpec((1, d), lambda j: (0, 0))
    return pl.pallas_call(
        functools.partial(_conv_pw2_kernel, d=d),
        grid=(n_tiles + 1,),
        in_specs=[
            pl.BlockSpec((CONV_TILE, d), lambda j: (conv_tile(j), 0)),
            pl.BlockSpec((CONV_HALO, d),
                         lambda j: (jnp.maximum(conv_tile(j) * halo_blocks_per_tile - 1, 0), 0)),
            pl.BlockSpec((CONV_K, d), lambda j: (0, 0)),
            row_vec, row_vec, row_vec,
            pl.BlockSpec((d, d), lambda j: (0, 0), pipeline_mode=pl.Buffered(1)),
            row_vec,
            pl.BlockSpec((CONV_TILE, d), lambda j: (done_tile(j), 0)),
            pl.BlockSpec((1, d), lambda j: (0, gate[1])),
        ],
        out_specs=pl.BlockSpec((CONV_TILE, d), lambda j: (done_tile(j), 0)),
        out_shape=jax.ShapeDtypeStruct((s, d), F32),
        scratch_shapes=[
            pltpu.VMEM((d // V7X_LANES, CONV_TILE + CONV_HALO, V7X_LANES), F32),
            pltpu.VMEM((2, CONV_ROWS, d), F32),
            pltpu.VMEM((CONV_TILE, d), BF16),
            pltpu.VMEM((CONV_TILE, d), BF16),
        ],
        compiler_params=_params(1),
        name="dwconv_pw2",
    )(glu, glu, dw_w, dw_b, ln_g, ln_b, w, b, x, gate[0])


FFN_HALO = V7X_BF16_ROWS_PER_VREG
TF = 512
FFN_OUT_LANES = 512
FFN_VMEM_LIMIT = V7X_VMEM_BYTES - 4 * MIB


def _ffn_kernel(x_ref, xh_ref, g_ref, sh_ref, sc_ref, gate_ref, wg_ref, wv_ref,
                dw_ref, db_ref, wd_ref, o_ref, h_ref, u_ref):
    i = pl.program_id(0)

    @pl.when(pl.program_id(1) == 0)
    def _():
        target = [(g_ref, sh_ref, sc_ref, h_ref)]
        _modulate_rows(xh_ref, target, rows=FFN_HALO)
        _modulate_rows(x_ref, target, rows=TM, h_row0=FFN_HALO)
        o_ref[...] = x_ref[...]

    w_up = jnp.concatenate([wg_ref[...], wv_ref[...]], axis=1)
    u = jnp.dot(h_ref[...], w_up, preferred_element_type=F32)
    u_ref[...] = u
    keep_halo = jnp.where(i == 0, 0.0, 1.0)
    u_ref[0:FFN_HALO, 0:TF] = u[0:FFN_HALO, 0:TF] * keep_halo

    gt = db_ref[...] + dw_ref[2:3, :] * u_ref[pl.ds(FFN_HALO, TM), 0:TF]
    gt = gt + dw_ref[1:2, :] * u_ref[pl.ds(FFN_HALO - 1, TM), 0:TF]
    gt = gt + dw_ref[0:1, :] * u_ref[pl.ds(FFN_HALO - 2, TM), 0:TF]
    val = u_ref[pl.ds(FFN_HALO, TM), TF:2 * TF]
    act = (_silu(gt) * val).astype(BF16)
    for c0 in range(0, o_ref.shape[1], FFN_OUT_LANES):
        cols = slice(c0, c0 + FFN_OUT_LANES)
        y = jnp.dot(act, wd_ref[:, cols], preferred_element_type=F32)
        o_ref[:, cols] += gate_ref[:, cols] * y


def _conv_ffn(x, norm_g, shift, scale, gate, layer, up_w, dw_w, dw_b, down_w):
    s, d = x.shape
    f = down_w.shape[1]
    nf = f // TF
    halo_blocks_per_tile = TM // FFN_HALO
    vec = lambda col: pl.BlockSpec((1, d), lambda i, j: (0, col))
    return pl.pallas_call(
        _ffn_kernel,
        grid=(s // TM, nf),
        in_specs=[
            pl.BlockSpec((TM, d), lambda i, j: (i, 0)),
            pl.BlockSpec((FFN_HALO, d),
                         lambda i, j: (jnp.maximum(i * halo_blocks_per_tile - 1, 0), 0)),
            pl.BlockSpec((1, d), lambda i, j: (0, 0)),
            vec(shift[1]), vec(scale[1]), vec(gate[1]),
            pl.BlockSpec((None, d, TF), lambda i, j: (layer, 0, j)),
            pl.BlockSpec((None, d, TF), lambda i, j: (layer, 0, nf + j)),
            pl.BlockSpec((None, FFN_CONV_K, TF), lambda i, j: (layer, 0, j)),
            pl.BlockSpec((None, 1, TF), lambda i, j: (layer, 0, j)),
            pl.BlockSpec((None, TF, d), lambda i, j: (layer, j, 0)),
        ],
        out_specs=pl.BlockSpec((TM, d), lambda i, j: (i, 0)),
        out_shape=jax.ShapeDtypeStruct((s, d), F32),
        scratch_shapes=[
            pltpu.VMEM((TM + FFN_HALO, d), BF16),
            pltpu.VMEM((TM + FFN_HALO, 2 * TF), F32),
        ],
        compiler_params=_params(2, FFN_VMEM_LIMIT),
        name="conv_ffn",
    )(x, x, norm_g, shift[0], scale[0], gate[0], up_w, up_w, dw_w, dw_b, down_w)


def _rope_table_kernel(pos_ref, freq_ref, cos_ref, sin_lo_ref, sin_hi_ref):
    pos = pos_ref[...].astype(F32)
    ang = pos * freq_ref[...]
    lane = lax.broadcasted_iota(jnp.int32, ang.shape, 1)
    c = jnp.cos(ang)
    sn = jnp.sin(ang)
    half = ROT_DIM // 2
    cos_ref[...] = jnp.where(lane < ROT_DIM, c, 1.0)
    sin_lo_ref[...] = jnp.where(lane < half, -sn, 0.0)
    sin_hi_ref[...] = jnp.where((lane >= half) & (lane < ROT_DIM), sn, 0.0)


def _rope_tables(positions):
    s = positions.shape[0]
    rows = 1024
    inv_freq = ROPE_THETA ** (-jnp.arange(0, ROT_DIM, 2, dtype=F32) / ROT_DIM)
    lane_freq = jnp.concatenate(
        [inv_freq, inv_freq, jnp.zeros((HEAD_DIM - ROT_DIM,), F32)])[None, :]
    out = jax.ShapeDtypeStruct((s, HEAD_DIM), F32)
    spec = pl.BlockSpec((rows, HEAD_DIM), lambda i: (i, 0))
    return pl.pallas_call(
        _rope_table_kernel,
        grid=(s // rows,),
        in_specs=[pl.BlockSpec((rows, 1), lambda i: (i, 0)),
                  pl.BlockSpec((1, HEAD_DIM), lambda i: (0, 0))],
        out_specs=[spec, spec, spec],
        out_shape=[out, out, out],
        compiler_params=_params(1),
        name="rope_tables",
    )(positions.reshape(s, 1), lane_freq)


TILES_PER_GROUP = O_WIDTH // TN


def _norm_rope_head(q, hg, cos, sin_lo, sin_hi):
    ms = jnp.mean(q * q, axis=-1, keepdims=True)
    qn = q * lax.rsqrt(ms + EPS) * hg
    hi_to_lo = pltpu.roll(qn, HEAD_DIM - ROT_DIM // 2, axis=1)
    lo_to_hi = pltpu.roll(qn, ROT_DIM // 2, axis=1)
    return qn * cos + hi_to_lo * sin_lo + lo_to_hi * sin_hi


SPLIT_STRIDE = 4


def _finish_tile(y_ref, out_ref, head_gain, use_rope, table_refs, y4_ref, *, r):
    rows = TM // r
    two_pass = r > SPLIT_STRIDE
    r_outer = r // SPLIT_STRIDE
    for head in range(TN // HEAD_DIM):
        cos, sin_lo, sin_hi = (t[...] for t in table_refs)
        y = y_ref[head]
        y_ref[head] = jnp.where(use_rope,
                                _norm_rope_head(y, head_gain, cos, sin_lo, sin_hi), y)
        if two_pass:
            for p in range(SPLIT_STRIDE):
                y4_ref[head % 2, p] = y_ref[head, pl.ds(p, TM // SPLIT_STRIDE,
                                                        stride=SPLIT_STRIDE), :]
        for res in range(r):
            if r == 1:
                y = y_ref[head]
            elif two_pass:
                p, q = res % SPLIT_STRIDE, res // SPLIT_STRIDE
                y = y4_ref[head % 2, p, pl.ds(q, rows, stride=r_outer), :]
            else:
                y = y_ref[head, pl.ds(res, rows, stride=r), :]
            out_ref[res, :, _lane_slab(head)] = y.astype(BF16)


N_KINDS = 3
K_KIND, V_KIND, Q_KIND = range(N_KINDS)
GROUP_TILES = N_KINDS * TILES_PER_GROUP
ROW_TILES = N_GROUPS * GROUP_TILES


def _qkv_kernel(x_ref, gkv_ref, shkv_ref, sckv_ref, gq_ref, shq_ref, scq_ref, w_ref,
                hg_ref, cos_ref, slo_ref, shi_ref, *refs):
    outs = refs[:N_GROUPS]
    h_ref, ya_ref, yb_ref, y4_ref = refs[N_GROUPS:]
    s = pl.program_id(0)
    t = TILES_PER_GROUP
    tile = jnp.minimum(s, pl.num_programs(0) - 2)
    col = tile % ROW_TILES
    done_col = jnp.maximum(s - 1, 0) % ROW_TILES
    done_kind = (done_col % GROUP_TILES) // t
    done_group = done_col // GROUP_TILES
    tables = (cos_ref, slo_ref, shi_ref)

    @pl.when(col == 0)
    def _():
        _modulate_rows(x_ref, [(gkv_ref, shkv_ref, sckv_ref, h_ref.at[0]),
                               (gq_ref, shq_ref, scq_ref, h_ref.at[1])], rows=TM)

    @pl.when(s == 0)
    def _():
        ya_ref[...] = jnp.zeros(ya_ref.shape, F32)

    h_sel = ((col % GROUP_TILES) // t == Q_KIND).astype(jnp.int32)
    gain = hg_ref[(done_kind == Q_KIND).astype(jnp.int32)]
    use_rope = done_kind != V_KIND
    for group, r in enumerate(GROUP_DILATIONS):
        @pl.when(done_group == group)
        def _(group=group, r=r):
            yb_ref[...] = ya_ref[...]
            y = jnp.dot(h_ref[h_sel], w_ref[...], preferred_element_type=F32)
            for head in range(TN // HEAD_DIM):
                ya_ref[head] = y[:, _lane_slab(head)]
            _finish_tile(yb_ref, outs[group], gain, use_rope, tables, y4_ref, r=r)


def _qkv(x, kv_norm_g, kv_shift, kv_scale, q_norm_g, q_shift, q_scale, w_all, head_gains,
         tables):
    s, d = x.shape
    t = TILES_PER_GROUP
    n_tiles = (s // TM) * ROW_TILES
    vec = lambda col: pl.BlockSpec((1, d), lambda j: (0, col))
    one = pl.BlockSpec((1, d), lambda j: (0, 0))

    def tile_of(j):
        return jnp.minimum(j, n_tiles - 1)

    def done_of(j):
        return jnp.maximum(j - 1, 0)

    def w_col(j):
        col = tile_of(j) % ROW_TILES
        group, rem = col // GROUP_TILES, col % GROUP_TILES
        return (rem // t) * (N_GROUPS * t) + group * t + rem % t

    def out_spec(group):
        r = GROUP_DILATIONS[group]
        return pl.BlockSpec(
            (r, TM // r, TN),
            lambda j: (0, done_of(j) // ROW_TILES,
                       jnp.clip(done_of(j) % ROW_TILES - group * GROUP_TILES,
                                0, GROUP_TILES - 1)))

    tab = pl.BlockSpec((TM, HEAD_DIM), lambda j: (done_of(j) // ROW_TILES, 0))
    return pl.pallas_call(
        _qkv_kernel,
        grid=(n_tiles + 1,),
        in_specs=[
            pl.BlockSpec((TM, d), lambda j: (tile_of(j) // ROW_TILES, 0)),
            one, vec(kv_shift[1]), vec(kv_scale[1]),
            one, vec(q_shift[1]), vec(q_scale[1]),
            pl.BlockSpec((d, TN), lambda j: (0, w_col(j))),
            pl.BlockSpec((2, 1, HEAD_DIM), lambda j: (0, 0, 0)),
            tab, tab, tab,
        ],
        out_specs=[out_spec(g) for g in range(N_GROUPS)],
        out_shape=[jax.ShapeDtypeStruct((r, s // r, N_KINDS * O_WIDTH), BF16)
                   for r in GROUP_DILATIONS],
        scratch_shapes=[
            pltpu.VMEM((2, TM, d), BF16),
            pltpu.VMEM((TN // V7X_LANES, TM, V7X_LANES), F32),
            pltpu.VMEM((TN // V7X_LANES, TM, V7X_LANES), F32),
            pltpu.VMEM((2, SPLIT_STRIDE, TM // SPLIT_STRIDE, V7X_LANES), F32),
        ],
        compiler_params=_params(1),
        name="qkv",
    )(x, kv_norm_g, kv_shift[0], kv_scale[0], q_norm_g, q_shift[0], q_scale[0],
      w_all, head_gains, *tables)


ATTN_Q_BLOCKS = 4


def _band_attn_kernel(q_ref, kp_ref, kc_ref, vp_ref, vc_ref, o_ref, lse_ref, *, span):
    n = pl.program_id(1)
    qi = lax.broadcasted_iota(jnp.int32, (BLK, 2 * BLK), 0)
    kj = lax.broadcasted_iota(jnp.int32, (BLK, 2 * BLK), 1)
    dist = qi + BLK - kj
    band = (dist >= 0) & (dist <= span)
    first_band = band & ((n > 0) | (kj >= BLK))
    lane = lax.broadcasted_iota(jnp.int32, (BLK, V7X_LANES), 1)
    scale = 1.0 / math.sqrt(HEAD_DIM)
    for blk in range(ATTN_Q_BLOCKS):
        rows = slice(blk * BLK, (blk + 1) * BLK)
        lse_tile = jnp.zeros((BLK, V7X_LANES), F32)
        for h in range(HEADS_PER_GROUP):
            cols = slice(h * HEAD_DIM, (h + 1) * HEAD_DIM)
            q = q_ref[rows, cols]
            if blk == 0:
                k = jnp.concatenate([kp_ref[:, cols], kc_ref[rows, cols]], axis=0)
                v = jnp.concatenate([vp_ref[:, cols], vc_ref[rows, cols]], axis=0)
                mask = first_band
            else:
                keys = slice((blk - 1) * BLK, (blk + 1) * BLK)
                k, v, mask = kc_ref[keys, cols], vc_ref[keys, cols], band
            sc = lax.dot_general(q, k, (((1,), (1,)), ((), ())),
                                 preferred_element_type=F32) * scale
            sc = jnp.where(mask, sc, NEG)
            m = jnp.max(sc, axis=-1, keepdims=True)
            e = jnp.exp(sc - m)
            l = jnp.sum(e, axis=-1, keepdims=True)
            p = (e * (1.0 / l)).astype(BF16)
            o_ref[rows, cols] = jnp.dot(p, v, preferred_element_type=F32)
            lse_tile = jnp.where(lane == h, m + jnp.log(l), lse_tile)
        lse_ref[rows, :] = lse_tile


def _band_attn(kvq, group):
    r, rows, _ = kvq.shape
    step_rows = ATTN_Q_BLOCKS * BLK

    def cur(col):
        return pl.BlockSpec((None, step_rows, O_WIDTH), lambda j, n: (j, n, col))

    def prev(col):
        return pl.BlockSpec((None, BLK, O_WIDTH),
                            lambda j, n: (j, jnp.maximum(n * ATTN_Q_BLOCKS - 1, 0), col))

    return pl.pallas_call(
        functools.partial(_band_attn_kernel, span=GROUP_SPANS[group]),
        grid=(r, rows // step_rows),
        in_specs=[cur(Q_KIND), prev(K_KIND), cur(K_KIND), prev(V_KIND), cur(V_KIND)],
        out_specs=[cur(0),
                   pl.BlockSpec((None, step_rows, V7X_LANES), lambda j, n: (j, n, 0))],
        out_shape=[jax.ShapeDtypeStruct((r, rows, O_WIDTH), F32),
                   jax.ShapeDtypeStruct((r, rows, V7X_LANES), F32)],
        compiler_params=_params(2),
        name=f"band_attn_r{r}",
    )(kvq, kvq, kvq, kvq, kvq)


TM_MIX = 256


def _rows_from_residues(src_ref, dst_ref, *, r, lanes=None):
    n = src_ref.shape[1]
    for res in range(r):
        rows = pl.ds(res, n, stride=r)
        if lanes is None:
            dst_ref[rows, :] = src_ref[res]
        else:
            for slab in range(lanes // V7X_LANES):
                dst_ref[slab, rows, :] = src_ref[res, :, _lane_slab(slab)]


def _mix_wo_kernel(o0_ref, o1_ref, o2_ref, l0_ref, l1_ref, l2_ref, w_ref, x_ref,
                   gate_ref, out_ref, hnew_ref, hprev_ref, on1_ref, on2_ref, ln1_ref,
                   ln2_ref):
    s = pl.program_id(0)

    @pl.when(s == 0)
    def _():
        hnew_ref[...] = jnp.zeros(hnew_ref.shape, BF16)

    hprev_ref[...] = hnew_ref[...]
    y = jnp.dot(hprev_ref[...], w_ref[...], preferred_element_type=F32)
    out_ref[...] = x_ref[...] + gate_ref[...] * y

    r1, r2 = GROUP_DILATIONS[1], GROUP_DILATIONS[2]
    _rows_from_residues(l1_ref, ln1_ref, r=r1)
    _rows_from_residues(l2_ref, ln2_ref, r=r2)
    _rows_from_residues(o1_ref, on1_ref, r=r1, lanes=O_WIDTH)
    _rows_from_residues(o2_ref, on2_ref, r=r2, lanes=O_WIDTH)
    l0, l1, l2 = l0_ref[0], ln1_ref[...], ln2_ref[...]
    m = jnp.maximum(jnp.maximum(l0, l1), l2)
    e0, e1, e2 = jnp.exp(l0 - m), jnp.exp(l1 - m), jnp.exp(l2 - m)
    inv = 1.0 / (e0 + e1 + e2)
    a0, a1, a2 = e0 * inv, e1 * inv, e2 * inv
    for h in range(HEADS_PER_GROUP):
        cols = _lane_slab(h)
        o = (a0[:, h:h + 1] * o0_ref[0, :, cols] + a1[:, h:h + 1] * on1_ref[h]
             + a2[:, h:h + 1] * on2_ref[h])
        hnew_ref[:, cols] = o.astype(BF16)


def _mix_wo(outs, lses, w_o, x, gate):
    s, d = x.shape
    n_tiles = s // TM_MIX
    mix_tile = lambda j: jnp.minimum(j, n_tiles - 1)
    done_tile = lambda j: jnp.maximum(j - 1, 0)

    def planes(width, r):
        return pl.BlockSpec((r, TM_MIX // r, width), lambda j: (0, mix_tile(j), 0))

    slabs = pltpu.VMEM((O_WIDTH // V7X_LANES, TM_MIX, V7X_LANES), F32)
    rows = pltpu.VMEM((TM_MIX, V7X_LANES), F32)
    h_tile = pltpu.VMEM((TM_MIX, O_WIDTH), BF16)
    return pl.pallas_call(
        _mix_wo_kernel,
        grid=(n_tiles + 1,),
        in_specs=[planes(O_WIDTH, r) for r in GROUP_DILATIONS]
        + [planes(V7X_LANES, r) for r in GROUP_DILATIONS]
        + [pl.BlockSpec((O_WIDTH, d), lambda j: (0, 0), pipeline_mode=pl.Buffered(1)),
           pl.BlockSpec((TM_MIX, d), lambda j: (done_tile(j), 0)),
           pl.BlockSpec((1, d), lambda j: (0, gate[1]))],
        out_specs=pl.BlockSpec((TM_MIX, d), lambda j: (done_tile(j), 0)),
        out_shape=jax.ShapeDtypeStruct((s, d), F32),
        scratch_shapes=[h_tile, h_tile, slabs, slabs, rows, rows],
        compiler_params=_params(1),
        name="mix_wo",
    )(*outs, *lses, w_o, x, gate[0])


def kernel(x, c, positions, mod_w, mod_b, norm_mix_g, norm_ffn_g, conv_pw1_w, conv_pw1_b,
           conv_dw_w, conv_dw_b, conv_ln_g, conv_ln_b, conv_pw2_w, conv_pw2_b, kv_mod_w,
           kv_mod_b, kv_norm_g, w_kv, k_norm_g, w_q, q_norm_g, w_o, ffn_up_w, ffn_dw_w,
           ffn_dw_b, ffn_down_w):
    batch, s, d = x.shape
    assert (batch, s, d) == (1, SEQ, D_MODEL)
    x = x[0]
    c_col = c.reshape(d, 1)
    row = lambda v: v.reshape(1, -1)

    mod = _mod_matvec(c_col, mod_w, mod_b[:, None, :])
    kv_mod = _mod_matvec(c_col, kv_mod_w[None], kv_mod_b[None, None, :])[0]

    mvec = lambda l, q: (mod[l], q)
    ffn_dw_b3 = ffn_dw_b[:, None, :]

    conv_pw1_w, conv_pw2_w, w_o, ffn_up_w, ffn_down_w = (
        w.astype(BF16) for w in (conv_pw1_w, conv_pw2_w, w_o, ffn_up_w, ffn_down_w))
    w_qkv = jnp.concatenate([w_kv.astype(BF16), w_q[0].astype(BF16)], axis=1)

    def ffn(x, l):
        return _conv_ffn(x, row(norm_ffn_g[l]), mvec(l, 3), mvec(l, 4), mvec(l, 5), l,
                         ffn_up_w, ffn_dw_w, ffn_dw_b3, ffn_down_w)

    glu = _pw1_glu(x, row(norm_mix_g[0]), mvec(0, 0), mvec(0, 1),
                   conv_pw1_w[0], row(conv_pw1_b[0]))
    x = _conv_pw2(glu, conv_dw_w[0], row(conv_dw_b[0]), row(conv_ln_g[0]),
                  row(conv_ln_b[0]), conv_pw2_w[0], row(conv_pw2_b[0]), x, mvec(0, 2))
    x = ffn(x, 0)

    tables = _rope_tables(positions[0])
    head_gains = jnp.stack([k_norm_g, q_norm_g[0]])[:, None, :]
    kvq = _qkv(x, row(kv_norm_g), (kv_mod, 0), (kv_mod, 1),
               row(norm_mix_g[1]), mvec(1, 0), mvec(1, 1), w_qkv, head_gains, tables)
    outs, lses = zip(*[_band_attn(kvq[g], g) for g in range(N_GROUPS)])
    x = _mix_wo(outs, lses, w_o[0], x, mvec(1, 2))
    x = ffn(x, 1)
    return x[None]
```
